```python
import jax
import jax.numpy as jnp
from jax import lax
import numpy as np

D_MODEL = 1024
BATCH = 16
SEQ = 2048
DEPTH = 2

GRID_W = 64
CTX_LEN = 256
HEAD_DIM = 64
MLSTM_HEADS = D_MODEL // 256
NA_HEADS = D_MODEL // 128
CONV_CH = D_MODEL // 4
MLSTM_W = MLSTM_HEADS * HEAD_DIM
NA_W = NA_HEADS * HEAD_DIM
MIX_W = MLSTM_W + NA_W + CONV_CH
MLSTM_CHUNK = 64
NA_WIN_R = 8
NA_WIN_C = 16
CONV_WIDTH = 31
ROPE_BASE = 10000.0
N_EXPERTS = 32
TOP_K = 4
D_EXPERT = D_MODEL
SWIGLU_LIMIT = 7.0
SWIGLU_ALPHA = 1.702
MOE_BLOCK = 256
EPS = 1e-6

A_Q = 0
A_K = A_Q + MLSTM_W
A_V = A_K + MLSTM_W
A_O = A_V + MLSTM_W
A_G = A_O + MLSTM_W
B_Q = A_G + 4 * MLSTM_HEADS
B_K = B_Q + NA_W
B_V = B_K + NA_W
C_A = B_V + NA_W
C_G = C_A + CONV_CH
IN_COLS = C_G + CONV_CH

kernel_name = 'hymba_mlstm_natten_conformer_moe_dit'


def _rms(x, w):
    xf = x.astype(jnp.float32)
    y = xf * lax.rsqrt(jnp.mean(xf * xf, axis=-1, keepdims=True) + EPS)
    return (y * w.astype(jnp.float32)).astype(x.dtype)


def _layer_norm(x, w, b):
    xf = x.astype(jnp.float32)
    xc = xf - jnp.mean(xf, axis=-1, keepdims=True)
    y = xc * lax.rsqrt(jnp.mean(xc * xc, axis=-1, keepdims=True) + EPS)
    return (y * w.astype(jnp.float32) + b.astype(jnp.float32)).astype(x.dtype)


def _modulate(x, norm_w, shift, scale):
    return _rms(x, norm_w) * (1.0 + scale) + shift


def _rope_2d(x, rows, cols):
    half = HEAD_DIM // 2
    quarter = half // 2
    inv_freq = ROPE_BASE ** (-jnp.arange(quarter, dtype=jnp.float32) / quarter)

    def rot(xp, pos):
        ang = pos.astype(jnp.float32)[:, None] * inv_freq[None, :]
        cos = jnp.cos(ang)[:, None, :]
        sin = jnp.sin(ang)[:, None, :]
        x1, x2 = xp[..., :quarter], xp[..., quarter:]
        return jnp.concatenate([x1 * cos - x2 * sin, x1 * sin + x2 * cos], axis=-1)

    return jnp.concatenate([rot(x[..., :half], rows), rot(x[..., half:], cols)], axis=-1)


def _mlstm_chunkwise(q, k, v, ig, lf, state, with_output):
    Z, B, H, T, d = q.shape
    nc = T // MLSTM_CHUNK

    def chunks(a):
        a = a.reshape(a.shape[:3] + (nc, MLSTM_CHUNK) + a.shape[4:])
        return jnp.moveaxis(a, 3, 0)

    tril = jnp.tril(jnp.ones((MLSTM_CHUNK, MLSTM_CHUNK), dtype=bool))

    def step(carry, xs):
        C, n, m = carry
        qc, kc, vc, ic, fc = xs
        b = jnp.cumsum(fc, axis=-1)
        bL = b[..., -1]
        logw = bL[..., None] - b + ic
        m_new = jnp.maximum(bL + m, jnp.max(logw, axis=-1))
        w = jnp.exp(logw - m_new[..., None])
        decay = jnp.exp(bL + m - m_new)
        C_new = decay[..., None, None] * C + jnp.einsum('zbhs,zbhsk,zbhsv->zbhkv', w, kc, vc)
        n_new = decay[..., None] * n + jnp.einsum('zbhs,zbhsk->zbhk', w, kc)
        if not with_output:
            return (C_new, n_new, m_new), None
        logD = jnp.where(tril, b[..., :, None] - b[..., None, :] + ic[..., None, :], -jnp.inf)
        inter = b + m[..., None]
        m_t = jnp.maximum(inter, jnp.max(logD, axis=-1))
        s = jnp.einsum('zbhtd,zbhsd->zbhts', qc, kc) * jnp.exp(logD - m_t[..., None])
        w_inter = jnp.exp(inter - m_t)
        num = jnp.einsum('zbhts,zbhsd->zbhtd', s, vc) + w_inter[..., None] * jnp.einsum('zbhtk,zbhkv->zbhtv', qc, C)
        den = jnp.sum(s, axis=-1) + w_inter * jnp.einsum('zbhtk,zbhk->zbht', qc, n)
        h = num / jnp.maximum(jnp.abs(den), jnp.exp(-m_t))[..., None]
        return (C_new, n_new, m_new), h

    xs = (chunks(q), chunks(k), chunks(v), chunks(ig), chunks(lf))
    state, hs = lax.scan(step, state, xs)
    if not with_output:
        return None, state
    return jnp.moveaxis(hs, 0, 3).reshape(Z, B, H, T, d), state


def _mlstm_prep(P, ig_b, fg_b, pos):
    B, T = P.shape[:2]

    def heads(lo, hi):
        return P[..., lo:hi].astype(jnp.float32).reshape(B, T, MLSTM_HEADS, HEAD_DIM)

    q = heads(A_Q, A_K)
    k = heads(A_K, A_V) * (HEAD_DIM ** -0.5)
    v = heads(A_V, A_O)
    if pos is not None:
        q = _rope_2d(q, pos[0], pos[1])
        k = _rope_2d(k, pos[0], pos[1])
    g = P[..., A_G:B_Q].astype(jnp.float32).reshape(B, T, 4, MLSTM_HEADS)
    ig = g[:, :, 0:2] + ig_b.astype(jnp.float32)
    lf = jax.nn.log_sigmoid(g[:, :, 2:4] + fg_b.astype(jnp.float32))

    def dirs(a):
        a = jnp.swapaxes(a, 1, 2)
        return jnp.stack([a, jnp.flip(a, axis=2)])

    def gdirs(a):
        a = jnp.transpose(a, (2, 0, 3, 1))
        return jnp.stack([a[0], jnp.flip(a[1], axis=-1)])

    return dirs(q), dirs(k), dirs(v), gdirs(ig), gdirs(lf)


def _mlstm_merge(h, P, norm_w):
    B, T = P.shape[:2]
    hs = jnp.swapaxes(h[0] + jnp.flip(h[1], axis=2), 1, 2)
    hs = hs * lax.rsqrt(jnp.mean(hs * hs, axis=-1, keepdims=True) + EPS) * norm_w.astype(jnp.float32).reshape(MLSTM_HEADS, HEAD_DIM)
    o = jax.nn.sigmoid(P[..., A_O:A_G].astype(jnp.float32)).reshape(B, T, MLSTM_HEADS, HEAD_DIM)
    return (o * hs).reshape(B, T, MLSTM_W).astype(P.dtype)


def _mlstm_mixer(P_lat, P_ctx, pos, ig_b, fg_b, norm_w, need_ctx):
    qc, kc, vc, ic, fc = _mlstm_prep(P_ctx, ig_b, fg_b, None)
    Z, B, H, _, d = qc.shape
    state0 = (jnp.zeros((Z, B, H, d, d), jnp.float32), jnp.zeros((Z, B, H, d), jnp.float32), jnp.zeros((Z, B, H), jnp.float32))
    h_ctx, state_ctx = _mlstm_chunkwise(qc, kc, vc, ic, fc, state0, need_ctx)
    ql, kl, vl, il, fl = _mlstm_prep(P_lat, ig_b, fg_b, pos)
    h_lat, _ = _mlstm_chunkwise(ql, kl, vl, il, fl, state_ctx, True)
    out_ctx = _mlstm_merge(h_ctx, P_ctx, norm_w) if need_ctx else None
    return _mlstm_merge(h_lat, P_lat, norm_w), out_ctx


def _na_mixer(P_lat, P_ctx, qn_w, kn_w, rpb, need_ctx):
    B, S = P_lat.shape[:2]
    L = P_ctx.shape[1]
    scale = HEAD_DIM ** -0.5

    def heads(P, lo, hi):
        return P[..., lo:hi].reshape(P.shape[0], P.shape[1], NA_HEADS, HEAD_DIM)

    q = _rms(heads(P_lat, B_Q, B_K), qn_w) * scale
    k = _rms(heads(P_lat, B_K, B_V), kn_w)
    v = heads(P_lat, B_V, C_A)
    kc = _rms(heads(P_ctx, B_K, B_V), kn_w)
    vc = heads(P_ctx, B_V, C_A)
    out_ctx = None
    if need_ctx:
        qc = _rms(heads(P_ctx, B_Q, B_K), qn_w) * scale
        p = jax.nn.softmax(jnp.einsum('blhd,bmhd->bhlm', qc, kc).astype(jnp.float32), axis=-1).astype(vc.dtype)
        out_ctx = jnp.einsum('bhlm,bmhd->blhd', p, vc).reshape(B, L, NA_W)

    R = S // GRID_W
    KR = min(NA_WIN_R, R)

    def grid(a):
        return a.reshape(B, R, GRID_W, NA_HEADS, HEAD_DIM)

    kg, vg = grid(k), grid(v)
    q_rows = jnp.moveaxis(grid(q), 1, 0)
    col = jnp.arange(GRID_W)
    col_start = jnp.clip(col - NA_WIN_C // 2, 0, GRID_W - NA_WIN_C)
    in_win = (col[None, :] >= col_start[:, None]) & (col[None, :] < col_start[:, None] + NA_WIN_C)
    dc_idx = jnp.clip(col[None, :] - col[:, None] + NA_WIN_C - 1, 0, 2 * NA_WIN_C - 2)
    rpb_t = jnp.transpose(rpb[:, :, dc_idx], (0, 2, 1, 3)).astype(jnp.float32)

    def row_block(inp):
        r, q_r = inp
        r0 = jnp.clip(r - KR // 2, 0, R - KR)
        k_win = lax.dynamic_slice_in_dim(kg, r0, KR, axis=1)
        v_win = lax.dynamic_slice_in_dim(vg, r0, KR, axis=1)
        dr_idx = r0 + jnp.arange(KR) - r + NA_WIN_R - 1
        s_loc = jnp.einsum('bqhd,brkhd->bhqrk', q_r, k_win).astype(jnp.float32) + rpb_t[:, :, dr_idx][None]
        s_loc = jnp.where(in_win[:, None, :], s_loc, -jnp.inf)
        s_ctx = jnp.einsum('bqhd,blhd->bhql', q_r, kc).astype(jnp.float32)
        s_all = jnp.concatenate([s_loc.reshape(B, NA_HEADS, GRID_W, KR * GRID_W), s_ctx], axis=-1)
        p = jax.nn.softmax(s_all, axis=-1).astype(v.dtype)
        p_loc = p[..., :KR * GRID_W].reshape(B, NA_HEADS, GRID_W, KR, GRID_W)
        p_ctx = p[..., KR * GRID_W:]
        return jnp.einsum('bhqrk,brkhd->bqhd', p_loc, v_win) + jnp.einsum('bhql,blhd->bqhd', p_ctx, vc)

    out = lax.map(row_block, (jnp.arange(R, dtype=jnp.int32), q_rows))
    return jnp.moveaxis(out, 0, 1).reshape(B, S, NA_W), out_ctx


def _conv_module(P, conv_w, conv_b, ln_w, ln_b):
    u = P[..., C_A:C_G] * jax.nn.sigmoid(P[..., C_G:IN_COLS])
    y = lax.conv_general_dilated(u, conv_w[:, None, :], window_strides=(1,), padding='SAME',
                                 dimension_numbers=('NWC', 'WIO', 'NWC'), feature_group_count=CONV_CH) + conv_b
    y = _layer_norm(y, ln_w, ln_b)
    return y * jax.nn.sigmoid(y)


def _moe(xt, router_w, router_b, w1, b1, w2, b2):
    N, D = xt.shape
    NK = N * TOP_K
    logits = (xt @ router_w + router_b).astype(jnp.float32)
    top_val, top_idx = lax.top_k(logits, TOP_K)
    gates = jax.nn.softmax(top_val, axis=-1)
    flat_e = top_idx.reshape(-1)
    flat_tok = jnp.repeat(jnp.arange(N, dtype=jnp.int32), TOP_K)
    flat_g = gates.reshape(-1)
    order = jnp.argsort(flat_e)
    se, stok, sg = flat_e[order], flat_tok[order], flat_g[order]
    counts = jnp.bincount(flat_e, length=N_EXPERTS)
    padded = (counts + MOE_BLOCK - 1) // MOE_BLOCK * MOE_BLOCK
    pad_end = jnp.cumsum(padded)
    pad_start = pad_end - padded
    start = jnp.cumsum(counts) - counts
    dest = pad_start[se] + jnp.arange(NK, dtype=jnp.int32) - start[se]
    n_blocks = -(-(NK + N_EXPERTS * (MOE_BLOCK - 1)) // MOE_BLOCK)
    P = n_blocks * MOE_BLOCK
    row_tok = jnp.full((P,), N, jnp.int32).at[dest].set(stok)
    row_g = jnp.zeros((P,), jnp.float32).at[dest].set(sg)
    block_e = jnp.minimum(jnp.searchsorted(pad_end, jnp.arange(n_blocks, dtype=jnp.int32) * MOE_BLOCK, side='right'), N_EXPERTS - 1)
    x_pad = jnp.concatenate([xt, jnp.zeros((1, D), xt.dtype)], axis=0)

    def block(y, inp):
        tok, g, e = inp
        h = x_pad[tok] @ w1[e] + b1[e]
        glu = jnp.minimum(h[:, :D_EXPERT], SWIGLU_LIMIT)
        lin = jnp.clip(h[:, D_EXPERT:], -SWIGLU_LIMIT, SWIGLU_LIMIT)
        act = (lin + 1.0) * glu * jax.nn.sigmoid(SWIGLU_ALPHA * glu)
        out = act @ w2[e] + b2[e]
        return y.at[tok].add(g[:, None] * out.astype(jnp.float32)), None

    y, _ = lax.scan(block, jnp.zeros((N + 1, D), jnp.float32),
                    (row_tok.reshape(n_blocks, MOE_BLOCK), row_g.reshape(n_blocks, MOE_BLOCK), block_e))
    return y[:N].astype(xt.dtype)


def _layer(x, ctx, c, c_ctx, norm_mix_w, norm_ffn_w, w_ada, b_ada, w_in, mlstm_ig_b, mlstm_fg_b, mlstm_norm_w,
           na_qnorm_w, na_knorm_w, na_rpb, conv_w, conv_b, conv_ln_w, conv_ln_b, w_out, router_w, router_b,
           exp_w1, exp_b1, exp_w2, exp_b2, pos, need_ctx):
    B, S, D = x.shape
    L = ctx.shape[1]
    sh1, sc1, g1, sh2, sc2, g2 = [m[:, None, :] for m in jnp.split(jax.nn.silu(c) @ w_ada + b_ada, 6, axis=-1)]
    csh1, csc1, cg1, csh2, csc2, cg2 = jnp.split(jax.nn.silu(c_ctx) @ w_ada + b_ada, 6, axis=-1)
    P = _modulate(x, norm_mix_w, sh1, sc1) @ w_in
    Pc = _modulate(ctx, norm_mix_w, csh1, csc1) @ w_in
    a_x, a_c = _mlstm_mixer(P, Pc, pos, mlstm_ig_b, mlstm_fg_b, mlstm_norm_w, need_ctx)
    b_x, b_c = _na_mixer(P, Pc, na_qnorm_w, na_knorm_w, na_rpb, need_ctx)
    c_x = _conv_module(P, conv_w, conv_b, conv_ln_w, conv_ln_b)
    x = x + g1 * (jnp.concatenate([a_x, b_x, c_x], axis=-1) @ w_out)
    hx = _modulate(x, norm_ffn_w, sh2, sc2)
    if need_ctx:
        c_c = _conv_module(Pc, conv_w, conv_b, conv_ln_w, conv_ln_b)
        ctx = ctx + cg1 * (jnp.concatenate([a_c, b_c, c_c], axis=-1) @ w_out)
        hc = _modulate(ctx, norm_ffn_w, csh2, csc2)
        y = _moe(jnp.concatenate([hx.reshape(B * S, D), hc.reshape(B * L, D)], axis=0),
                 router_w, router_b, exp_w1, exp_b1, exp_w2, exp_b2)
        x = x + g2 * y[:B * S].reshape(B, S, D)
        ctx = ctx + cg2 * y[B * S:].reshape(B, L, D)
    else:
        x = x + g2 * _moe(hx.reshape(B * S, D), router_w, router_b, exp_w1, exp_b1, exp_w2, exp_b2).reshape(B, S, D)
    return x, ctx


def setup_inputs(seed: int = 0) -> dict:
    key = jax.random.key(seed)
    ks = jax.random.split(key, 26)
    nrm = jax.random.normal
    f32 = jnp.float32
    D = D_MODEL
    return {
        'x': nrm(ks[0], (BATCH, SEQ, D), f32),
        'c': nrm(ks[1], (BATCH, D), f32),
        'ctx': nrm(ks[2], (BATCH, CTX_LEN, D), f32),
        'c_ctx': nrm(ks[3], (D,), f32),
        'norm_mix_w': 1.0 + 0.02 * nrm(ks[4], (DEPTH, D), f32),
        'norm_ffn_w': 1.0 + 0.02 * nrm(ks[5], (DEPTH, D), f32),
        'w_ada': nrm(ks[6], (DEPTH, D, 6 * D), f32) * (0.5 * D ** -0.5),
        'b_ada': 0.02 * nrm(ks[7], (DEPTH, 6 * D), f32),
        'w_in': nrm(ks[8], (DEPTH, D, IN_COLS), f32) * D ** -0.5,
        'mlstm_ig_b': 0.1 * nrm(ks[9], (DEPTH, 2, MLSTM_HEADS), f32),
        'mlstm_fg_b': jnp.linspace(3.0, 6.0, MLSTM_HEADS, dtype=f32)[None, None, :] + 0.1 * nrm(ks[10], (DEPTH, 2, MLSTM_HEADS), f32),
        'mlstm_norm_w': 1.0 + 0.02 * nrm(ks[11], (DEPTH, MLSTM_W), f32),
        'na_qnorm_w': 1.0 + 0.02 * nrm(ks[12], (DEPTH, HEAD_DIM), f32),
        'na_knorm_w': 1.0 + 0.02 * nrm(ks[13], (DEPTH, HEAD_DIM), f32),
        'na_rpb': 0.02 * nrm(ks[14], (DEPTH, NA_HEADS, 2 * NA_WIN_R - 1, 2 * NA_WIN_C - 1), f32),
        'conv_w': nrm(ks[15], (DEPTH, CONV_WIDTH, CONV_CH), f32) * CONV_WIDTH ** -0.5,
        'conv_b': 0.02 * nrm(ks[16], (DEPTH, CONV_CH), f32),
        'conv_ln_w': 1.0 + 0.02 * nrm(ks[17], (DEPTH, CONV_CH), f32),
        'conv_ln_b': 0.02 * nrm(ks[18], (DEPTH, CONV_CH), f32),
        'w_out': nrm(ks[19], (DEPTH, MIX_W, D), f32) * MIX_W ** -0.5,
        'router_w': nrm(ks[20], (DEPTH, D, N_EXPERTS), f32) * D ** -0.5,
        'router_b': 0.01 * nrm(ks[21], (DEPTH, N_EXPERTS), f32),
        'exp_w1': nrm(ks[22], (DEPTH, N_EXPERTS, D, 2 * D_EXPERT), f32) * D ** -0.5,
        'exp_b1': 0.02 * nrm(ks[23], (DEPTH, N_EXPERTS, 2 * D_EXPERT), f32),
        'exp_w2': nrm(ks[24], (DEPTH, N_EXPERTS, D_EXPERT, D), f32) * D_EXPERT ** -0.5,
        'exp_b2': 0.02 * nrm(ks[25], (DEPTH, N_EXPERTS, D), f32),
    }


def reference(x, c, ctx, c_ctx, norm_mix_w, norm_ffn_w, w_ada, b_ada, w_in, mlstm_ig_b, mlstm_fg_b, mlstm_norm_w,
              na_qnorm_w, na_knorm_w, na_rpb, conv_w, conv_b, conv_ln_w, conv_ln_b, w_out, router_w, router_b,
              exp_w1, exp_b1, exp_w2, exp_b2):
    S = x.shape[1]
    t = jnp.arange(S, dtype=jnp.int32)
    pos = (t // GRID_W, t % GRID_W)
    for l in range(DEPTH):
        x, ctx = _layer(x, ctx, c, c_ctx, norm_mix_w[l], norm_ffn_w[l], w_ada[l], b_ada[l], w_in[l],
                        mlstm_ig_b[l], mlstm_fg_b[l], mlstm_norm_w[l], na_qnorm_w[l], na_knorm_w[l], na_rpb[l],
                        conv_w[l], conv_b[l], conv_ln_w[l], conv_ln_b[l], w_out[l], router_w[l], router_b[l],
                        exp_w1[l], exp_b1[l], exp_w2[l], exp_b2[l], pos, l < DEPTH - 1)
    return x
```

```python
import functools

import numpy as np
import jax
import jax.numpy as jnp
from jax import lax
from jax.experimental import pallas as pl
from jax.experimental.pallas import tpu as pltpu

F32 = jnp.float32
BF16 = jnp.bfloat16
MXU_DTYPE = BF16

GRID_W = 64
HEAD_DIM = 64
MLSTM_HEADS = 4
NA_HEADS = 8
CONV_CH = 256
MLSTM_W = MLSTM_HEADS * HEAD_DIM
NA_W = NA_HEADS * HEAD_DIM
NA_WIN_R = 8
NA_WIN_C = 16
CONV_WIDTH = 31
ROPE_BASE = 10000.0
N_EXPERTS = 32
TOP_K = 4
SWIGLU_LIMIT = 7.0
SWIGLU_ALPHA = 1.702
EPS = 1e-6

A_Q = 0
A_G = 4 * MLSTM_W
B_Q = A_G + 4 * MLSTM_HEADS
C_A = B_Q + 3 * NA_W
IN_COLS = C_A + 2 * CONV_CH

LANES = 128
NEG_BIG = -1e30
VMEM_LIMIT = 56 * 1024 * 1024

NA_QROWS = 4
NA_KROWS = NA_QROWS + NA_WIN_R - 1
TOK_TILE = 512
MOE_BLOCK = 512
COMBINE_TILE = 256
CONV_ROWS = 64
CONV_PAD = 16


def _mm(a, b):
    return jnp.dot(a.astype(MXU_DTYPE), b.astype(MXU_DTYPE), preferred_element_type=F32)


def _mm_nt(a, b):
    return lax.dot_general(a.astype(MXU_DTYPE), b.astype(MXU_DTYPE), (((1,), (1,)), ((), ())),
                           preferred_element_type=F32)


def _mm_tn(a, b):
    return lax.dot_general(a.astype(MXU_DTYPE), b.astype(MXU_DTYPE), (((0,), (0,)), ((), ())),
                           preferred_element_type=F32)


def _cparams(sem):
    return pltpu.CompilerParams(dimension_semantics=sem, vmem_limit_bytes=VMEM_LIMIT)


def _ada_kernel(c_ref, w_ref, b_ref, o_ref):
    cc = c_ref[...]
    s = cc * jax.nn.sigmoid(cc)
    o_ref[0] = _mm(s, w_ref[0]) + b_ref[0]


def _ada(cc, w_ada, b_ada):
    depth, d, n = w_ada.shape
    rows = cc.shape[0]
    tn = 512
    return pl.pallas_call(
        _ada_kernel,
        grid=(depth, n // tn),
        in_specs=[pl.BlockSpec((rows, d), lambda l, j: (0, 0)),
                  pl.BlockSpec((1, d, tn), lambda l, j: (l, 0, j)),
                  pl.BlockSpec((1, 1, tn), lambda l, j: (l, 0, j))],
        out_specs=pl.BlockSpec((1, rows, tn), lambda l, j: (l, 0, j)),
        out_shape=jax.ShapeDtypeStruct((depth, rows, n), F32),
        compiler_params=_cparams(("parallel", "parallel")),
        name="ada_mod",
    )(cc, w_ada, b_ada.reshape(depth, 1, n))


def _inproj_kernel(x_ref, sh_ref, sc_ref, nw_ref, wa_ref, wg_ref, wgt_ref, wb_ref, wc_ref,
                   pa_ref, g_ref, gt_ref, pb_ref, u_ref):
    x = x_ref[...]
    ms = jnp.mean(x * x, axis=-1, keepdims=True)
    y = x * lax.rsqrt(ms + EPS) * nw_ref[...]
    h = (y * (1.0 + sc_ref[0]) + sh_ref[0]).astype(MXU_DTYPE)
    pa_ref[...] = _mm(h, wa_ref[...]).astype(pa_ref.dtype)
    g_ref[...] = _mm(h, wg_ref[...])
    gt_ref[...] = _mm_nt(wgt_ref[...], h)
    pb_ref[...] = _mm(h, wb_ref[...]).astype(pb_ref.dtype)
    pc = _mm(h, wc_ref[...])
    u_ref[...] = (pc[:, :CONV_CH] * jax.nn.sigmoid(pc[:, CONV_CH:])).astype(u_ref.dtype)


def _inproj(xall, shift, scale, norm_w, w_in, n_lat, seq):
    n, d = xall.shape
    tm = TOK_TILE
    n_lat_tiles = n_lat // tm
    per_batch = seq // tm
    n_mod = shift.shape[0]

    def mod_map(i):
        return (jnp.where(i < n_lat_tiles, i // per_batch, n_mod - 1), 0, 0)

    wa = w_in[:, A_Q:A_G].astype(MXU_DTYPE)
    wg = w_in[:, A_G:B_Q].astype(MXU_DTYPE)
    wb = w_in[:, B_Q:C_A].astype(MXU_DTYPE)
    wc = w_in[:, C_A:IN_COLS].astype(MXU_DTYPE)
    ng = B_Q - A_G
    full = lambda r, c: pl.BlockSpec((r, c), lambda i: (0, 0))
    return pl.pallas_call(
        _inproj_kernel,
        grid=(n // tm,),
        in_specs=[pl.BlockSpec((tm, d), lambda i: (i, 0)),
                  pl.BlockSpec((1, 1, d), mod_map),
                  pl.BlockSpec((1, 1, d), mod_map),
                  full(1, d),
                  full(d, A_G), full(d, ng), full(ng, d), full(d, 3 * NA_W), full(d, 2 * CONV_CH)],
        out_specs=[pl.BlockSpec((tm, A_G), lambda i: (i, 0)),
                   pl.BlockSpec((tm, ng), lambda i: (i, 0)),
                   pl.BlockSpec((ng, tm), lambda i: (0, i)),
                   pl.BlockSpec((tm, 3 * NA_W), lambda i: (i, 0)),
                   pl.BlockSpec((tm, CONV_CH), lambda i: (i, 0))],
        out_shape=[jax.ShapeDtypeStruct((n, A_G), BF16),
                   jax.ShapeDtypeStruct((n, ng), F32),
                   jax.ShapeDtypeStruct((ng, n), F32),
                   jax.ShapeDtypeStruct((n, 3 * NA_W), BF16),
                   jax.ShapeDtypeStruct((n, CONV_CH), BF16)],
        compiler_params=_cparams(("parallel",)),
        name="in_proj",
    )(xall, shift, scale, norm_w.reshape(1, d), wa, wg, wg.T, wb, wc)


def _split3(x):
    hi = x.astype(BF16)
    r1 = x - hi.astype(F32)
    mid = r1.astype(BF16)
    lo = (r1 - mid.astype(F32)).astype(BF16)
    return hi, mid, lo


def _tri_left(tri, x):
    return sum(jnp.dot(tri, p, preferred_element_type=F32) for p in _split3(x))


def _tri_right(x, tri):
    return sum(jnp.dot(p, tri, preferred_element_type=F32) for p in _split3(x))


def _log_sigmoid(x):
    return jnp.minimum(x, 0.0) - jnp.log(1.0 + jnp.exp(-jnp.abs(x)))


def _mlstm_direction(z, q, k, v, gcol, grow, c_st, m_st, h_ref, row0, lc, with_output):
    nh = MLSTM_HEADS
    ti = lax.broadcasted_iota(jnp.int32, (lc, lc), 0)
    si = lax.broadcasted_iota(jnp.int32, (lc, lc), 1)
    lower = si <= ti
    upper = si >= ti
    tl = jnp.where(lower, 1.0, 0.0).astype(BF16)
    tu = jnp.where(upper, 1.0, 0.0).astype(BF16)
    i_col = gcol[:, z * nh:(z + 1) * nh]
    f_col = _log_sigmoid(gcol[:, 2 * nh + z * nh:2 * nh + (z + 1) * nh])
    i_row = grow[z * nh:(z + 1) * nh, :]
    f_row = _log_sigmoid(grow[2 * nh + z * nh:2 * nh + (z + 1) * nh, :])
    if z == 0:
        b_col = _tri_left(tl, f_col)
        b_row = _tri_right(f_row, tu)
        b_tot = b_col[lc - 1:lc, :]
        mask = lower
    else:
        b_col = _tri_left(tu, f_col)
        b_row = _tri_right(f_row, tl)
        b_tot = b_col[0:1, :]
        mask = upper
    lane256 = lax.broadcasted_iota(jnp.int32, (1, 4 * HEAD_DIM), 1)
    lane128 = lax.broadcasted_iota(jnp.int32, (1, LANES), 1)
    k_mx = k.astype(MXU_DTYPE)
    c_all = c_st[z].astype(MXU_DTYPE)
    for h in range(nh):
        r = z * nh + h
        bc = b_col[:, h:h + 1]
        br = b_row[h:h + 1, :]
        ic = i_col[:, h:h + 1]
        ir = i_row[h:h + 1, :]
        bl = b_tot[:, h:h + 1]
        m_old = m_st[r:r + 1, 0:1]
        vh = v[:, (h // 2) * LANES:(h // 2 + 1) * LANES]
        if h % 2 == 1:
            vh = pltpu.roll(vh, HEAD_DIM, 1)
        vext = jnp.where(lane128 < HEAD_DIM, vh, jnp.where(lane128 == HEAD_DIM, 1.0, 0.0)).astype(MXU_DTYPE)
        if with_output:
            qh = jnp.where(lane256 // HEAD_DIM == h, q, 0.0).astype(MXU_DTYPE)
            log_d = jnp.where(mask, bc - br + ir, NEG_BIG)
            inter = bc + m_old
            m_t = jnp.maximum(inter, jnp.max(log_d, axis=1, keepdims=True))
            s = _mm_nt(qh, k_mx) * jnp.exp(log_d - m_t)
            w_inter = jnp.exp(inter - m_t)
            nd = _mm(s, vext) + w_inter * _mm(qh, c_all)
            den = nd[:, HEAD_DIM:HEAD_DIM + 1]
            hval = nd / jnp.maximum(jnp.abs(den), jnp.exp(-m_t))
            h_ref[pl.ds(row0, lc), h * LANES:(h + 1) * LANES] = hval
        log_w = bl - bc + ic
        m_new = jnp.maximum(bl + m_old, jnp.max(log_w, axis=0, keepdims=True))
        w = jnp.exp(log_w - m_new)
        decay = jnp.exp(bl + m_old - m_new)
        upd = _mm_tn(k * w, vext)
        rows = slice(h * HEAD_DIM, (h + 1) * HEAD_DIM)
        c_st[z, rows, :] = decay * c_st[z, rows, :] + upd[rows, :]
        m_st[r:r + 1, :] = jnp.broadcast_to(m_new, (1, LANES))


def _mlstm_kernel(lc, nc, need_ctx,
                  pa_c, g_c, gt_c,
                  pa_f, g_f, gt_f, cos_f, sin_f,
                  pa_b, g_b, gt_b, cos_b, sin_b,
                  o_lat, o_ctx, brow_ref, bcol_ref, nw_ref, perm_ref,
                  *rest):
    if need_ctx:
        out_lat, out_ctx, hf, hb, c_st, m_st = rest
    else:
        out_lat, hf, hb, c_st, m_st = rest
        out_ctx = None
    s = pl.program_id(1)
    w = MLSTM_W
    k_scale = HEAD_DIM ** -0.5

    def load(pa, cos_ref, sin_ref):
        q = pa[:, 0:w].astype(F32)
        k = pa[:, w:2 * w].astype(F32)
        v = pa[:, 2 * w:3 * w].astype(F32)
        if cos_ref is not None:
            cs = cos_ref[...]
            sn = sin_ref[...]
            q = q * cs + _mm(q, perm_ref[...]) * sn
            k = k * cs + _mm(k, perm_ref[...]) * sn
        return q, k * k_scale, v

    @pl.when(s == 0)
    def _():
        c_st[...] = jnp.zeros(c_st.shape, F32)
        m_st[...] = jnp.zeros(m_st.shape, F32)
        q, k, v = load(pa_c, None, None)
        gcol = g_c[...] + brow_ref[...]
        grow = gt_c[...] + bcol_ref[...]
        for z, h_ref in ((0, hf), (1, hb)):
            _mlstm_direction(z, q, k, v, gcol, grow, c_st, m_st, h_ref, 0, lc, need_ctx)

    @pl.when(s > 0)
    def _():
        for z, h_ref, refs in ((0, hf, (pa_f, g_f, gt_f, cos_f, sin_f)),
                               (1, hb, (pa_b, g_b, gt_b, cos_b, sin_b))):
            pa, g, gt, cos_ref, sin_ref = refs
            j = s - 1 if z == 0 else nc - s
            row0 = pl.multiple_of(lc + j * lc, lc)
            q, k, v = load(pa, cos_ref, sin_ref)
            gcol = g[...] + brow_ref[...]
            grow = gt[...] + bcol_ref[...]
            _mlstm_direction(z, q, k, v, gcol, grow, c_st, m_st, h_ref, row0, lc, True)

    @pl.when(s == nc)
    def _():
        lane128 = lax.broadcasted_iota(jnp.int32, (1, LANES), 1)
        first = 0 if need_ctx else 1
        for ch in range(first, nc + 1):
            rows = slice(ch * lc, (ch + 1) * lc)
            if ch == 0:
                o_val, dst, dst_rows = o_ctx[...], out_ctx, slice(0, lc)
            else:
                dst_rows = slice((ch - 1) * lc, ch * lc)
                o_val, dst = o_lat[dst_rows, :], out_lat
            o_val = o_val.astype(F32)
            for p in range(MLSTM_HEADS // 2):
                pair = []
                for h in (2 * p, 2 * p + 1):
                    hv = hf[rows, h * LANES:(h + 1) * LANES] + hb[rows, h * LANES:(h + 1) * LANES]
                    hv = jnp.where(lane128 < HEAD_DIM, hv, 0.0)
                    ms = jnp.sum(hv * hv, axis=1, keepdims=True) * (1.0 / HEAD_DIM)
                    pair.append(hv * lax.rsqrt(ms + EPS))
                packed = jnp.where(lane128 < HEAD_DIM, pair[0], pltpu.roll(pair[1], HEAD_DIM, 1))
                cols = slice(p * LANES, (p + 1) * LANES)
                res = packed * nw_ref[:, cols] * jax.nn.sigmoid(o_val[:, cols])
                dst[dst_rows, cols] = res.astype(dst.dtype)


def _rope_tables(seq):
    half = HEAD_DIM // 2
    quarter = half // 2
    t = jnp.arange(seq, dtype=jnp.int32)
    inv_freq = ROPE_BASE ** (-jnp.arange(quarter, dtype=F32) / quarter)
    parts_c, parts_s = [], []
    for pos in (t // GRID_W, t % GRID_W):
        ang = pos.astype(F32)[:, None] * inv_freq[None, :]
        parts_c += [jnp.cos(ang), jnp.cos(ang)]
        parts_s += [-jnp.sin(ang), jnp.sin(ang)]
    cos = jnp.tile(jnp.concatenate(parts_c, axis=-1), (1, MLSTM_HEADS))
    sin = jnp.tile(jnp.concatenate(parts_s, axis=-1), (1, MLSTM_HEADS))
    j = np.arange(MLSTM_W)
    partner = np.where(j % half < quarter, j + quarter, j - quarter)
    perm = np.zeros((MLSTM_W, MLSTM_W), np.float32)
    perm[partner, j] = 1.0
    return cos, sin, jnp.asarray(perm, dtype=MXU_DTYPE)


def _mlstm(pa, g, gt, ig_b, fg_b, norm_w, rope, batch, seq, ctx_len, need_ctx):
    lc = ctx_len
    nc = seq // lc
    n_lat = batch * seq
    cos, sin, perm = rope
    bias = jnp.concatenate([ig_b.reshape(-1), fg_b.reshape(-1)]).astype(F32)
    ng = bias.shape[0]
    lat_blocks = n_lat // lc

    def fwd(b, s):
        return b * nc + jnp.maximum(s - 1, 0)

    def bwd(b, s):
        return b * nc + nc - jnp.maximum(s, 1)

    def fwd_c(b, s):
        return jnp.maximum(s - 1, 0)

    def bwd_c(b, s):
        return nc - jnp.maximum(s, 1)

    def lat_specs(chunk, chunk_c):
        return [pl.BlockSpec((lc, A_G), lambda b, s: (chunk(b, s), 0)),
                pl.BlockSpec((lc, ng), lambda b, s: (chunk(b, s), 0)),
                pl.BlockSpec((ng, lc), lambda b, s: (0, chunk(b, s))),
                pl.BlockSpec((lc, MLSTM_W), lambda b, s: (chunk_c(b, s), 0)),
                pl.BlockSpec((lc, MLSTM_W), lambda b, s: (chunk_c(b, s), 0))]

    in_specs = ([pl.BlockSpec((lc, A_G), lambda b, s: (lat_blocks + b, 0)),
                 pl.BlockSpec((lc, ng), lambda b, s: (lat_blocks + b, 0)),
                 pl.BlockSpec((ng, lc), lambda b, s: (0, lat_blocks + b))]
                + lat_specs(fwd, fwd_c) + lat_specs(bwd, bwd_c)
                + [pl.BlockSpec((seq, MLSTM_W), lambda b, s: (b, 3)),
                   pl.BlockSpec((lc, MLSTM_W), lambda b, s: (lat_blocks + b, 3)),
                   pl.BlockSpec((1, ng), lambda b, s: (0, 0)),
                   pl.BlockSpec((ng, 1), lambda b, s: (0, 0)),
                   pl.BlockSpec((1, MLSTM_W), lambda b, s: (0, 0)),
                   pl.BlockSpec((MLSTM_W, MLSTM_W), lambda b, s: (0, 0))])
    out_specs = [pl.BlockSpec((seq, MLSTM_W), lambda b, s: (b, 0))]
    out_shape = [jax.ShapeDtypeStruct((n_lat, MLSTM_W), BF16)]
    if need_ctx:
        out_specs.append(pl.BlockSpec((lc, MLSTM_W), lambda b, s: (b, 0)))
        out_shape.append(jax.ShapeDtypeStruct((batch * ctx_len, MLSTM_W), BF16))
    t_all = ctx_len + seq
    outs = pl.pallas_call(
        functools.partial(_mlstm_kernel, lc, nc, need_ctx),
        grid=(batch, nc + 1),
        in_specs=in_specs,
        out_specs=out_specs,
        out_shape=out_shape,
        scratch_shapes=[pltpu.VMEM((t_all, MLSTM_HEADS * LANES), F32),
                        pltpu.VMEM((t_all, MLSTM_HEADS * LANES), F32),
                        pltpu.VMEM((2, MLSTM_W, LANES), F32),
                        pltpu.VMEM((2 * MLSTM_HEADS, LANES), F32)],
        compiler_params=_cparams(("parallel", "arbitrary")),
        name="mlstm",
    )(pa, g, gt, pa, g, gt, cos, sin, pa, g, gt, cos, sin, pa, pa,
      bias.reshape(1, ng), bias.reshape(ng, 1), norm_w.reshape(1, MLSTM_W).astype(F32), perm)
    return (outs[0], outs[1]) if need_ctx else (outs[0], None)


def _na_patterns(n_rows):
    kr = min(NA_WIN_R, n_rows)
    groups = n_rows // NA_QROWS
    col = np.arange(GRID_W)
    col_start = np.clip(col - NA_WIN_C // 2, 0, GRID_W - NA_WIN_C)
    in_win = (col[None, :] >= col_start[:, None]) & (col[None, :] < col_start[:, None] + NA_WIN_C)
    dc = np.clip(col[None, :] - col[:, None] + NA_WIN_C - 1, 0, 2 * NA_WIN_C - 2)
    pats, pat_ids, bases = [], [], []
    for gi in range(groups):
        base = int(np.clip(NA_QROWS * gi - NA_WIN_R // 2, 0, n_rows - NA_KROWS))
        nq, nk = NA_QROWS * GRID_W, NA_KROWS * GRID_W
        dr_idx = np.zeros((nq, nk), np.int32)
        dc_idx = np.zeros((nq, nk), np.int32)
        valid = np.zeros((nq, nk), bool)
        for qr in range(NA_QROWS):
            r = NA_QROWS * gi + qr
            r0 = int(np.clip(r - kr // 2, 0, n_rows - kr))
            for kj in range(NA_KROWS):
                krow = base + kj
                qs = slice(qr * GRID_W, (qr + 1) * GRID_W)
                ks = slice(kj * GRID_W, (kj + 1) * GRID_W)
                dc_idx[qs, ks] = dc
                if r0 <= krow < r0 + kr:
                    dr_idx[qs, ks] = krow - r + NA_WIN_R - 1
                    valid[qs, ks] = in_win
        key = (dr_idx.tobytes(), valid.tobytes())
        for pi, (pk, _, _, _) in enumerate(pats):
            if pk == key:
                pat_ids.append(pi)
                break
        else:
            pat_ids.append(len(pats))
            pats.append((key, dr_idx, dc_idx, valid))
        bases.append(base)
    dr_all = np.stack([p[1] for p in pats])
    dc_all = np.stack([p[2] for p in pats])
    valid_all = np.stack([p[3] for p in pats])
    return tuple(pat_ids), tuple(bases), dr_all, dc_all, valid_all


def _na_kernel(pat_ids, bases, need_ctx, q_ref, k_ref, v_ref, kc_ref, vc_ref, *rest):
    if need_ctx:
        qc_ref, bias_ref, qw_ref, kw_ref, out_ref, outc_ref, kn_s, kcn_s = rest
    else:
        bias_ref, qw_ref, kw_ref, out_ref, kn_s, kcn_s = rest
    lane = lax.broadcasted_iota(jnp.int32, (1, LANES), 1)
    low = lane < HEAD_DIM
    inv_d = 1.0 / HEAD_DIM

    def rmsn(x, w):
        x2 = x * x
        s0 = jnp.sum(jnp.where(low, x2, 0.0), axis=-1, keepdims=True)
        s1 = jnp.sum(jnp.where(low, 0.0, x2), axis=-1, keepdims=True)
        r = jnp.where(low, lax.rsqrt(s0 * inv_d + EPS), lax.rsqrt(s1 * inv_d + EPS))
        return x * r * w

    qw = qw_ref[...]
    kn_s[...] = rmsn(k_ref[...].astype(F32), kw_ref[...]).astype(kn_s.dtype)
    kcn_s[...] = rmsn(kc_ref[...].astype(F32), kw_ref[...]).astype(kcn_s.dtype)
    kcn = kcn_s[...]
    vc = vc_ref[...]
    scale = HEAD_DIM ** -0.5
    nq = NA_QROWS * GRID_W
    nk = NA_KROWS * GRID_W

    def attend(qn, parts):
        outs = []
        for hh in range(2):
            qh = jnp.where(low if hh == 0 else jnp.logical_not(low), qn, 0.0).astype(MXU_DTYPE)
            scores = []
            for keys, _, bias in parts:
                sc = _mm_nt(qh, keys)
                if bias is not None:
                    sc = sc + bias[hh]
                scores.append(sc)
            m = scores[0].max(axis=-1, keepdims=True)
            for sc in scores[1:]:
                m = jnp.maximum(m, sc.max(axis=-1, keepdims=True))
            acc = None
            den = None
            for sc, (_, vals, _) in zip(scores, parts):
                p = jnp.exp(sc - m)
                d = jnp.sum(p, axis=-1, keepdims=True)
                o = _mm(p, vals)
                acc = o if acc is None else acc + o
                den = d if den is None else den + d
            outs.append(acc / den)
        return jnp.where(low, outs[0], outs[1])

    for gi, (pid, base) in enumerate(zip(pat_ids, bases)):
        qn = rmsn(q_ref[gi * nq:(gi + 1) * nq, :].astype(F32), qw) * scale
        kwin = kn_s[base * GRID_W:base * GRID_W + nk, :]
        vwin = v_ref[base * GRID_W:base * GRID_W + nk, :]
        bias = (bias_ref[0, pid], bias_ref[1, pid])
        res = attend(qn, [(kwin, vwin, bias), (kcn, vc, None)])
        out_ref[gi * nq:(gi + 1) * nq, :] = res.astype(out_ref.dtype)

    if need_ctx:
        qn = rmsn(qc_ref[...].astype(F32), qw) * scale
        outc_ref[...] = attend(qn, [(kcn, vc, None)]).astype(outc_ref.dtype)


def _na(pb, qn_w, kn_w, rpb, batch, seq, ctx_len, need_ctx):
    n_rows = seq // GRID_W
    pat_ids, bases, dr_all, dc_all, valid_all = _na_patterns(n_rows)
    npat = dr_all.shape[0]
    nq, nk = NA_QROWS * GRID_W, NA_KROWS * GRID_W
    bias = jnp.where(valid_all[None], rpb.astype(F32)[:, dr_all, dc_all], NEG_BIG)
    n_lat = batch * seq
    pairs = NA_HEADS // 2
    qoff, koff, voff = 0, pairs, 2 * pairs

    in_specs = [pl.BlockSpec((seq, LANES), lambda p, b: (b, qoff + p)),
                pl.BlockSpec((seq, LANES), lambda p, b: (b, koff + p)),
                pl.BlockSpec((seq, LANES), lambda p, b: (b, voff + p)),
                pl.BlockSpec((ctx_len, LANES), lambda p, b: (n_lat // ctx_len + b, koff + p)),
                pl.BlockSpec((ctx_len, LANES), lambda p, b: (n_lat // ctx_len + b, voff + p))]
    args = [pb, pb, pb, pb, pb]
    if need_ctx:
        in_specs.append(pl.BlockSpec((ctx_len, LANES), lambda p, b: (n_lat // ctx_len + b, qoff + p)))
        args.append(pb)
    in_specs += [pl.BlockSpec((2, npat, nq, nk), lambda p, b: (p, 0, 0, 0)),
                 pl.BlockSpec((1, LANES), lambda p, b: (0, 0)),
                 pl.BlockSpec((1, LANES), lambda p, b: (0, 0))]
    args += [bias, jnp.tile(qn_w.astype(F32), 2).reshape(1, LANES), jnp.tile(kn_w.astype(F32), 2).reshape(1, LANES)]
    out_specs = [pl.BlockSpec((seq, LANES), lambda p, b: (b, p))]
    out_shape = [jax.ShapeDtypeStruct((n_lat, NA_W), BF16)]
    if need_ctx:
        out_specs.append(pl.BlockSpec((ctx_len, LANES), lambda p, b: (b, p)))
        out_shape.append(jax.ShapeDtypeStruct((batch * ctx_len, NA_W), BF16))
    outs = pl.pallas_call(
        functools.partial(_na_kernel, pat_ids, bases, need_ctx),
        grid=(pairs, batch),
        in_specs=in_specs,
        out_specs=out_specs,
        out_shape=out_shape,
        scratch_shapes=[pltpu.VMEM((seq, LANES), MXU_DTYPE), pltpu.VMEM((ctx_len, LANES), MXU_DTYPE)],
        compiler_params=_cparams(("parallel", "parallel")),
        name="na_attn",
    )(*args)
    return (outs[0], outs[1]) if need_ctx else (outs[0], None)


def _conv_kernel(t_len, u_ref, w_ref, cb_ref, lw_ref, lb_ref, o_ref, pad_s):
    zeros = jnp.zeros((CONV_PAD, CONV_CH), F32)
    pad_s[0:CONV_PAD, :] = zeros
    pad_s[CONV_PAD + t_len:2 * CONV_PAD + t_len, :] = zeros
    pad_s[CONV_PAD:CONV_PAD + t_len, :] = u_ref[...].astype(F32)
    shift = CONV_PAD - CONV_WIDTH // 2

    def body(c, carry):
        r0 = pl.multiple_of(c * CONV_ROWS, CONV_ROWS)
        win = pad_s[pl.ds(r0, CONV_ROWS + 2 * CONV_PAD), :]
        acc = jnp.zeros((CONV_ROWS, CONV_CH), F32) + cb_ref[...]
        for j in range(CONV_WIDTH):
            acc = acc + win[j + shift:j + shift + CONV_ROWS, :] * w_ref[j:j + 1, :]
        mean = jnp.mean(acc, axis=-1, keepdims=True)
        xc = acc - mean
        var = jnp.mean(xc * xc, axis=-1, keepdims=True)
        y = xc * lax.rsqrt(var + EPS) * lw_ref[...] + lb_ref[...]
        o_ref[pl.ds(r0, CONV_ROWS), :] = (y * jax.nn.sigmoid(y)).astype(o_ref.dtype)
        return carry

    lax.fori_loop(0, t_len // CONV_ROWS, body, 0)


def _conv(u, conv_w, conv_b, ln_w, ln_b, first_block, n_seq, t_len):
    row = lambda a: a.reshape(1, CONV_CH).astype(F32)
    return pl.pallas_call(
        functools.partial(_conv_kernel, t_len),
        grid=(n_seq,),
        in_specs=[pl.BlockSpec((t_len, CONV_CH), lambda b: (first_block + b, 0)),
                  pl.BlockSpec((CONV_WIDTH, CONV_CH), lambda b: (0, 0)),
                  pl.BlockSpec((1, CONV_CH), lambda b: (0, 0)),
                  pl.BlockSpec((1, CONV_CH), lambda b: (0, 0)),
                  pl.BlockSpec((1, CONV_CH), lambda b: (0, 0))],
        out_specs=pl.BlockSpec((t_len, CONV_CH), lambda b: (b, 0)),
        out_shape=jax.ShapeDtypeStruct((n_seq * t_len, CONV_CH), BF16),
        scratch_shapes=[pltpu.VMEM((t_len + 2 * CONV_PAD, CONV_CH), F32)],
        compiler_params=_cparams(("parallel",)),
        name="conv_module",
    )(u, conv_w.astype(F32), row(conv_b), row(ln_w), row(ln_b))


def _outproj_kernel(a_ref, b_ref, c_ref, x_ref, g1_ref, sh_ref, sc_ref, nw_ref, wa_ref, wb_ref, wc_ref,
                    rw_ref, rb_ref, xo_ref, hx_ref, idx_ref, gate_ref):
    mix = _mm(a_ref[...], wa_ref[...]) + _mm(b_ref[...], wb_ref[...]) + _mm(c_ref[...], wc_ref[...])
    xn = x_ref[...] + g1_ref[0] * mix
    xo_ref[...] = xn
    ms = jnp.mean(xn * xn, axis=-1, keepdims=True)
    hx = xn * lax.rsqrt(ms + EPS) * nw_ref[...] * (1.0 + sc_ref[0]) + sh_ref[0]
    hx_ref[...] = hx
    logits = jnp.dot(hx, rw_ref[...], preferred_element_type=F32, precision=lax.Precision.HIGHEST) + rb_ref[...]
    lane = lax.broadcasted_iota(jnp.int32, logits.shape, 1)
    idx_out = jnp.zeros(logits.shape, jnp.int32)
    val_out = jnp.zeros(logits.shape, F32)
    top = None
    den = None
    for kk in range(TOP_K):
        m = jnp.max(logits, axis=-1, keepdims=True)
        sel = jnp.min(jnp.where(logits == m, lane, LANES), axis=-1, keepdims=True)
        if kk == 0:
            top = m
        e = jnp.exp(m - top)
        den = e if den is None else den + e
        idx_out = jnp.where(lane == kk, sel, idx_out)
        val_out = jnp.where(lane == kk, e, val_out)
        logits = jnp.where(lane == sel, -jnp.inf, logits)
    idx_ref[...] = idx_out
    gate_ref[...] = val_out / den


def _outproj(a, b, c, xall, g1, sh2, sc2, norm_w, w_out, router_w, router_b, n_rows, n_lat, seq):
    n, d = xall.shape
    tm = TOK_TILE
    n_lat_tiles = n_lat // tm
    per_batch = seq // tm
    n_mod = g1.shape[0]

    def mod_map(i):
        return (jnp.where(i < n_lat_tiles, i // per_batch, n_mod - 1), 0, 0)

    wa = w_out[0:MLSTM_W].astype(MXU_DTYPE)
    wb = w_out[MLSTM_W:MLSTM_W + NA_W].astype(MXU_DTYPE)
    wc = w_out[MLSTM_W + NA_W:].astype(MXU_DTYPE)
    rw = jnp.zeros((d, LANES), F32).at[:, :N_EXPERTS].set(router_w.astype(F32))
    rb = jnp.full((1, LANES), NEG_BIG, F32).at[0, :N_EXPERTS].set(router_b.astype(F32))
    full = lambda r, cc: pl.BlockSpec((r, cc), lambda i: (0, 0))
    tile = lambda cc: pl.BlockSpec((tm, cc), lambda i: (i, 0))
    return pl.pallas_call(
        _outproj_kernel,
        grid=(n_rows // tm,),
        in_specs=[tile(MLSTM_W), tile(NA_W), tile(CONV_CH), tile(d),
                  pl.BlockSpec((1, 1, d), mod_map), pl.BlockSpec((1, 1, d), mod_map),
                  pl.BlockSpec((1, 1, d), mod_map), full(1, d),
                  full(MLSTM_W, d), full(NA_W, d), full(CONV_CH, d), full(d, LANES), full(1, LANES)],
        out_specs=[tile(d), tile(d), tile(LANES), tile(LANES)],
        out_shape=[jax.ShapeDtypeStruct((n_rows, d), F32), jax.ShapeDtypeStruct((n_rows, d), F32),
                   jax.ShapeDtypeStruct((n_rows, LANES), jnp.int32), jax.ShapeDtypeStruct((n_rows, LANES), F32)],
        compiler_params=_cparams(("parallel",)),
        name="out_proj",
    )(a, b, c, xall, g1, sh2, sc2, norm_w.reshape(1, d).astype(F32), wa, wb, wc, rw, rb)


def _dispatch_kernel(tm, dest_ref, hx_ref, xs_in, xs_out, sem):
    del xs_in

    def body(r, carry):
        for kk in range(TOP_K):
            dst = dest_ref[0, 0, kk * tm + r]
            pltpu.make_async_copy(hx_ref.at[pl.ds(r, 1), :], xs_out.at[pl.ds(dst, 1), :], sem).start()
        return carry

    lax.fori_loop(0, tm, body, 0)
    for kk in range(TOP_K):
        pltpu.make_async_copy(hx_ref, xs_out.at[pl.ds(0, tm), :], sem).wait()


def _dispatch(hx, dest_tiles, xs_init, tm):
    n, d = hx.shape
    return pl.pallas_call(
        functools.partial(_dispatch_kernel, tm),
        grid=(n // tm,),
        in_specs=[pl.BlockSpec((1, 1, TOP_K * tm), lambda i: (i, 0, 0), memory_space=pltpu.SMEM),
                  pl.BlockSpec((tm, d), lambda i: (i, 0)),
                  pl.BlockSpec(memory_space=pl.ANY)],
        out_specs=pl.BlockSpec(memory_space=pl.ANY),
        out_shape=jax.ShapeDtypeStruct(xs_init.shape, xs_init.dtype),
        scratch_shapes=[pltpu.SemaphoreType.DMA],
        input_output_aliases={2: 0},
        compiler_params=_cparams(("arbitrary",)),
        name="moe_dispatch",
    )(dest_tiles, hx, xs_init)


def _expert_kernel(be_ref, nu_ref, x_ref, w1_ref, b1_ref, w2_ref, b2_ref, o_ref):
    i = pl.program_id(0)
    de = w2_ref.shape[1]

    @pl.when(i < nu_ref[0])
    def _():
        h = _mm(x_ref[...], w1_ref[0]) + b1_ref[0]
        glu = jnp.minimum(h[:, :de], SWIGLU_LIMIT)
        lin = jnp.clip(h[:, de:], -SWIGLU_LIMIT, SWIGLU_LIMIT)
        act = (lin + 1.0) * glu * jax.nn.sigmoid(SWIGLU_ALPHA * glu)
        o_ref[...] = _mm(act, w2_ref[0]) + b2_ref[0]

    @pl.when(i >= nu_ref[0])
    def _():
        o_ref[...] = jnp.zeros(o_ref.shape, o_ref.dtype)


def _experts(xs, block_e, n_used, w1, b1, w2, b2):
    p, d = xs.shape
    ne, _, two_de = w1.shape
    de = w2.shape[1]
    bm = MOE_BLOCK
    grid_spec = pltpu.PrefetchScalarGridSpec(
        num_scalar_prefetch=2,
        grid=(p // bm,),
        in_specs=[pl.BlockSpec((bm, d), lambda i, be, nu: (i, 0)),
                  pl.BlockSpec((1, d, two_de), lambda i, be, nu: (be[i], 0, 0)),
                  pl.BlockSpec((1, 1, two_de), lambda i, be, nu: (be[i], 0, 0)),
                  pl.BlockSpec((1, de, d), lambda i, be, nu: (be[i], 0, 0)),
                  pl.BlockSpec((1, 1, d), lambda i, be, nu: (be[i], 0, 0))],
        out_specs=pl.BlockSpec((bm, d), lambda i, be, nu: (i, 0)),
    )
    return pl.pallas_call(
        _expert_kernel,
        grid_spec=grid_spec,
        out_shape=jax.ShapeDtypeStruct((p, d), F32),
        compiler_params=_cparams(("arbitrary",)),
        name="moe_experts",
    )(block_e, n_used, xs, w1, b1.reshape(ne, 1, two_de).astype(F32), w2, b2.reshape(ne, 1, d).astype(F32))


def _combine_kernel(tm, dest_ref, eo_hbm, x_ref, gate_ref, g2_ref, xo_ref, buf, sem):
    def body(r, carry):
        for kk in range(TOP_K):
            src = dest_ref[0, 0, kk * tm + r]
            pltpu.make_async_copy(eo_hbm.at[pl.ds(src, 1), :], buf.at[kk, pl.ds(r, 1), :], sem).start()
        return carry

    lax.fori_loop(0, tm, body, 0)
    for kk in range(TOP_K):
        pltpu.make_async_copy(eo_hbm.at[pl.ds(0, tm), :], buf.at[kk], sem).wait()
    gates = gate_ref[...]
    y = gates[:, 0:1] * buf[0]
    for kk in range(1, TOP_K):
        y = y + gates[:, kk:kk + 1] * buf[kk]
    xo_ref[...] = x_ref[...] + g2_ref[0] * y


def _combine(eo, dest_tiles, xres, gates, g2, n_rows, n_lat, seq, tm):
    d = xres.shape[1]
    n_lat_tiles = n_lat // tm
    per_batch = seq // tm
    n_mod = g2.shape[0]

    def mod_map(i):
        return (jnp.where(i < n_lat_tiles, i // per_batch, n_mod - 1), 0, 0)

    return pl.pallas_call(
        functools.partial(_combine_kernel, tm),
        grid=(n_rows // tm,),
        in_specs=[pl.BlockSpec((1, 1, TOP_K * tm), lambda i: (i, 0, 0), memory_space=pltpu.SMEM),
                  pl.BlockSpec(memory_space=pl.ANY),
                  pl.BlockSpec((tm, d), lambda i: (i, 0)),
                  pl.BlockSpec((tm, LANES), lambda i: (i, 0)),
                  pl.BlockSpec((1, 1, d), mod_map)],
        out_specs=pl.BlockSpec((tm, d), lambda i: (i, 0)),
        out_shape=jax.ShapeDtypeStruct((n_rows, d), F32),
        scratch_shapes=[pltpu.VMEM((TOP_K, tm, d), F32), pltpu.SemaphoreType.DMA],
        compiler_params=_cparams(("arbitrary",)),
        name="moe_combine",
    )(dest_tiles, eo, xres, gates, g2)


def _dest_tiles(dest, tm):
    n = dest.shape[0]
    return dest.reshape(n // tm, tm, TOP_K).transpose(0, 2, 1).reshape(n // tm, 1, TOP_K * tm)


def _route(idx, n_blocks):
    bm = MOE_BLOCK
    eid = lax.broadcasted_iota(jnp.int32, (1, 1, N_EXPERTS), 2)
    onehot = (idx[:, :, None] == eid)
    per_tok = jnp.sum(onehot, axis=1, dtype=jnp.int32)
    before = jnp.cumsum(per_tok, axis=0) - per_tok
    counts = jnp.sum(per_tok, axis=0)
    padded = (counts + bm - 1) // bm * bm
    pad_end = jnp.cumsum(padded)
    pad_start = pad_end - padded
    base = before + pad_start[None, :]
    dest = jnp.sum(jnp.where(onehot, base[:, None, :], 0), axis=2)
    block_e = jnp.minimum(jnp.searchsorted(pad_end, jnp.arange(n_blocks, dtype=jnp.int32) * bm, side='right'),
                          N_EXPERTS - 1).astype(jnp.int32)
    n_used = (pad_end[-1] // bm).astype(jnp.int32).reshape(1)
    return dest.astype(jnp.int32), block_e, n_used


def _moe(hx, idx, gates, xres, g2, w1, b1, w2, b2, xs_buf, n_out_rows, n_lat, seq):
    n = hx.shape[0]
    n_blocks = -(-(n * TOP_K + N_EXPERTS * (MOE_BLOCK - 1)) // MOE_BLOCK)
    dest, block_e, n_used = _route(idx[:, :TOP_K], n_blocks)
    p = n_blocks * MOE_BLOCK
    xs = _dispatch(hx, _dest_tiles(dest, TOK_TILE), xs_buf[:p] if xs_buf.shape[0] != p else xs_buf, TOK_TILE)
    eo = _experts(xs, block_e, n_used, w1, b1, w2, b2)
    out = _combine(eo, _dest_tiles(dest[:n_out_rows], COMBINE_TILE), xres, gates, g2, n_out_rows, n_lat, seq,
                   COMBINE_TILE)
    return out, xs


def kernel(x, c, ctx, c_ctx, norm_mix_w, norm_ffn_w, w_ada, b_ada, w_in, mlstm_ig_b, mlstm_fg_b, mlstm_norm_w,
           na_qnorm_w, na_knorm_w, na_rpb, conv_w, conv_b, conv_ln_w, conv_ln_b, w_out, router_w, router_b,
           exp_w1, exp_b1, exp_w2, exp_b2):
    batch, seq, d = x.shape
    ctx_len = ctx.shape[1]
    depth = w_ada.shape[0]
    n_lat = batch * seq
    n_ctx = batch * ctx_len
    n_all = n_lat + n_ctx
    assert seq % TOK_TILE == 0 and n_ctx % TOK_TILE == 0 and seq % ctx_len == 0
    assert (seq // GRID_W) % NA_QROWS == 0 and seq // GRID_W >= NA_KROWS

    mod_rows = -(-(batch + 1) // 8) * 8
    cc = jnp.zeros((mod_rows, d), F32).at[:batch].set(c).at[batch].set(c_ctx)
    mods = _ada(cc, w_ada, b_ada)[:, :batch + 1].reshape(depth, batch + 1, 1, 6, d)
    xall = jnp.concatenate([x.reshape(n_lat, d), ctx.reshape(n_ctx, d)], axis=0)
    rope = _rope_tables(seq)
    n_blocks0 = -(-(n_all * TOP_K + N_EXPERTS * (MOE_BLOCK - 1)) // MOE_BLOCK)
    xs_buf = jnp.zeros((n_blocks0 * MOE_BLOCK, d), F32)

    for l in range(depth):
        need_ctx = l < depth - 1
        sh1, sc1, g1, sh2, sc2, g2 = [mods[l, :, :, i, :] for i in range(6)]
        pa, g, gt, pb, u = _inproj(xall, sh1, sc1, norm_mix_w[l], w_in[l], n_lat, seq)
        a_lat, a_ctx = _mlstm(pa, g, gt, mlstm_ig_b[l], mlstm_fg_b[l], mlstm_norm_w[l], rope,
                              batch, seq, ctx_len, need_ctx)
        b_lat, b_ctx = _na(pb, na_qnorm_w[l], na_knorm_w[l], na_rpb[l], batch, seq, ctx_len, need_ctx)
        c_lat = _conv(u, conv_w[l], conv_b[l], conv_ln_w[l], conv_ln_b[l], 0, batch, seq)
        if need_ctx:
            c_ctx_out = _conv(u, conv_w[l], conv_b[l], conv_ln_w[l], conv_ln_b[l], n_lat // ctx_len, batch, ctx_len)
            a_all = jnp.concatenate([a_lat, a_ctx], axis=0)
            b_all = jnp.concatenate([b_lat, b_ctx], axis=0)
            c_all = jnp.concatenate([c_lat, c_ctx_out], axis=0)
            n_rows = n_all
        else:
            a_all, b_all, c_all = a_lat, b_lat, c_lat
            n_rows = n_lat
        xmid, hx, idx, gates = _outproj(a_all, b_all, c_all, xall, g1, sh2, sc2, norm_ffn_w[l], w_out[l],
                                        router_w[l], router_b[l], n_rows, n_lat, seq)
        w1 = exp_w1[l].astype(MXU_DTYPE)
        w2 = exp_w2[l].astype(MXU_DTYPE)
        xall, xs_buf = _moe(hx, idx, gates, xmid, g2, w1, exp_b1[l], w2, exp_b2[l], xs_buf, n_rows, n_lat, seq)
    return xall[:n_lat].reshape(batch, seq, d)
```

```python
import functools

import numpy as np
import jax
import jax.numpy as jnp
from jax import lax
from jax.experimental import pallas as pl
from jax.experimental.pallas import tpu as pltpu

F32 = jnp.float32
BF16 = jnp.bfloat16
MXU_DTYPE = BF16

GRID_W = 64
HEAD_DIM = 64
MLSTM_HEADS = 4
NA_HEADS = 8
CONV_CH = 256
MLSTM_W = MLSTM_HEADS * HEAD_DIM
NA_W = NA_HEADS * HEAD_DIM
NA_WIN_R = 8
NA_WIN_C = 16
CONV_WIDTH = 31
ROPE_BASE = 10000.0
N_EXPERTS = 32
TOP_K = 4
SWIGLU_LIMIT = 7.0
SWIGLU_ALPHA = 1.702
EPS = 1e-6

A_Q = 0
A_G = 4 * MLSTM_W
B_Q = A_G + 4 * MLSTM_HEADS
C_A = B_Q + 3 * NA_W
IN_COLS = C_A + 2 * CONV_CH

LANES = 128
NEG_BIG = -1e30
VMEM_LIMIT = 56 * 1024 * 1024

NA_QROWS = 4
NA_KROWS = NA_QROWS + NA_WIN_R - 1
TOK_TILE = 512
MOE_BLOCK = 512
COMBINE_TILE = 256
CONV_ROWS = 64
CONV_PAD = 16


def _mm(a, b):
    return jnp.dot(a.astype(MXU_DTYPE), b.astype(MXU_DTYPE), preferred_element_type=F32)


def _mm_nt(a, b):
    return lax.dot_general(a.astype(MXU_DTYPE), b.astype(MXU_DTYPE), (((1,), (1,)), ((), ())),
                           preferred_element_type=F32)


def _mm_tn(a, b):
    return lax.dot_general(a.astype(MXU_DTYPE), b.astype(MXU_DTYPE), (((0,), (0,)), ((), ())),
                           preferred_element_type=F32)


def _cparams(sem):
    return pltpu.CompilerParams(dimension_semantics=sem, vmem_limit_bytes=VMEM_LIMIT)


def _ada_kernel(c_ref, w_ref, b_ref, o_ref):
    cc = c_ref[...]
    s = cc * jax.nn.sigmoid(cc)
    o_ref[0] = _mm(s, w_ref[0]) + b_ref[0]


def _ada(cc, w_ada, b_ada):
    depth, d, n = w_ada.shape
    rows = cc.shape[0]
    tn = 512
    return pl.pallas_call(
        _ada_kernel,
        grid=(depth, n // tn),
        in_specs=[pl.BlockSpec((rows, d), lambda l, j: (0, 0)),
                  pl.BlockSpec((1, d, tn), lambda l, j: (l, 0, j)),
                  pl.BlockSpec((1, 1, tn), lambda l, j: (l, 0, j))],
        out_specs=pl.BlockSpec((1, rows, tn), lambda l, j: (l, 0, j)),
        out_shape=jax.ShapeDtypeStruct((depth, rows, n), F32),
        compiler_params=_cparams(("parallel", "parallel")),
        name="ada_mod",
    )(cc, w_ada, b_ada.reshape(depth, 1, n))


def _inproj_kernel(x_ref, sh_ref, sc_ref, nw_ref, wa_ref, wg_ref, wgt_ref, wb_ref, wc_ref,
                   pa_ref, g_ref, gt_ref, pb_ref, u_ref):
    x = x_ref[...]
    ms = jnp.mean(x * x, axis=-1, keepdims=True)
    y = x * lax.rsqrt(ms + EPS) * nw_ref[...]
    h = (y * (1.0 + sc_ref[0]) + sh_ref[0]).astype(MXU_DTYPE)
    pa_ref[...] = _mm(h, wa_ref[...]).astype(pa_ref.dtype)
    g_ref[...] = _mm(h, wg_ref[...])
    gt_ref[...] = _mm_nt(wgt_ref[...], h)
    pb_ref[...] = _mm(h, wb_ref[...]).astype(pb_ref.dtype)
    pc = _mm(h, wc_ref[...])
    u_ref[...] = (pc[:, :CONV_CH] * jax.nn.sigmoid(pc[:, CONV_CH:])).astype(u_ref.dtype)


def _inproj(xall, shift, scale, norm_w, w_in, n_lat, seq):
    n, d = xall.shape
    tm = TOK_TILE
    n_lat_tiles = n_lat // tm
    per_batch = seq // tm
    n_mod = shift.shape[0]

    def mod_map(i):
        return (jnp.where(i < n_lat_tiles, i // per_batch, n_mod - 1), 0, 0)

    wa = w_in[:, A_Q:A_G].astype(MXU_DTYPE)
    wg = w_in[:, A_G:B_Q].astype(MXU_DTYPE)
    wb = w_in[:, B_Q:C_A].astype(MXU_DTYPE)
    wc = w_in[:, C_A:IN_COLS].astype(MXU_DTYPE)
    ng = B_Q - A_G
    full = lambda r, c: pl.BlockSpec((r, c), lambda i: (0, 0))
    return pl.pallas_call(
        _inproj_kernel,
        grid=(n // tm,),
        in_specs=[pl.BlockSpec((tm, d), lambda i: (i, 0)),
                  pl.BlockSpec((1, 1, d), mod_map),
                  pl.BlockSpec((1, 1, d), mod_map),
                  full(1, d),
                  full(d, A_G), full(d, ng), full(ng, d), full(d, 3 * NA_W), full(d, 2 * CONV_CH)],
        out_specs=[pl.BlockSpec((tm, A_G), lambda i: (i, 0)),
                   pl.BlockSpec((tm, ng), lambda i: (i, 0)),
                   pl.BlockSpec((ng, tm), lambda i: (0, i)),
                   pl.BlockSpec((tm, 3 * NA_W), lambda i: (i, 0)),
                   pl.BlockSpec((tm, CONV_CH), lambda i: (i, 0))],
        out_shape=[jax.ShapeDtypeStruct((n, A_G), BF16),
                   jax.ShapeDtypeStruct((n, ng), F32),
                   jax.ShapeDtypeStruct((ng, n), F32),
                   jax.ShapeDtypeStruct((n, 3 * NA_W), BF16),
                   jax.ShapeDtypeStruct((n, CONV_CH), BF16)],
        compiler_params=_cparams(("parallel",)),
        name="in_proj",
    )(xall, shift, scale, norm_w.reshape(1, d), wa, wg, wg.T, wb, wc)


def _split3(x):
    hi = x.astype(BF16)
    r1 = x - hi.astype(F32)
    mid = r1.astype(BF16)
    lo = (r1 - mid.astype(F32)).astype(BF16)
    return hi, mid, lo


def _tri_left(tri, x):
    return sum(jnp.dot(tri, p, preferred_element_type=F32) for p in _split3(x))


def _tri_right(x, tri):
    return sum(jnp.dot(p, tri, preferred_element_type=F32) for p in _split3(x))


def _log_sigmoid(x):
    return jnp.minimum(x, 0.0) - jnp.log(1.0 + jnp.exp(-jnp.abs(x)))


def _mlstm_direction(z, q, k, v, gcol, grow, c_st, m_st, h_ref, row0, lc, with_output):
    nh = MLSTM_HEADS
    ti = lax.broadcasted_iota(jnp.int32, (lc, lc), 0)
    si = lax.broadcasted_iota(jnp.int32, (lc, lc), 1)
    lower = si <= ti
    upper = si >= ti
    tl = jnp.where(lower, 1.0, 0.0).astype(BF16)
    tu = jnp.where(upper, 1.0, 0.0).astype(BF16)
    i_col = gcol[:, z * nh:(z + 1) * nh]
    f_col = _log_sigmoid(gcol[:, 2 * nh + z * nh:2 * nh + (z + 1) * nh])
    i_row = grow[z * nh:(z + 1) * nh, :]
    f_row = _log_sigmoid(grow[2 * nh + z * nh:2 * nh + (z + 1) * nh, :])
    if z == 0:
        b_col = _tri_left(tl, f_col)
        b_row = _tri_right(f_row, tu)
        b_tot = b_col[lc - 1:lc, :]
        mask = lower
    else:
        b_col = _tri_left(tu, f_col)
        b_row = _tri_right(f_row, tl)
        b_tot = b_col[0:1, :]
        mask = upper
    lane256 = lax.broadcasted_iota(jnp.int32, (1, 4 * HEAD_DIM), 1)
    lane128 = lax.broadcasted_iota(jnp.int32, (1, LANES), 1)
    k_mx = k.astype(MXU_DTYPE)
    c_all = c_st[z].astype(MXU_DTYPE)
    for h in range(nh):
        r = z * nh + h
        bc = b_col[:, h:h + 1]
        br = b_row[h:h + 1, :]
        ic = i_col[:, h:h + 1]
        ir = i_row[h:h + 1, :]
        bl = b_tot[:, h:h + 1]
        m_old = m_st[r:r + 1, 0:1]
        vh = v[:, (h // 2) * LANES:(h // 2 + 1) * LANES]
        if h % 2 == 1:
            vh = pltpu.roll(vh, HEAD_DIM, 1)
        vext = jnp.where(lane128 < HEAD_DIM, vh, jnp.where(lane128 == HEAD_DIM, 1.0, 0.0)).astype(MXU_DTYPE)
        if with_output:
            qh = jnp.where(lane256 // HEAD_DIM == h, q, 0.0).astype(MXU_DTYPE)
            log_d = jnp.where(mask, bc - br + ir, NEG_BIG)
            inter = bc + m_old
            m_t = jnp.maximum(inter, jnp.max(log_d, axis=1, keepdims=True))
            s = _mm_nt(qh, k_mx) * jnp.exp(log_d - m_t)
            w_inter = jnp.exp(inter - m_t)
            nd = _mm(s, vext) + w_inter * _mm(qh, c_all)
            den = nd[:, HEAD_DIM:HEAD_DIM + 1]
            hval = nd / jnp.maximum(jnp.abs(den), jnp.exp(-m_t))
            h_ref[pl.ds(row0, lc), h * LANES:(h + 1) * LANES] = hval
        log_w = bl - bc + ic
        m_new = jnp.maximum(bl + m_old, jnp.max(log_w, axis=0, keepdims=True))
        w = jnp.exp(log_w - m_new)
        decay = jnp.exp(bl + m_old - m_new)
        upd = _mm_tn(k * w, vext)
        rows = slice(h * HEAD_DIM, (h + 1) * HEAD_DIM)
        c_st[z, rows, :] = decay * c_st[z, rows, :] + upd[rows, :]
        m_st[r:r + 1, :] = jnp.broadcast_to(m_new, (1, LANES))


def _mlstm_kernel(lc, nc, need_ctx,
                  pa_c, g_c, gt_c,
                  pa_f, g_f, gt_f, cos_f, sin_f,
                  pa_b, g_b, gt_b, cos_b, sin_b,
                  o_lat, o_ctx, brow_ref, bcol_ref, nw_ref, perm_ref,
                  *rest):
    if need_ctx:
        out_lat, out_ctx, hf, hb, c_st, m_st = rest
    else:
        out_lat, hf, hb, c_st, m_st = rest
        out_ctx = None
    s = pl.program_id(1)
    w = MLSTM_W
    k_scale = HEAD_DIM ** -0.5

    def load(pa, cos_ref, sin_ref):
        q = pa[:, 0:w].astype(F32)
        k = pa[:, w:2 * w].astype(F32)
        v = pa[:, 2 * w:3 * w].astype(F32)
        if cos_ref is not None:
            cs = cos_ref[...]
            sn = sin_ref[...]
            q = q * cs + _mm(q, perm_ref[...]) * sn
            k = k * cs + _mm(k, perm_ref[...]) * sn
        return q, k * k_scale, v

    @pl.when(s == 0)
    def _():
        c_st[...] = jnp.zeros(c_st.shape, F32)
        m_st[...] = jnp.zeros(m_st.shape, F32)
        q, k, v = load(pa_c, None, None)
        gcol = g_c[...] + brow_ref[...]
        grow = gt_c[...] + bcol_ref[...]
        for z, h_ref in ((0, hf), (1, hb)):
            _mlstm_direction(z, q, k, v, gcol, grow, c_st, m_st, h_ref, 0, lc, need_ctx)

    @pl.when(s > 0)
    def _():
        for z, h_ref, refs in ((0, hf, (pa_f, g_f, gt_f, cos_f, sin_f)),
                               (1, hb, (pa_b, g_b, gt_b, cos_b, sin_b))):
            pa, g, gt, cos_ref, sin_ref = refs
            j = s - 1 if z == 0 else nc - s
            row0 = pl.multiple_of(lc + j * lc, lc)
            q, k, v = load(pa, cos_ref, sin_ref)
            gcol = g[...] + brow_ref[...]
            grow = gt[...] + bcol_ref[...]
            _mlstm_direction(z, q, k, v, gcol, grow, c_st, m_st, h_ref, row0, lc, True)

    @pl.when(s == nc)
    def _():
        lane128 = lax.broadcasted_iota(jnp.int32, (1, LANES), 1)
        first = 0 if need_ctx else 1
        for ch in range(first, nc + 1):
            rows = slice(ch * lc, (ch + 1) * lc)
            if ch == 0:
                o_val, dst, dst_rows = o_ctx[...], out_ctx, slice(0, lc)
            else:
                dst_rows = slice((ch - 1) * lc, ch * lc)
                o_val, dst = o_lat[dst_rows, :], out_lat
            o_val = o_val.astype(F32)
            for p in range(MLSTM_HEADS // 2):
                pair = []
                for h in (2 * p, 2 * p + 1):
                    hv = hf[rows, h * LANES:(h + 1) * LANES] + hb[rows, h * LANES:(h + 1) * LANES]
                    hv = jnp.where(lane128 < HEAD_DIM, hv, 0.0)
                    ms = jnp.sum(hv * hv, axis=1, keepdims=True) * (1.0 / HEAD_DIM)
                    pair.append(hv * lax.rsqrt(ms + EPS))
                packed = jnp.where(lane128 < HEAD_DIM, pair[0], pltpu.roll(pair[1], HEAD_DIM, 1))
                cols = slice(p * LANES, (p + 1) * LANES)
                res = packed * nw_ref[:, cols] * jax.nn.sigmoid(o_val[:, cols])
                dst[dst_rows, cols] = res.astype(dst.dtype)


def _rope_tables(seq):
    half = HEAD_DIM // 2
    quarter = half // 2
    t = jnp.arange(seq, dtype=jnp.int32)
    inv_freq = ROPE_BASE ** (-jnp.arange(quarter, dtype=F32) / quarter)
    parts_c, parts_s = [], []
    for pos in (t // GRID_W, t % GRID_W):
        ang = pos.astype(F32)[:, None] * inv_freq[None, :]
        parts_c += [jnp.cos(ang), jnp.cos(ang)]
        parts_s += [-jnp.sin(ang), jnp.sin(ang)]
    cos = jnp.tile(jnp.concatenate(parts_c, axis=-1), (1, MLSTM_HEADS))
    sin = jnp.tile(jnp.concatenate(parts_s, axis=-1), (1, MLSTM_HEADS))
    j = np.arange(MLSTM_W)
    partner = np.where(j % half < quarter, j + quarter, j - quarter)
    perm = np.zeros((MLSTM_W, MLSTM_W), np.float32)
    perm[partner, j] = 1.0
    return cos, sin, jnp.asarray(perm, dtype=MXU_DTYPE)


def _mlstm(pa, g, gt, ig_b, fg_b, norm_w, rope, batch, seq, ctx_len, need_ctx):
    lc = ctx_len
    nc = seq // lc
    n_lat = batch * seq
    cos, sin, perm = rope
    bias = jnp.concatenate([ig_b.reshape(-1), fg_b.reshape(-1)]).astype(F32)
    ng = bias.shape[0]
    lat_blocks = n_lat // lc

    def fwd(b, s):
        return b * nc + jnp.maximum(s - 1, 0)

    def bwd(b, s):
        return b * nc + nc - jnp.maximum(s, 1)

    def fwd_c(b, s):
        return jnp.maximum(s - 1, 0)

    def bwd_c(b, s):
        return nc - jnp.maximum(s, 1)

    def lat_specs(chunk, chunk_c):
        return [pl.BlockSpec((lc, A_G), lambda b, s: (chunk(b, s), 0)),
                pl.BlockSpec((lc, ng), lambda b, s: (chunk(b, s), 0)),
                pl.BlockSpec((ng, lc), lambda b, s: (0, chunk(b, s))),
                pl.BlockSpec((lc, MLSTM_W), lambda b, s: (chunk_c(b, s), 0)),
                pl.BlockSpec((lc, MLSTM_W), lambda b, s: (chunk_c(b, s), 0))]

    in_specs = ([pl.BlockSpec((lc, A_G), lambda b, s: (lat_blocks + b, 0)),
                 pl.BlockSpec((lc, ng), lambda b, s: (lat_blocks + b, 0)),
                 pl.BlockSpec((ng, lc), lambda b, s: (0, lat_blocks + b))]
                + lat_specs(fwd, fwd_c) + lat_specs(bwd, bwd_c)
                + [pl.BlockSpec((seq, MLSTM_W), lambda b, s: (b, 3)),
                   pl.BlockSpec((lc, MLSTM_W), lambda b, s: (lat_blocks + b, 3)),
                   pl.BlockSpec((1, ng), lambda b, s: (0, 0)),
                   pl.BlockSpec((ng, 1), lambda b, s: (0, 0)),
                   pl.BlockSpec((1, MLSTM_W), lambda b, s: (0, 0)),
                   pl.BlockSpec((MLSTM_W, MLSTM_W), lambda b, s: (0, 0))])
    out_specs = [pl.BlockSpec((seq, MLSTM_W), lambda b, s: (b, 0))]
    out_shape = [jax.ShapeDtypeStruct((n_lat, MLSTM_W), BF16)]
    if need_ctx:
        out_specs.append(pl.BlockSpec((lc, MLSTM_W), lambda b, s: (b, 0)))
        out_shape.append(jax.ShapeDtypeStruct((batch * ctx_len, MLSTM_W), BF16))
    t_all = ctx_len + seq
    outs = pl.pallas_call(
        functools.partial(_mlstm_kernel, lc, nc, need_ctx),
        grid=(batch, nc + 1),
        in_specs=in_specs,
        out_specs=out_specs,
        out_shape=out_shape,
        scratch_shapes=[pltpu.VMEM((t_all, MLSTM_HEADS * LANES), F32),
                        pltpu.VMEM((t_all, MLSTM_HEADS * LANES), F32),
                        pltpu.VMEM((2, MLSTM_W, LANES), F32),
                        pltpu.VMEM((2 * MLSTM_HEADS, LANES), F32)],
        compiler_params=_cparams(("parallel", "arbitrary")),
        name="mlstm",
    )(pa, g, gt, pa, g, gt, cos, sin, pa, g, gt, cos, sin, pa, pa,
      bias.reshape(1, ng), bias.reshape(ng, 1), norm_w.reshape(1, MLSTM_W).astype(F32), perm)
    return (outs[0], outs[1]) if need_ctx else (outs[0], None)


def _na_patterns(n_rows):
    kr = min(NA_WIN_R, n_rows)
    n_dr = 2 * NA_WIN_R - 1
    pats, pat_ids, bases = [], [], []
    for gi in range(n_rows // NA_QROWS):
        base = int(np.clip(NA_QROWS * gi - NA_WIN_R // 2, 0, n_rows - NA_KROWS))
        dr = np.full((NA_QROWS, NA_KROWS), n_dr, np.int32)
        for qr in range(NA_QROWS):
            r = NA_QROWS * gi + qr
            r0 = int(np.clip(r - kr // 2, 0, n_rows - kr))
            for kj in range(NA_KROWS):
                if r0 <= base + kj < r0 + kr:
                    dr[qr, kj] = base + kj - r + NA_WIN_R - 1
        for pi, p in enumerate(pats):
            if np.array_equal(p, dr):
                pat_ids.append(pi)
                break
        else:
            pat_ids.append(len(pats))
            pats.append(dr)
        bases.append(base)
    return tuple(pat_ids), tuple(bases), np.stack(pats)


def _na_bias_table(rpb, row_idx):
    heads = rpb.shape[0]
    col = np.arange(GRID_W)
    col_start = np.clip(col - NA_WIN_C // 2, 0, GRID_W - NA_WIN_C)
    in_win = (col[None, :] >= col_start[:, None]) & (col[None, :] < col_start[:, None] + NA_WIN_C)
    dc = np.clip(col[None, :] - col[:, None] + NA_WIN_C - 1, 0, 2 * NA_WIN_C - 2)
    onehot = (dc[None] == np.arange(2 * NA_WIN_C - 1)[:, None, None]).astype(np.float32)
    planes = jnp.einsum('hdc,cqk->hdqk', rpb, onehot, precision=lax.Precision.HIGHEST)
    planes = jnp.where(in_win[None, None], planes, NEG_BIG)
    planes = jnp.concatenate([planes, jnp.full((heads, 1, GRID_W, GRID_W), NEG_BIG, F32)], axis=1)
    npat = row_idx.shape[0]
    tab = planes[:, row_idx.reshape(-1)].reshape(heads, npat, NA_QROWS, NA_KROWS, GRID_W, GRID_W)
    return tab.transpose(0, 1, 2, 4, 3, 5).reshape(heads, npat, NA_QROWS * GRID_W, NA_KROWS * GRID_W)


def _na_kernel(pat_ids, bases, need_ctx, q_ref, k_ref, v_ref, kc_ref, vc_ref, *rest):
    if need_ctx:
        qc_ref, bias_ref, qw_ref, kw_ref, out_ref, outc_ref, kn_s, kcn_s = rest
    else:
        bias_ref, qw_ref, kw_ref, out_ref, kn_s, kcn_s = rest
    lane = lax.broadcasted_iota(jnp.int32, (1, LANES), 1)
    low = lane < HEAD_DIM
    inv_d = 1.0 / HEAD_DIM

    def rmsn(x, w):
        x2 = x * x
        s0 = jnp.sum(jnp.where(low, x2, 0.0), axis=-1, keepdims=True)
        s1 = jnp.sum(jnp.where(low, 0.0, x2), axis=-1, keepdims=True)
        r = jnp.where(low, lax.rsqrt(s0 * inv_d + EPS), lax.rsqrt(s1 * inv_d + EPS))
        return x * r * w

    qw = qw_ref[...]
    kn_s[...] = rmsn(k_ref[...].astype(F32), kw_ref[...]).astype(kn_s.dtype)
    kcn_s[...] = rmsn(kc_ref[...].astype(F32), kw_ref[...]).astype(kcn_s.dtype)
    kcn = kcn_s[...]
    vc = vc_ref[...]
    scale = HEAD_DIM ** -0.5
    nq = NA_QROWS * GRID_W
    nk = NA_KROWS * GRID_W

    def attend(qn, parts):
        outs = []
        for hh in range(2):
            qh = jnp.where(low if hh == 0 else jnp.logical_not(low), qn, 0.0).astype(MXU_DTYPE)
            scores = []
            for keys, _, bias in parts:
                sc = _mm_nt(qh, keys)
                if bias is not None:
                    sc = sc + bias[hh]
                scores.append(sc)
            m = scores[0].max(axis=-1, keepdims=True)
            for sc in scores[1:]:
                m = jnp.maximum(m, sc.max(axis=-1, keepdims=True))
            acc = None
            den = None
            for sc, (_, vals, _) in zip(scores, parts):
                p = jnp.exp(sc - m)
                d = jnp.sum(p, axis=-1, keepdims=True)
                o = _mm(p, vals)
                acc = o if acc is None else acc + o
                den = d if den is None else den + d
            outs.append(acc / den)
        return jnp.where(low, outs[0], outs[1])

    for gi, (pid, base) in enumerate(zip(pat_ids, bases)):
        qn = rmsn(q_ref[gi * nq:(gi + 1) * nq, :].astype(F32), qw) * scale
        kwin = kn_s[base * GRID_W:base * GRID_W + nk, :]
        vwin = v_ref[base * GRID_W:base * GRID_W + nk, :]
        bias = (bias_ref[0, pid], bias_ref[1, pid])
        res = attend(qn, [(kwin, vwin, bias), (kcn, vc, None)])
        out_ref[gi * nq:(gi + 1) * nq, :] = res.astype(out_ref.dtype)

    if need_ctx:
        qn = rmsn(qc_ref[...].astype(F32), qw) * scale
        outc_ref[...] = attend(qn, [(kcn, vc, None)]).astype(outc_ref.dtype)


def _na(pb, qn_w, kn_w, rpb, batch, seq, ctx_len, need_ctx):
    n_rows = seq // GRID_W
    pat_ids, bases, row_idx = _na_patterns(n_rows)
    npat = row_idx.shape[0]
    nq, nk = NA_QROWS * GRID_W, NA_KROWS * GRID_W
    bias = _na_bias_table(rpb.astype(F32), row_idx)
    n_lat = batch * seq
    pairs = NA_HEADS // 2
    qoff, koff, voff = 0, pairs, 2 * pairs

    in_specs = [pl.BlockSpec((seq, LANES), lambda p, b: (b, qoff + p)),
                pl.BlockSpec((seq, LANES), lambda p, b: (b, koff + p)),
                pl.BlockSpec((seq, LANES), lambda p, b: (b, voff + p)),
                pl.BlockSpec((ctx_len, LANES), lambda p, b: (n_lat // ctx_len + b, koff + p)),
                pl.BlockSpec((ctx_len, LANES), lambda p, b: (n_lat // ctx_len + b, voff + p))]
    args = [pb, pb, pb, pb, pb]
    if need_ctx:
        in_specs.append(pl.BlockSpec((ctx_len, LANES), lambda p, b: (n_lat // ctx_len + b, qoff + p)))
        args.append(pb)
    in_specs += [pl.BlockSpec((2, npat, nq, nk), lambda p, b: (p, 0, 0, 0)),
                 pl.BlockSpec((1, LANES), lambda p, b: (0, 0)),
                 pl.BlockSpec((1, LANES), lambda p, b: (0, 0))]
    args += [bias, jnp.tile(qn_w.astype(F32), 2).reshape(1, LANES), jnp.tile(kn_w.astype(F32), 2).reshape(1, LANES)]
    out_specs = [pl.BlockSpec((seq, LANES), lambda p, b: (b, p))]
    out_shape = [jax.ShapeDtypeStruct((n_lat, NA_W), BF16)]
    if need_ctx:
        out_specs.append(pl.BlockSpec((ctx_len, LANES), lambda p, b: (b, p)))
        out_shape.append(jax.ShapeDtypeStruct((batch * ctx_len, NA_W), BF16))
    outs = pl.pallas_call(
        functools.partial(_na_kernel, pat_ids, bases, need_ctx),
        grid=(pairs, batch),
        in_specs=in_specs,
        out_specs=out_specs,
        out_shape=out_shape,
        scratch_shapes=[pltpu.VMEM((seq, LANES), MXU_DTYPE), pltpu.VMEM((ctx_len, LANES), MXU_DTYPE)],
        compiler_params=_cparams(("parallel", "parallel")),
        name="na_attn",
    )(*args)
    return (outs[0], outs[1]) if need_ctx else (outs[0], None)


def _conv_kernel(t_len, u_ref, w_ref, cb_ref, lw_ref, lb_ref, o_ref, pad_s):
    zeros = jnp.zeros((CONV_PAD, CONV_CH), F32)
    pad_s[0:CONV_PAD, :] = zeros
    pad_s[CONV_PAD + t_len:2 * CONV_PAD + t_len, :] = zeros
    pad_s[CONV_PAD:CONV_PAD + t_len, :] = u_ref[...].astype(F32)
    shift = CONV_PAD - CONV_WIDTH // 2

    def body(c, carry):
        r0 = pl.multiple_of(c * CONV_ROWS, CONV_ROWS)
        win = pad_s[pl.ds(r0, CONV_ROWS + 2 * CONV_PAD), :]
        acc = jnp.zeros((CONV_ROWS, CONV_CH), F32) + cb_ref[...]
        for j in range(CONV_WIDTH):
            acc = acc + win[j + shift:j + shift + CONV_ROWS, :] * w_ref[j:j + 1, :]
        mean = jnp.mean(acc, axis=-1, keepdims=True)
        xc = acc - mean
        var = jnp.mean(xc * xc, axis=-1, keepdims=True)
        y = xc * lax.rsqrt(var + EPS) * lw_ref[...] + lb_ref[...]
        o_ref[pl.ds(r0, CONV_ROWS), :] = (y * jax.nn.sigmoid(y)).astype(o_ref.dtype)
        return carry

    lax.fori_loop(0, t_len // CONV_ROWS, body, 0)


def _conv(u, conv_w, conv_b, ln_w, ln_b, first_block, n_seq, t_len):
    row = lambda a: a.reshape(1, CONV_CH).astype(F32)
    return pl.pallas_call(
        functools.partial(_conv_kernel, t_len),
        grid=(n_seq,),
        in_specs=[pl.BlockSpec((t_len, CONV_CH), lambda b: (first_block + b, 0)),
                  pl.BlockSpec((CONV_WIDTH, CONV_CH), lambda b: (0, 0)),
                  pl.BlockSpec((1, CONV_CH), lambda b: (0, 0)),
                  pl.BlockSpec((1, CONV_CH), lambda b: (0, 0)),
                  pl.BlockSpec((1, CONV_CH), lambda b: (0, 0))],
        out_specs=pl.BlockSpec((t_len, CONV_CH), lambda b: (b, 0)),
        out_shape=jax.ShapeDtypeStruct((n_seq * t_len, CONV_CH), BF16),
        scratch_shapes=[pltpu.VMEM((t_len + 2 * CONV_PAD, CONV_CH), F32)],
        compiler_params=_cparams(("parallel",)),
        name="conv_module",
    )(u, conv_w.astype(F32), row(conv_b), row(ln_w), row(ln_b))


def _outproj_kernel(a_ref, b_ref, c_ref, x_ref, g1_ref, sh_ref, sc_ref, nw_ref, wa_ref, wb_ref, wc_ref,
                    rw_ref, rb_ref, xo_ref, hx_ref, idx_ref, gate_ref, cnt_ref):
    mix = _mm(a_ref[...], wa_ref[...]) + _mm(b_ref[...], wb_ref[...]) + _mm(c_ref[...], wc_ref[...])
    xn = x_ref[...] + g1_ref[0] * mix
    xo_ref[...] = xn
    ms = jnp.mean(xn * xn, axis=-1, keepdims=True)
    hx = xn * lax.rsqrt(ms + EPS) * nw_ref[...] * (1.0 + sc_ref[0]) + sh_ref[0]
    hx_ref[...] = hx
    logits = jnp.dot(hx, rw_ref[...], preferred_element_type=F32, precision=lax.Precision.HIGHEST) + rb_ref[...]
    lane = lax.broadcasted_iota(jnp.int32, logits.shape, 1)
    idx_out = jnp.zeros(logits.shape, jnp.int32)
    val_out = jnp.zeros(logits.shape, F32)
    top = None
    den = None
    sels = []
    for kk in range(TOP_K):
        m = jnp.max(logits, axis=-1, keepdims=True)
        sel = jnp.min(jnp.where(logits == m, lane, LANES), axis=-1, keepdims=True)
        if kk == 0:
            top = m
        e = jnp.exp(m - top)
        den = e if den is None else den + e
        idx_out = jnp.where(lane == kk, sel, idx_out)
        val_out = jnp.where(lane == kk, e, val_out)
        logits = jnp.where(lane == sel, -jnp.inf, logits)
        sels.append(sel)
    gate_ref[...] = val_out / den
    tm = logits.shape[0]
    chosen = jnp.where(logits == -jnp.inf, 1.0, 0.0)
    ti = lax.broadcasted_iota(jnp.int32, (tm, tm), 0)
    si = lax.broadcasted_iota(jnp.int32, (tm, tm), 1)
    earlier = jnp.where(si < ti, 1.0, 0.0).astype(BF16)
    before = jnp.dot(earlier, chosen.astype(BF16), preferred_element_type=F32)
    for kk in range(TOP_K):
        rank = jnp.sum(jnp.where(lane == sels[kk], before, 0.0), axis=-1, keepdims=True)
        idx_out = jnp.where(lane == TOP_K + kk, rank.astype(jnp.int32), idx_out)
    idx_ref[...] = idx_out
    cnt_ref[0] = jnp.broadcast_to(jnp.sum(chosen, axis=0, keepdims=True), cnt_ref.shape[1:]).astype(jnp.int32)


def _outproj(a, b, c, xall, g1, sh2, sc2, norm_w, w_out, router_w, router_b, n_rows, n_lat, seq):
    n, d = xall.shape
    tm = TOK_TILE
    n_lat_tiles = n_lat // tm
    per_batch = seq // tm
    n_mod = g1.shape[0]

    def mod_map(i):
        return (jnp.where(i < n_lat_tiles, i // per_batch, n_mod - 1), 0, 0)

    wa = w_out[0:MLSTM_W].astype(MXU_DTYPE)
    wb = w_out[MLSTM_W:MLSTM_W + NA_W].astype(MXU_DTYPE)
    wc = w_out[MLSTM_W + NA_W:].astype(MXU_DTYPE)
    rw = jnp.zeros((d, LANES), F32).at[:, :N_EXPERTS].set(router_w.astype(F32))
    rb = jnp.full((1, LANES), NEG_BIG, F32).at[0, :N_EXPERTS].set(router_b.astype(F32))
    full = lambda r, cc: pl.BlockSpec((r, cc), lambda i: (0, 0))
    tile = lambda cc: pl.BlockSpec((tm, cc), lambda i: (i, 0))
    return pl.pallas_call(
        _outproj_kernel,
        grid=(n_rows // tm,),
        in_specs=[tile(MLSTM_W), tile(NA_W), tile(CONV_CH), tile(d),
                  pl.BlockSpec((1, 1, d), mod_map), pl.BlockSpec((1, 1, d), mod_map),
                  pl.BlockSpec((1, 1, d), mod_map), full(1, d),
                  full(MLSTM_W, d), full(NA_W, d), full(CONV_CH, d), full(d, LANES), full(1, LANES)],
        out_specs=[tile(d), tile(d), tile(LANES), tile(LANES),
                   pl.BlockSpec((1, 8, LANES), lambda i: (i, 0, 0))],
        out_shape=[jax.ShapeDtypeStruct((n_rows, d), F32), jax.ShapeDtypeStruct((n_rows, d), F32),
                   jax.ShapeDtypeStruct((n_rows, LANES), jnp.int32), jax.ShapeDtypeStruct((n_rows, LANES), F32),
                   jax.ShapeDtypeStruct((n_rows // tm, 8, LANES), jnp.int32)],
        compiler_params=_cparams(("parallel",)),
        name="out_proj",
    )(a, b, c, xall, g1, sh2, sc2, norm_w.reshape(1, d).astype(F32), wa, wb, wc, rw, rb)


def _dispatch_kernel(tm, dest_ref, hx_ref, xs_in, xs_out, sem):
    del xs_in

    def body(r, carry):
        for kk in range(TOP_K):
            dst = dest_ref[0, 0, kk * tm + r]
            pltpu.make_async_copy(hx_ref.at[pl.ds(r, 1), :], xs_out.at[pl.ds(dst, 1), :], sem).start()
        return carry

    lax.fori_loop(0, tm, body, 0)
    for kk in range(TOP_K):
        pltpu.make_async_copy(hx_ref, xs_out.at[pl.ds(0, tm), :], sem).wait()


def _dispatch(hx, dest_tiles, xs_init, tm):
    n, d = hx.shape
    return pl.pallas_call(
        functools.partial(_dispatch_kernel, tm),
        grid=(n // tm,),
        in_specs=[pl.BlockSpec((1, 1, TOP_K * tm), lambda i: (i, 0, 0), memory_space=pltpu.SMEM),
                  pl.BlockSpec((tm, d), lambda i: (i, 0)),
                  pl.BlockSpec(memory_space=pl.ANY)],
        out_specs=pl.BlockSpec(memory_space=pl.ANY),
        out_shape=jax.ShapeDtypeStruct(xs_init.shape, xs_init.dtype),
        scratch_shapes=[pltpu.SemaphoreType.DMA],
        input_output_aliases={2: 0},
        compiler_params=_cparams(("arbitrary",)),
        name="moe_dispatch",
    )(dest_tiles, hx, xs_init)


def _expert_kernel(be_ref, nu_ref, x_ref, w1_ref, b1_ref, w2_ref, b2_ref, o_ref, w1_s, w2_s):
    i = pl.program_id(0)
    de = w2_ref.shape[1]

    @pl.when(jnp.logical_or(i == 0, be_ref[i] != be_ref[jnp.maximum(i - 1, 0)]))
    def _():
        w1_s[...] = w1_ref[0].astype(w1_s.dtype)
        w2_s[...] = w2_ref[0].astype(w2_s.dtype)

    @pl.when(i < nu_ref[0])
    def _():
        h = _mm(x_ref[...], w1_s[...]) + b1_ref[0]
        glu = jnp.minimum(h[:, :de], SWIGLU_LIMIT)
        lin = jnp.clip(h[:, de:], -SWIGLU_LIMIT, SWIGLU_LIMIT)
        act = (lin + 1.0) * glu * jax.nn.sigmoid(SWIGLU_ALPHA * glu)
        o_ref[...] = _mm(act, w2_s[...]) + b2_ref[0]

    @pl.when(i >= nu_ref[0])
    def _():
        o_ref[...] = jnp.zeros(o_ref.shape, o_ref.dtype)


def _experts(xs, n_blocks, block_e, n_used, w1, b1, w2, b2):
    d = xs.shape[1]
    ne, _, two_de = w1.shape
    de = w2.shape[1]
    bm = MOE_BLOCK
    grid_spec = pltpu.PrefetchScalarGridSpec(
        num_scalar_prefetch=2,
        grid=(n_blocks,),
        in_specs=[pl.BlockSpec((bm, d), lambda i, be, nu: (i, 0)),
                  pl.BlockSpec((1, d, two_de), lambda i, be, nu: (be[i], 0, 0)),
                  pl.BlockSpec((1, 1, two_de), lambda i, be, nu: (be[i], 0, 0)),
                  pl.BlockSpec((1, de, d), lambda i, be, nu: (be[i], 0, 0)),
                  pl.BlockSpec((1, 1, d), lambda i, be, nu: (be[i], 0, 0))],
        out_specs=pl.BlockSpec((bm, d), lambda i, be, nu: (i, 0)),
        scratch_shapes=[pltpu.VMEM((d, two_de), MXU_DTYPE), pltpu.VMEM((de, d), MXU_DTYPE)],
    )
    return pl.pallas_call(
        _expert_kernel,
        grid_spec=grid_spec,
        out_shape=jax.ShapeDtypeStruct((n_blocks * bm, d), F32),
        compiler_params=_cparams(("arbitrary",)),
        name="moe_experts",
    )(block_e, n_used, xs, w1.astype(F32), b1.reshape(ne, 1, two_de).astype(F32), w2.astype(F32),
      b2.reshape(ne, 1, d).astype(F32))


def _combine_kernel(tm, dest_ref, eo_hbm, x_ref, gate_ref, g2_ref, xo_ref, buf, sem):
    def body(r, carry):
        for kk in range(TOP_K):
            src = dest_ref[0, 0, kk * tm + r]
            pltpu.make_async_copy(eo_hbm.at[pl.ds(src, 1), :], buf.at[kk, pl.ds(r, 1), :], sem).start()
        return carry

    lax.fori_loop(0, tm, body, 0)
    for kk in range(TOP_K):
        pltpu.make_async_copy(eo_hbm.at[pl.ds(0, tm), :], buf.at[kk], sem).wait()
    gates = gate_ref[...]
    y = gates[:, 0:1] * buf[0]
    for kk in range(1, TOP_K):
        y = y + gates[:, kk:kk + 1] * buf[kk]
    xo_ref[...] = x_ref[...] + g2_ref[0] * y


def _combine(eo, dest_tiles, xres, gates, g2, n_rows, n_lat, seq, tm):
    d = xres.shape[1]
    n_lat_tiles = n_lat // tm
    per_batch = seq // tm
    n_mod = g2.shape[0]

    def mod_map(i):
        return (jnp.where(i < n_lat_tiles, i // per_batch, n_mod - 1), 0, 0)

    return pl.pallas_call(
        functools.partial(_combine_kernel, tm),
        grid=(n_rows // tm,),
        in_specs=[pl.BlockSpec((1, 1, TOP_K * tm), lambda i: (i, 0, 0), memory_space=pltpu.SMEM),
                  pl.BlockSpec(memory_space=pl.ANY),
                  pl.BlockSpec((tm, d), lambda i: (i, 0)),
                  pl.BlockSpec((tm, LANES), lambda i: (i, 0)),
                  pl.BlockSpec((1, 1, d), mod_map)],
        out_specs=pl.BlockSpec((tm, d), lambda i: (i, 0)),
        out_shape=jax.ShapeDtypeStruct((n_rows, d), F32),
        scratch_shapes=[pltpu.VMEM((TOP_K, tm, d), F32), pltpu.SemaphoreType.DMA],
        compiler_params=_cparams(("arbitrary",)),
        name="moe_combine",
    )(dest_tiles, eo, xres, gates, g2)


def _dest_tiles(dest, tm):
    n = dest.shape[0]
    return dest.reshape(n // tm, tm, TOP_K).transpose(0, 2, 1).reshape(n // tm, 1, TOP_K * tm)


def _route(idx, counts, tm, n_blocks):
    bm = MOE_BLOCK
    n = idx.shape[0]
    tile_before = jnp.cumsum(counts, axis=0) - counts
    total = jnp.sum(counts, axis=0)
    padded = (total + bm - 1) // bm * bm
    pad_end = jnp.cumsum(padded)
    pad_start = pad_end - padded
    base = (tile_before + pad_start[None, :])[:, None, None, :]
    experts = idx[:, :TOP_K].reshape(n // tm, tm, TOP_K)
    eid = lax.broadcasted_iota(jnp.int32, (1, 1, 1, N_EXPERTS), 3)
    first = jnp.sum(jnp.where(experts[..., None] == eid, base, 0), axis=-1)
    dest = first.reshape(n, TOP_K) + idx[:, TOP_K:2 * TOP_K]
    block_start = jnp.arange(n_blocks, dtype=jnp.int32) * bm
    block_e = jnp.minimum(jnp.sum(pad_end[None, :] <= block_start[:, None], axis=1), N_EXPERTS - 1)
    n_used = (pad_end[-1] // bm).astype(jnp.int32).reshape(1)
    return dest.astype(jnp.int32), block_e.astype(jnp.int32), n_used


def _moe(hx, idx, counts, gates, xres, g2, layer, w1, b1, w2, b2, xs_buf, n_out_rows, n_lat, seq):
    n = hx.shape[0]
    n_blocks = -(-(n * TOP_K + N_EXPERTS * (MOE_BLOCK - 1)) // MOE_BLOCK)
    dest, block_e, n_used = _route(idx, counts[:, 0, :N_EXPERTS], TOK_TILE, n_blocks)
    xs = _dispatch(hx, _dest_tiles(dest, TOK_TILE), xs_buf, TOK_TILE)
    eo = _experts(xs, n_blocks, block_e + layer * N_EXPERTS, n_used, w1, b1, w2, b2)
    out = _combine(eo, _dest_tiles(dest[:n_out_rows], COMBINE_TILE), xres, gates, g2, n_out_rows, n_lat, seq,
                   COMBINE_TILE)
    return out, xs


def kernel(x, c, ctx, c_ctx, norm_mix_w, norm_ffn_w, w_ada, b_ada, w_in, mlstm_ig_b, mlstm_fg_b, mlstm_norm_w,
           na_qnorm_w, na_knorm_w, na_rpb, conv_w, conv_b, conv_ln_w, conv_ln_b, w_out, router_w, router_b,
           exp_w1, exp_b1, exp_w2, exp_b2):
    batch, seq, d = x.shape
    ctx_len = ctx.shape[1]
    depth = w_ada.shape[0]
    n_lat = batch * seq
    n_ctx = batch * ctx_len
    n_all = n_lat + n_ctx
    assert seq % TOK_TILE == 0 and n_ctx % TOK_TILE == 0 and seq % ctx_len == 0
    assert (seq // GRID_W) % NA_QROWS == 0 and seq // GRID_W >= NA_KROWS

    mod_rows = -(-(batch + 1) // 8) * 8
    cc = jnp.zeros((mod_rows, d), F32).at[:batch].set(c).at[batch].set(c_ctx)
    mods = _ada(cc, w_ada, b_ada)[:, :batch + 1].reshape(depth, batch + 1, 1, 6, d)
    xall = jnp.concatenate([x.reshape(n_lat, d), ctx.reshape(n_ctx, d)], axis=0)
    rope = _rope_tables(seq)
    n_blocks0 = -(-(n_all * TOP_K + N_EXPERTS * (MOE_BLOCK - 1)) // MOE_BLOCK)
    xs_buf = jnp.zeros((n_blocks0 * MOE_BLOCK, d), F32)
    n_exp = exp_w1.shape[1]
    ew1 = exp_w1.reshape((depth * n_exp,) + exp_w1.shape[2:])
    eb1 = exp_b1.reshape((depth * n_exp,) + exp_b1.shape[2:])
    ew2 = exp_w2.reshape((depth * n_exp,) + exp_w2.shape[2:])
    eb2 = exp_b2.reshape((depth * n_exp,) + exp_b2.shape[2:])

    for l in range(depth):
        need_ctx = l < depth - 1
        sh1, sc1, g1, sh2, sc2, g2 = [mods[l, :, :, i, :] for i in range(6)]
        pa, g, gt, pb, u = _inproj(xall, sh1, sc1, norm_mix_w[l], w_in[l], n_lat, seq)
        a_lat, a_ctx = _mlstm(pa, g, gt, mlstm_ig_b[l], mlstm_fg_b[l], mlstm_norm_w[l], rope,
                              batch, seq, ctx_len, need_ctx)
        b_lat, b_ctx = _na(pb, na_qnorm_w[l], na_knorm_w[l], na_rpb[l], batch, seq, ctx_len, need_ctx)
        c_lat = _conv(u, conv_w[l], conv_b[l], conv_ln_w[l], conv_ln_b[l], 0, batch, seq)
        if need_ctx:
            c_ctx_out = _conv(u, conv_w[l], conv_b[l], conv_ln_w[l], conv_ln_b[l], n_lat // ctx_len, batch, ctx_len)
            a_all = jnp.concatenate([a_lat, a_ctx], axis=0)
            b_all = jnp.concatenate([b_lat, b_ctx], axis=0)
            c_all = jnp.concatenate([c_lat, c_ctx_out], axis=0)
            n_rows = n_all
        else:
            a_all, b_all, c_all = a_lat, b_lat, c_lat
            n_rows = n_lat
        xmid, hx, idx, gates, counts = _outproj(a_all, b_all, c_all, xall, g1, sh2, sc2, norm_ffn_w[l], w_out[l],
                                        router_w[l], router_b[l], n_rows, n_lat, seq)
        xall, xs_buf = _moe(hx, idx, counts, gates, xmid, g2, l, ew1, eb1, ew2, eb2, xs_buf, n_rows, n_lat, seq)
    return xall[:n_lat].reshape(batch, seq, d)
```

```python
import functools

import numpy as np
import jax
import jax.numpy as jnp
from jax import lax
from jax.experimental import pallas as pl
from jax.experimental.pallas import tpu as pltpu

F32 = jnp.float32
BF16 = jnp.bfloat16
MXU_DTYPE = BF16

GRID_W = 64
HEAD_DIM = 64
MLSTM_HEADS = 4
NA_HEADS = 8
CONV_CH = 256
MLSTM_W = MLSTM_HEADS * HEAD_DIM
NA_W = NA_HEADS * HEAD_DIM
NA_WIN_R = 8
NA_WIN_C = 16
CONV_WIDTH = 31
ROPE_BASE = 10000.0
N_EXPERTS = 32
TOP_K = 4
SWIGLU_LIMIT = 7.0
SWIGLU_ALPHA = 1.702
EPS = 1e-6

A_Q = 0
A_G = 4 * MLSTM_W
B_Q = A_G + 4 * MLSTM_HEADS
C_A = B_Q + 3 * NA_W
IN_COLS = C_A + 2 * CONV_CH

LANES = 128
SUBLANES = 8
NEG_BIG = -1e30
VMEM_LIMIT = 56 * 1024 * 1024

NA_QROWS = 4
NA_KROWS = NA_QROWS + NA_WIN_R - 1
TOK_TILE = 512
MOE_BLOCK = 512
ROUTE_TILE = 256
COMBINE_CHUNKS = (256, 128, 64, 32, 16, 8)
CONV_ROWS = 64
CONV_PAD = 16


def _mm(a, b):
    return jnp.dot(a.astype(MXU_DTYPE), b.astype(MXU_DTYPE), preferred_element_type=F32)


def _mm_nt(a, b):
    return lax.dot_general(a.astype(MXU_DTYPE), b.astype(MXU_DTYPE), (((1,), (1,)), ((), ())),
                           preferred_element_type=F32)


def _mm_tn(a, b):
    return lax.dot_general(a.astype(MXU_DTYPE), b.astype(MXU_DTYPE), (((0,), (0,)), ((), ())),
                           preferred_element_type=F32)


def _cparams(sem):
    return pltpu.CompilerParams(dimension_semantics=sem, vmem_limit_bytes=VMEM_LIMIT)


def _ada_kernel(c_ref, w_ref, b_ref, o_ref):
    cc = c_ref[...]
    s = cc * jax.nn.sigmoid(cc)
    o_ref[0] = _mm(s, w_ref[0]) + b_ref[0]


def _ada(cc, w_ada, b_ada):
    depth, d, n = w_ada.shape
    rows = cc.shape[0]
    tn = 512
    return pl.pallas_call(
        _ada_kernel,
        grid=(depth, n // tn),
        in_specs=[pl.BlockSpec((rows, d), lambda l, j: (0, 0)),
                  pl.BlockSpec((1, d, tn), lambda l, j: (l, 0, j)),
                  pl.BlockSpec((1, 1, tn), lambda l, j: (l, 0, j))],
        out_specs=pl.BlockSpec((1, rows, tn), lambda l, j: (l, 0, j)),
        out_shape=jax.ShapeDtypeStruct((depth, rows, n), F32),
        compiler_params=_cparams(("parallel", "parallel")),
        name="ada_mod",
    )(cc, w_ada, b_ada.reshape(depth, 1, n))


def _inproj_kernel(x_ref, sh_ref, sc_ref, nw_ref, wa_ref, wg_ref, wgt_ref, wb_ref, wc_ref,
                   pa_ref, g_ref, gt_ref, pb_ref, u_ref):
    x = x_ref[...]
    ms = jnp.mean(x * x, axis=-1, keepdims=True)
    y = x * lax.rsqrt(ms + EPS) * nw_ref[...]
    h = (y * (1.0 + sc_ref[0]) + sh_ref[0]).astype(MXU_DTYPE)
    pa_ref[...] = _mm(h, wa_ref[...]).astype(pa_ref.dtype)
    g_ref[...] = _mm(h, wg_ref[...])
    gt_ref[...] = _mm_nt(wgt_ref[...], h)
    pb_ref[...] = _mm(h, wb_ref[...]).astype(pb_ref.dtype)
    pc = _mm(h, wc_ref[...])
    u_ref[...] = (pc[:, :CONV_CH] * jax.nn.sigmoid(pc[:, CONV_CH:])).astype(u_ref.dtype)


def _inproj(xall, shift, scale, norm_w, w_in, n_lat, seq):
    n, d = xall.shape
    tm = TOK_TILE
    n_lat_tiles = n_lat // tm
    per_batch = seq // tm
    n_mod = shift.shape[0]

    def mod_map(i):
        return (jnp.where(i < n_lat_tiles, i // per_batch, n_mod - 1), 0, 0)

    wa = w_in[:, A_Q:A_G].astype(MXU_DTYPE)
    wg = w_in[:, A_G:B_Q].astype(MXU_DTYPE)
    wb = w_in[:, B_Q:C_A].astype(MXU_DTYPE)
    wc = w_in[:, C_A:IN_COLS].astype(MXU_DTYPE)
    ng = B_Q - A_G
    full = lambda r, c: pl.BlockSpec((r, c), lambda i: (0, 0))
    return pl.pallas_call(
        _inproj_kernel,
        grid=(n // tm,),
        in_specs=[pl.BlockSpec((tm, d), lambda i: (i, 0)),
                  pl.BlockSpec((1, 1, d), mod_map),
                  pl.BlockSpec((1, 1, d), mod_map),
                  full(1, d),
                  full(d, A_G), full(d, ng), full(ng, d), full(d, 3 * NA_W), full(d, 2 * CONV_CH)],
        out_specs=[pl.BlockSpec((tm, A_G), lambda i: (i, 0)),
                   pl.BlockSpec((tm, ng), lambda i: (i, 0)),
                   pl.BlockSpec((ng, tm), lambda i: (0, i)),
                   pl.BlockSpec((tm, 3 * NA_W), lambda i: (i, 0)),
                   pl.BlockSpec((tm, CONV_CH), lambda i: (i, 0))],
        out_shape=[jax.ShapeDtypeStruct((n, A_G), BF16),
                   jax.ShapeDtypeStruct((n, ng), F32),
                   jax.ShapeDtypeStruct((ng, n), F32),
                   jax.ShapeDtypeStruct((n, 3 * NA_W), BF16),
                   jax.ShapeDtypeStruct((n, CONV_CH), BF16)],
        compiler_params=_cparams(("parallel",)),
        name="in_proj",
    )(xall, shift, scale, norm_w.reshape(1, d), wa, wg, wg.T, wb, wc)


def _split3(x):
    hi = x.astype(BF16)
    r1 = x - hi.astype(F32)
    mid = r1.astype(BF16)
    lo = (r1 - mid.astype(F32)).astype(BF16)
    return hi, mid, lo


def _tri_left(tri, x):
    return sum(jnp.dot(tri, p, preferred_element_type=F32) for p in _split3(x))


def _tri_right(x, tri):
    return sum(jnp.dot(p, tri, preferred_element_type=F32) for p in _split3(x))


def _log_sigmoid(x):
    return jnp.minimum(x, 0.0) - jnp.log(1.0 + jnp.exp(-jnp.abs(x)))


def _mlstm_direction(z, q, k, v, gcol, grow, c_st, m_st, h_ref, row0, lc, with_output):
    nh = MLSTM_HEADS
    ti = lax.broadcasted_iota(jnp.int32, (lc, lc), 0)
    si = lax.broadcasted_iota(jnp.int32, (lc, lc), 1)
    lower = si <= ti
    upper = si >= ti
    tl = jnp.where(lower, 1.0, 0.0).astype(BF16)
    tu = jnp.where(upper, 1.0, 0.0).astype(BF16)
    i_col = gcol[:, z * nh:(z + 1) * nh]
    f_col = _log_sigmoid(gcol[:, 2 * nh + z * nh:2 * nh + (z + 1) * nh])
    i_row = grow[z * nh:(z + 1) * nh, :]
    f_row = _log_sigmoid(grow[2 * nh + z * nh:2 * nh + (z + 1) * nh, :])
    if z == 0:
        b_col = _tri_left(tl, f_col)
        b_row = _tri_right(f_row, tu)
        b_tot = b_col[lc - 1:lc, :]
        mask = lower
    else:
        b_col = _tri_left(tu, f_col)
        b_row = _tri_right(f_row, tl)
        b_tot = b_col[0:1, :]
        mask = upper
    lane256 = lax.broadcasted_iota(jnp.int32, (1, 4 * HEAD_DIM), 1)
    lane128 = lax.broadcasted_iota(jnp.int32, (1, LANES), 1)
    k_mx = k.astype(MXU_DTYPE)
    c_all = c_st[z].astype(MXU_DTYPE)
    for h in range(nh):
        r = z * nh + h
        bc = b_col[:, h:h + 1]
        br = b_row[h:h + 1, :]
        ic = i_col[:, h:h + 1]
        ir = i_row[h:h + 1, :]
        bl = b_tot[:, h:h + 1]
        m_old = m_st[r:r + 1, 0:1]
        vh = v[:, (h // 2) * LANES:(h // 2 + 1) * LANES]
        if h % 2 == 1:
            vh = pltpu.roll(vh, HEAD_DIM, 1)
        vext = jnp.where(lane128 < HEAD_DIM, vh, jnp.where(lane128 == HEAD_DIM, 1.0, 0.0)).astype(MXU_DTYPE)
        if with_output:
            qh = jnp.where(lane256 // HEAD_DIM == h, q, 0.0).astype(MXU_DTYPE)
            log_d = jnp.where(mask, bc - br + ir, NEG_BIG)
            inter = bc + m_old
            m_t = jnp.maximum(inter, jnp.max(log_d, axis=1, keepdims=True))
            s = _mm_nt(qh, k_mx) * jnp.exp(log_d - m_t)
            w_inter = jnp.exp(inter - m_t)
            nd = _mm(s, vext) + w_inter * _mm(qh, c_all)
            den = nd[:, HEAD_DIM:HEAD_DIM + 1]
            hval = nd / jnp.maximum(jnp.abs(den), jnp.exp(-m_t))
            h_ref[pl.ds(row0, lc), h * LANES:(h + 1) * LANES] = hval
        log_w = bl - bc + ic
        m_new = jnp.maximum(bl + m_old, jnp.max(log_w, axis=0, keepdims=True))
        w = jnp.exp(log_w - m_new)
        decay = jnp.exp(bl + m_old - m_new)
        upd = _mm_tn(k * w, vext)
        rows = slice(h * HEAD_DIM, (h + 1) * HEAD_DIM)
        c_st[z, rows, :] = decay * c_st[z, rows, :] + upd[rows, :]
        m_st[r:r + 1, :] = jnp.broadcast_to(m_new, (1, LANES))


def _mlstm_kernel(lc, nc, need_ctx,
                  pa_c, g_c, gt_c,
                  pa_f, g_f, gt_f, cos_f, sin_f,
                  pa_b, g_b, gt_b, cos_b, sin_b,
                  o_lat, o_ctx, brow_ref, bcol_ref, nw_ref, perm_ref,
                  *rest):
    if need_ctx:
        out_lat, out_ctx, hf, hb, c_st, m_st = rest
    else:
        out_lat, hf, hb, c_st, m_st = rest
        out_ctx = None
    s = pl.program_id(1)
    w = MLSTM_W
    k_scale = HEAD_DIM ** -0.5

    def load(pa, cos_ref, sin_ref):
        q = pa[:, 0:w].astype(F32)
        k = pa[:, w:2 * w].astype(F32)
        v = pa[:, 2 * w:3 * w].astype(F32)
        if cos_ref is not None:
            cs = cos_ref[...]
            sn = sin_ref[...]
            q = q * cs + _mm(q, perm_ref[...]) * sn
            k = k * cs + _mm(k, perm_ref[...]) * sn
        return q, k * k_scale, v

    @pl.when(s == 0)
    def _():
        c_st[...] = jnp.zeros(c_st.shape, F32)
        m_st[...] = jnp.zeros(m_st.shape, F32)
        q, k, v = load(pa_c, None, None)
        gcol = g_c[...] + brow_ref[...]
        grow = gt_c[...] + bcol_ref[...]
        for z, h_ref in ((0, hf), (1, hb)):
            _mlstm_direction(z, q, k, v, gcol, grow, c_st, m_st, h_ref, 0, lc, need_ctx)

    @pl.when(s > 0)
    def _():
        for z, h_ref, refs in ((0, hf, (pa_f, g_f, gt_f, cos_f, sin_f)),
                               (1, hb, (pa_b, g_b, gt_b, cos_b, sin_b))):
            pa, g, gt, cos_ref, sin_ref = refs
            j = s - 1 if z == 0 else nc - s
            row0 = pl.multiple_of(lc + j * lc, lc)
            q, k, v = load(pa, cos_ref, sin_ref)
            gcol = g[...] + brow_ref[...]
            grow = gt[...] + bcol_ref[...]
            _mlstm_direction(z, q, k, v, gcol, grow, c_st, m_st, h_ref, row0, lc, True)

    @pl.when(s == nc)
    def _():
        lane128 = lax.broadcasted_iota(jnp.int32, (1, LANES), 1)
        first = 0 if need_ctx else 1
        for ch in range(first, nc + 1):
            rows = slice(ch * lc, (ch + 1) * lc)
            if ch == 0:
                o_val, dst, dst_rows = o_ctx[...], out_ctx, slice(0, lc)
            else:
                dst_rows = slice((ch - 1) * lc, ch * lc)
                o_val, dst = o_lat[dst_rows, :], out_lat
            o_val = o_val.astype(F32)
            for p in range(MLSTM_HEADS // 2):
                pair = []
                for h in (2 * p, 2 * p + 1):
                    hv = hf[rows, h * LANES:(h + 1) * LANES] + hb[rows, h * LANES:(h + 1) * LANES]
                    hv = jnp.where(lane128 < HEAD_DIM, hv, 0.0)
                    ms = jnp.sum(hv * hv, axis=1, keepdims=True) * (1.0 / HEAD_DIM)
                    pair.append(hv * lax.rsqrt(ms + EPS))
                packed = jnp.where(lane128 < HEAD_DIM, pair[0], pltpu.roll(pair[1], HEAD_DIM, 1))
                cols = slice(p * LANES, (p + 1) * LANES)
                res = packed * nw_ref[:, cols] * jax.nn.sigmoid(o_val[:, cols])
                dst[dst_rows, cols] = res.astype(dst.dtype)


def _rope_tables(seq):
    half = HEAD_DIM // 2
    quarter = half // 2
    t = jnp.arange(seq, dtype=jnp.int32)
    inv_freq = ROPE_BASE ** (-jnp.arange(quarter, dtype=F32) / quarter)
    parts_c, parts_s = [], []
    for pos in (t // GRID_W, t % GRID_W):
        ang = pos.astype(F32)[:, None] * inv_freq[None, :]
        parts_c += [jnp.cos(ang), jnp.cos(ang)]
        parts_s += [-jnp.sin(ang), jnp.sin(ang)]
    cos = jnp.tile(jnp.concatenate(parts_c, axis=-1), (1, MLSTM_HEADS))
    sin = jnp.tile(jnp.concatenate(parts_s, axis=-1), (1, MLSTM_HEADS))
    j = np.arange(MLSTM_W)
    partner = np.where(j % half < quarter, j + quarter, j - quarter)
    perm = np.zeros((MLSTM_W, MLSTM_W), np.float32)
    perm[partner, j] = 1.0
    return cos, sin, jnp.asarray(perm, dtype=MXU_DTYPE)


def _mlstm(pa, g, gt, ig_b, fg_b, norm_w, rope, batch, seq, ctx_len, need_ctx):
    lc = ctx_len
    nc = seq // lc
    n_lat = batch * seq
    cos, sin, perm = rope
    bias = jnp.concatenate([ig_b.reshape(-1), fg_b.reshape(-1)]).astype(F32)
    ng = bias.shape[0]
    lat_blocks = n_lat // lc

    def fwd(b, s):
        return b * nc + jnp.maximum(s - 1, 0)

    def bwd(b, s):
        return b * nc + nc - jnp.maximum(s, 1)

    def fwd_c(b, s):
        return jnp.maximum(s - 1, 0)

    def bwd_c(b, s):
        return nc - jnp.maximum(s, 1)

    def lat_specs(chunk, chunk_c):
        return [pl.BlockSpec((lc, A_G), lambda b, s: (chunk(b, s), 0)),
                pl.BlockSpec((lc, ng), lambda b, s: (chunk(b, s), 0)),
                pl.BlockSpec((ng, lc), lambda b, s: (0, chunk(b, s))),
                pl.BlockSpec((lc, MLSTM_W), lambda b, s: (chunk_c(b, s), 0)),
                pl.BlockSpec((lc, MLSTM_W), lambda b, s: (chunk_c(b, s), 0))]

    in_specs = ([pl.BlockSpec((lc, A_G), lambda b, s: (lat_blocks + b, 0)),
                 pl.BlockSpec((lc, ng), lambda b, s: (lat_blocks + b, 0)),
                 pl.BlockSpec((ng, lc), lambda b, s: (0, lat_blocks + b))]
                + lat_specs(fwd, fwd_c) + lat_specs(bwd, bwd_c)
                + [pl.BlockSpec((seq, MLSTM_W), lambda b, s: (b, 3)),
                   pl.BlockSpec((lc, MLSTM_W), lambda b, s: (lat_blocks + b, 3)),
                   pl.BlockSpec((1, ng), lambda b, s: (0, 0)),
                   pl.BlockSpec((ng, 1), lambda b, s: (0, 0)),
                   pl.BlockSpec((1, MLSTM_W), lambda b, s: (0, 0)),
                   pl.BlockSpec((MLSTM_W, MLSTM_W), lambda b, s: (0, 0))])
    out_specs = [pl.BlockSpec((seq, MLSTM_W), lambda b, s: (b, 0))]
    out_shape = [jax.ShapeDtypeStruct((n_lat, MLSTM_W), BF16)]
    if need_ctx:
        out_specs.append(pl.BlockSpec((lc, MLSTM_W), lambda b, s: (b, 0)))
        out_shape.append(jax.ShapeDtypeStruct((batch * ctx_len, MLSTM_W), BF16))
    t_all = ctx_len + seq
    outs = pl.pallas_call(
        functools.partial(_mlstm_kernel, lc, nc, need_ctx),
        grid=(batch, nc + 1),
        in_specs=in_specs,
        out_specs=out_specs,
        out_shape=out_shape,
        scratch_shapes=[pltpu.VMEM((t_all, MLSTM_HEADS * LANES), F32),
                        pltpu.VMEM((t_all, MLSTM_HEADS * LANES), F32),
                        pltpu.VMEM((2, MLSTM_W, LANES), F32),
                        pltpu.VMEM((2 * MLSTM_HEADS, LANES), F32)],
        compiler_params=_cparams(("parallel", "arbitrary")),
        name="mlstm",
    )(pa, g, gt, pa, g, gt, cos, sin, pa, g, gt, cos, sin, pa, pa,
      bias.reshape(1, ng), bias.reshape(ng, 1), norm_w.reshape(1, MLSTM_W).astype(F32), perm)
    return (outs[0], outs[1]) if need_ctx else (outs[0], None)


def _na_patterns(n_rows):
    kr = min(NA_WIN_R, n_rows)
    n_dr = 2 * NA_WIN_R - 1
    pats, pat_ids, bases = [], [], []
    for gi in range(n_rows // NA_QROWS):
        base = int(np.clip(NA_QROWS * gi - NA_WIN_R // 2, 0, n_rows - NA_KROWS))
        dr = np.full((NA_QROWS, NA_KROWS), n_dr, np.int32)
        for qr in range(NA_QROWS):
            r = NA_QROWS * gi + qr
            r0 = int(np.clip(r - kr // 2, 0, n_rows - kr))
            for kj in range(NA_KROWS):
                if r0 <= base + kj < r0 + kr:
                    dr[qr, kj] = base + kj - r + NA_WIN_R - 1
        for pi, p in enumerate(pats):
            if np.array_equal(p, dr):
                pat_ids.append(pi)
                break
        else:
            pat_ids.append(len(pats))
            pats.append(dr)
        bases.append(base)
    return tuple(pat_ids), tuple(bases), np.stack(pats)


def _na_bias_table(rpb, row_idx):
    heads = rpb.shape[0]
    col = np.arange(GRID_W)
    col_start = np.clip(col - NA_WIN_C // 2, 0, GRID_W - NA_WIN_C)
    in_win = (col[None, :] >= col_start[:, None]) & (col[None, :] < col_start[:, None] + NA_WIN_C)
    dc = np.clip(col[None, :] - col[:, None] + NA_WIN_C - 1, 0, 2 * NA_WIN_C - 2)
    onehot = (dc[None] == np.arange(2 * NA_WIN_C - 1)[:, None, None]).astype(np.float32)
    planes = jnp.einsum('hdc,cqk->hdqk', rpb, onehot, precision=lax.Precision.HIGHEST)
    planes = jnp.where(in_win[None, None], planes, NEG_BIG)
    planes = jnp.concatenate([planes, jnp.full((heads, 1, GRID_W, GRID_W), NEG_BIG, F32)], axis=1)
    npat = row_idx.shape[0]
    tab = planes[:, row_idx.reshape(-1)].reshape(heads, npat, NA_QROWS, NA_KROWS, GRID_W, GRID_W)
    return tab.transpose(0, 1, 2, 4, 3, 5).reshape(heads, npat, NA_QROWS * GRID_W, NA_KROWS * GRID_W)


def _na_kernel(pat_ids, bases, need_ctx, q_ref, k_ref, v_ref, kc_ref, vc_ref, *rest):
    if need_ctx:
        qc_ref, bias_ref, qw_ref, kw_ref, out_ref, outc_ref, kn_s, kcn_s = rest
    else:
        bias_ref, qw_ref, kw_ref, out_ref, kn_s, kcn_s = rest
    lane = lax.broadcasted_iota(jnp.int32, (1, LANES), 1)
    low = lane < HEAD_DIM
    inv_d = 1.0 / HEAD_DIM

    def rmsn(x, w):
        x2 = x * x
        s0 = jnp.sum(jnp.where(low, x2, 0.0), axis=-1, keepdims=True)
        s1 = jnp.sum(jnp.where(low, 0.0, x2), axis=-1, keepdims=True)
        r = jnp.where(low, lax.rsqrt(s0 * inv_d + EPS), lax.rsqrt(s1 * inv_d + EPS))
        return x * r * w

    qw = qw_ref[...]
    kn_s[...] = rmsn(k_ref[...].astype(F32), kw_ref[...]).astype(kn_s.dtype)
    kcn_s[...] = rmsn(kc_ref[...].astype(F32), kw_ref[...]).astype(kcn_s.dtype)
    kcn = kcn_s[...]
    vc = vc_ref[...]
    scale = HEAD_DIM ** -0.5
    nq = NA_QROWS * GRID_W
    nk = NA_KROWS * GRID_W

    def attend(qn, parts):
        outs = []
        for hh in range(2):
            qh = jnp.where(low if hh == 0 else jnp.logical_not(low), qn, 0.0).astype(MXU_DTYPE)
            scores = []
            for keys, _, bias in parts:
                sc = _mm_nt(qh, keys)
                if bias is not None:
                    sc = sc + bias[hh]
                scores.append(sc)
            m = scores[0].max(axis=-1, keepdims=True)
            for sc in scores[1:]:
                m = jnp.maximum(m, sc.max(axis=-1, keepdims=True))
            acc = None
            den = None
            for sc, (_, vals, _) in zip(scores, parts):
                p = jnp.exp(sc - m)
                d = jnp.sum(p, axis=-1, keepdims=True)
                o = _mm(p, vals)
                acc = o if acc is None else acc + o
                den = d if den is None else den + d
            outs.append(acc / den)
        return jnp.where(low, outs[0], outs[1])

    for gi, (pid, base) in enumerate(zip(pat_ids, bases)):
        qn = rmsn(q_ref[gi * nq:(gi + 1) * nq, :].astype(F32), qw) * scale
        kwin = kn_s[base * GRID_W:base * GRID_W + nk, :]
        vwin = v_ref[base * GRID_W:base * GRID_W + nk, :]
        bias = (bias_ref[0, pid], bias_ref[1, pid])
        res = attend(qn, [(kwin, vwin, bias), (kcn, vc, None)])
        out_ref[gi * nq:(gi + 1) * nq, :] = res.astype(out_ref.dtype)

    if need_ctx:
        qn = rmsn(qc_ref[...].astype(F32), qw) * scale
        outc_ref[...] = attend(qn, [(kcn, vc, None)]).astype(outc_ref.dtype)


def _na(pb, qn_w, kn_w, rpb, batch, seq, ctx_len, need_ctx):
    n_rows = seq // GRID_W
    pat_ids, bases, row_idx = _na_patterns(n_rows)
    npat = row_idx.shape[0]
    nq, nk = NA_QROWS * GRID_W, NA_KROWS * GRID_W
    bias = _na_bias_table(rpb.astype(F32), row_idx)
    n_lat = batch * seq
    pairs = NA_HEADS // 2
    qoff, koff, voff = 0, pairs, 2 * pairs

    in_specs = [pl.BlockSpec((seq, LANES), lambda p, b: (b, qoff + p)),
                pl.BlockSpec((seq, LANES), lambda p, b: (b, koff + p)),
                pl.BlockSpec((seq, LANES), lambda p, b: (b, voff + p)),
                pl.BlockSpec((ctx_len, LANES), lambda p, b: (n_lat // ctx_len + b, koff + p)),
                pl.BlockSpec((ctx_len, LANES), lambda p, b: (n_lat // ctx_len + b, voff + p))]
    args = [pb, pb, pb, pb, pb]
    if need_ctx:
        in_specs.append(pl.BlockSpec((ctx_len, LANES), lambda p, b: (n_lat // ctx_len + b, qoff + p)))
        args.append(pb)
    in_specs += [pl.BlockSpec((2, npat, nq, nk), lambda p, b: (p, 0, 0, 0)),
                 pl.BlockSpec((1, LANES), lambda p, b: (0, 0)),
                 pl.BlockSpec((1, LANES), lambda p, b: (0, 0))]
    args += [bias, jnp.tile(qn_w.astype(F32), 2).reshape(1, LANES), jnp.tile(kn_w.astype(F32), 2).reshape(1, LANES)]
    out_specs = [pl.BlockSpec((seq, LANES), lambda p, b: (b, p))]
    out_shape = [jax.ShapeDtypeStruct((n_lat, NA_W), BF16)]
    if need_ctx:
        out_specs.append(pl.BlockSpec((ctx_len, LANES), lambda p, b: (b, p)))
        out_shape.append(jax.ShapeDtypeStruct((batch * ctx_len, NA_W), BF16))
    outs = pl.pallas_call(
        functools.partial(_na_kernel, pat_ids, bases, need_ctx),
        grid=(pairs, batch),
        in_specs=in_specs,
        out_specs=out_specs,
        out_shape=out_shape,
        scratch_shapes=[pltpu.VMEM((seq, LANES), MXU_DTYPE), pltpu.VMEM((ctx_len, LANES), MXU_DTYPE)],
        compiler_params=_cparams(("parallel", "parallel")),
        name="na_attn",
    )(*args)
    return (outs[0], outs[1]) if need_ctx else (outs[0], None)


def _conv_kernel(t_len, u_ref, w_ref, cb_ref, lw_ref, lb_ref, o_ref, pad_s):
    zeros = jnp.zeros((CONV_PAD, CONV_CH), F32)
    pad_s[0:CONV_PAD, :] = zeros
    pad_s[CONV_PAD + t_len:2 * CONV_PAD + t_len, :] = zeros
    pad_s[CONV_PAD:CONV_PAD + t_len, :] = u_ref[...].astype(F32)
    shift = CONV_PAD - CONV_WIDTH // 2

    def body(c, carry):
        r0 = pl.multiple_of(c * CONV_ROWS, CONV_ROWS)
        n_win = CONV_ROWS + 2 * CONV_PAD
        win = pad_s[pl.ds(r0, n_win), :]
        rot = [win] + [pltpu.roll(win, n_win - ph, 0) for ph in range(1, SUBLANES)]
        acc = jnp.zeros((CONV_ROWS, CONV_CH), F32) + cb_ref[...]
        for j in range(CONV_WIDTH):
            ph, al = (j + shift) % SUBLANES, (j + shift) // SUBLANES * SUBLANES
            acc = acc + rot[ph][al:al + CONV_ROWS, :] * w_ref[j:j + 1, :]
        mean = jnp.mean(acc, axis=-1, keepdims=True)
        xc = acc - mean
        var = jnp.mean(xc * xc, axis=-1, keepdims=True)
        y = xc * lax.rsqrt(var + EPS) * lw_ref[...] + lb_ref[...]
        o_ref[pl.ds(r0, CONV_ROWS), :] = (y * jax.nn.sigmoid(y)).astype(o_ref.dtype)
        return carry

    lax.fori_loop(0, t_len // CONV_ROWS, body, 0)


def _conv(u, conv_w, conv_b, ln_w, ln_b, first_block, n_seq, t_len):
    row = lambda a: a.reshape(1, CONV_CH).astype(F32)
    return pl.pallas_call(
        functools.partial(_conv_kernel, t_len),
        grid=(n_seq,),
        in_specs=[pl.BlockSpec((t_len, CONV_CH), lambda b: (first_block + b, 0)),
                  pl.BlockSpec((CONV_WIDTH, CONV_CH), lambda b: (0, 0)),
                  pl.BlockSpec((1, CONV_CH), lambda b: (0, 0)),
                  pl.BlockSpec((1, CONV_CH), lambda b: (0, 0)),
                  pl.BlockSpec((1, CONV_CH), lambda b: (0, 0))],
        out_specs=pl.BlockSpec((t_len, CONV_CH), lambda b: (b, 0)),
        out_shape=jax.ShapeDtypeStruct((n_seq * t_len, CONV_CH), BF16),
        scratch_shapes=[pltpu.VMEM((t_len + 2 * CONV_PAD, CONV_CH), F32)],
        compiler_params=_cparams(("parallel",)),
        name="conv_module",
    )(u, conv_w.astype(F32), row(conv_b), row(ln_w), row(ln_b))


def _outproj_kernel(rt, a_ref, b_ref, c_ref, x_ref, g1_ref, sh_ref, sc_ref, nw_ref, wa_ref, wb_ref, wc_ref,
                    rwh_ref, rwl_ref, rb_ref, xo_ref, hx_ref, idx_ref, gate_ref, cnt_ref):
    mix = _mm(a_ref[...], wa_ref[...]) + _mm(b_ref[...], wb_ref[...]) + _mm(c_ref[...], wc_ref[...])
    xn = x_ref[...] + g1_ref[0] * mix
    xo_ref[...] = xn
    ms = jnp.mean(xn * xn, axis=-1, keepdims=True)
    hx = xn * lax.rsqrt(ms + EPS) * nw_ref[...] * (1.0 + sc_ref[0]) + sh_ref[0]
    hx_ref[...] = hx
    h_hi = hx.astype(BF16)
    h_lo = (hx - h_hi.astype(F32)).astype(BF16)
    logits = (jnp.dot(h_hi, rwh_ref[...], preferred_element_type=F32)
              + jnp.dot(h_lo, rwh_ref[...], preferred_element_type=F32)
              + jnp.dot(h_hi, rwl_ref[...], preferred_element_type=F32)) + rb_ref[...]
    lane = lax.broadcasted_iota(jnp.int32, logits.shape, 1)
    idx_out = jnp.zeros(logits.shape, jnp.int32)
    val_out = jnp.zeros(logits.shape, F32)
    top = None
    den = None
    sels = []
    for kk in range(TOP_K):
        m = jnp.max(logits, axis=-1, keepdims=True)
        sel = jnp.min(jnp.where(logits == m, lane, LANES), axis=-1, keepdims=True)
        if kk == 0:
            top = m
        e = jnp.exp(m - top)
        den = e if den is None else den + e
        idx_out = jnp.where(lane == kk, sel, idx_out)
        val_out = jnp.where(lane == kk, e, val_out)
        logits = jnp.where(lane == sel, -jnp.inf, logits)
        sels.append(sel)
    gate_ref[...] = val_out / den
    tm = logits.shape[0]
    chosen = jnp.where(logits == -jnp.inf, 1.0, 0.0)
    ti = lax.broadcasted_iota(jnp.int32, (tm, tm), 0)
    si = lax.broadcasted_iota(jnp.int32, (tm, tm), 1)
    earlier = jnp.where(jnp.logical_and(si < ti, si // rt == ti // rt), 1.0, 0.0).astype(BF16)
    before = jnp.dot(earlier, chosen.astype(BF16), preferred_element_type=F32)
    for kk in range(TOP_K):
        rank = jnp.sum(jnp.where(lane == sels[kk], before, 0.0), axis=-1, keepdims=True)
        idx_out = jnp.where(lane == TOP_K + kk, rank.astype(jnp.int32), idx_out)
    idx_ref[...] = idx_out
    row = lax.broadcasted_iota(jnp.int32, cnt_ref.shape[1:], 0)
    cnt = jnp.zeros(cnt_ref.shape[1:], F32)
    for sub in range(tm // rt):
        cnt = jnp.where(row == sub, jnp.sum(chosen[sub * rt:(sub + 1) * rt], axis=0, keepdims=True), cnt)
    cnt_ref[0] = cnt.astype(jnp.int32)


def _outproj(a, b, c, xall, g1, sh2, sc2, norm_w, w_out, router_w, router_b, n_rows, n_lat, seq):
    n, d = xall.shape
    tm = TOK_TILE
    n_lat_tiles = n_lat // tm
    per_batch = seq // tm
    n_mod = g1.shape[0]

    def mod_map(i):
        return (jnp.where(i < n_lat_tiles, i // per_batch, n_mod - 1), 0, 0)

    wa = w_out[0:MLSTM_W].astype(MXU_DTYPE)
    wb = w_out[MLSTM_W:MLSTM_W + NA_W].astype(MXU_DTYPE)
    wc = w_out[MLSTM_W + NA_W:].astype(MXU_DTYPE)
    rw = jnp.zeros((d, LANES), F32).at[:, :N_EXPERTS].set(router_w.astype(F32))
    rw_hi = rw.astype(BF16)
    rw_lo = (rw - rw_hi.astype(F32)).astype(BF16)
    rb = jnp.full((1, LANES), NEG_BIG, F32).at[0, :N_EXPERTS].set(router_b.astype(F32))
    full = lambda r, cc: pl.BlockSpec((r, cc), lambda i: (0, 0))
    tile = lambda cc: pl.BlockSpec((tm, cc), lambda i: (i, 0))
    return pl.pallas_call(
        functools.partial(_outproj_kernel, ROUTE_TILE),
        grid=(n_rows // tm,),
        in_specs=[tile(MLSTM_W), tile(NA_W), tile(CONV_CH), tile(d),
                  pl.BlockSpec((1, 1, d), mod_map), pl.BlockSpec((1, 1, d), mod_map),
                  pl.BlockSpec((1, 1, d), mod_map), full(1, d),
                  full(MLSTM_W, d), full(NA_W, d), full(CONV_CH, d), full(d, LANES), full(d, LANES),
                  full(1, LANES)],
        out_specs=[tile(d), tile(d), tile(LANES), tile(LANES),
                   pl.BlockSpec((1, SUBLANES, LANES), lambda i: (i, 0, 0))],
        out_shape=[jax.ShapeDtypeStruct((n_rows, d), F32), jax.ShapeDtypeStruct((n_rows, d), F32),
                   jax.ShapeDtypeStruct((n_rows, LANES), jnp.int32), jax.ShapeDtypeStruct((n_rows, LANES), F32),
                   jax.ShapeDtypeStruct((n_rows // tm, SUBLANES, LANES), jnp.int32)],
        compiler_params=_cparams(("parallel",)),
        name="out_proj",
    )(a, b, c, xall, g1, sh2, sc2, norm_w.reshape(1, d).astype(F32), wa, wb, wc, rw_hi, rw_lo, rb)


def _dispatch_kernel(tm, dest_ref, hx_ref, xs_in, xs_out, sem):
    del xs_in

    def body(r, carry):
        for kk in range(TOP_K):
            dst = dest_ref[0, 0, kk * tm + r]
            pltpu.make_async_copy(hx_ref.at[pl.ds(r, 1), :], xs_out.at[pl.ds(dst, 1), :], sem).start()
        return carry

    lax.fori_loop(0, tm, body, 0)
    for kk in range(TOP_K):
        pltpu.make_async_copy(hx_ref, xs_out.at[pl.ds(0, tm), :], sem).wait()


def _dispatch(hx, dest_tiles, xs_init, tm):
    n, d = hx.shape
    return pl.pallas_call(
        functools.partial(_dispatch_kernel, tm),
        grid=(n // tm,),
        in_specs=[pl.BlockSpec((1, 1, TOP_K * tm), lambda i: (i, 0, 0), memory_space=pltpu.SMEM),
                  pl.BlockSpec((tm, d), lambda i: (i, 0)),
                  pl.BlockSpec(memory_space=pl.ANY)],
        out_specs=pl.BlockSpec(memory_space=pl.ANY),
        out_shape=jax.ShapeDtypeStruct(xs_init.shape, xs_init.dtype),
        scratch_shapes=[pltpu.SemaphoreType.DMA],
        input_output_aliases={2: 0},
        compiler_params=_cparams(("arbitrary",)),
        name="moe_dispatch",
    )(dest_tiles, hx, xs_init)


def _expert_kernel(be_ref, nu_ref, x_ref, w1_ref, b1_ref, w2_ref, b2_ref, o_ref, w1_s, w2_s):
    i = pl.program_id(0)
    de = w2_ref.shape[1]

    @pl.when(jnp.logical_or(i == 0, be_ref[i] != be_ref[jnp.maximum(i - 1, 0)]))
    def _():
        w1_s[...] = w1_ref[0].astype(w1_s.dtype)
        w2_s[...] = w2_ref[0].astype(w2_s.dtype)

    @pl.when(i < nu_ref[0])
    def _():
        h = _mm(x_ref[...], w1_s[...]) + b1_ref[0]
        glu = jnp.minimum(h[:, :de], SWIGLU_LIMIT)
        lin = jnp.clip(h[:, de:], -SWIGLU_LIMIT, SWIGLU_LIMIT)
        act = (lin + 1.0) * glu * jax.nn.sigmoid(SWIGLU_ALPHA * glu)
        o_ref[...] = _mm(act, w2_s[...]) + b2_ref[0]

    @pl.when(i >= nu_ref[0])
    def _():
        o_ref[...] = jnp.zeros(o_ref.shape, o_ref.dtype)


def _experts(xs, n_blocks, block_e, n_used, w1, b1, w2, b2):
    d = xs.shape[1]
    ne, _, two_de = w1.shape
    de = w2.shape[1]
    bm = MOE_BLOCK
    grid_spec = pltpu.PrefetchScalarGridSpec(
        num_scalar_prefetch=2,
        grid=(n_blocks,),
        in_specs=[pl.BlockSpec((bm, d), lambda i, be, nu: (i, 0)),
                  pl.BlockSpec((1, d, two_de), lambda i, be, nu: (be[i], 0, 0)),
                  pl.BlockSpec((1, 1, two_de), lambda i, be, nu: (be[i], 0, 0)),
                  pl.BlockSpec((1, de, d), lambda i, be, nu: (be[i], 0, 0)),
                  pl.BlockSpec((1, 1, d), lambda i, be, nu: (be[i], 0, 0))],
        out_specs=pl.BlockSpec((bm, d), lambda i, be, nu: (i, 0)),
        scratch_shapes=[pltpu.VMEM((d, two_de), MXU_DTYPE), pltpu.VMEM((de, d), MXU_DTYPE)],
    )
    return pl.pallas_call(
        _expert_kernel,
        grid_spec=grid_spec,
        out_shape=jax.ShapeDtypeStruct((n_blocks * bm, d), F32),
        compiler_params=_cparams(("arbitrary",)),
        name="moe_experts",
    )(block_e, n_used, xs, w1.astype(F32), b1.reshape(ne, 1, two_de).astype(F32), w2.astype(F32),
      b2.reshape(ne, 1, d).astype(F32))


def _combine_kernel(rt, src_ref, size_ref, soff_ref, eo_hbm, x_ref, pos_ref, gate_ref, g2_ref, xo_ref, stage, sem):
    i = pl.program_id(0)
    sr = stage.shape[0]

    @pl.when(i == 0)
    def _():
        stage[...] = jnp.zeros(stage.shape, stage.dtype)

    def run_chunks(e):
        base = i * N_EXPERTS + e
        src, size, dst = src_ref[base], size_ref[base], soff_ref[base]
        for chunk in COMBINE_CHUNKS:
            done = size & ~(2 * chunk - 1)
            copy = pltpu.make_async_copy(
                eo_hbm.at[pl.ds(pl.multiple_of(src + done, SUBLANES), chunk), :],
                stage.at[pl.ds(pl.multiple_of(dst + done, SUBLANES), chunk), :], sem)
            yield (size & chunk) != 0, copy

    for e in range(N_EXPERTS):
        for needed, copy in run_chunks(e):
            pl.when(needed)(copy.start)
    for e in range(N_EXPERTS):
        for needed, copy in run_chunks(e):
            pl.when(needed)(copy.wait)

    last = i * N_EXPERTS + N_EXPERTS - 1
    n_staged = soff_ref[last] + size_ref[last]
    row = lax.broadcasted_iota(jnp.int32, (sr, 1), 0)
    staged = jnp.where(row < n_staged, stage[...], 0.0).astype(MXU_DTYPE)
    col = lax.broadcasted_iota(jnp.int32, (rt, sr), 1)
    pos = pos_ref[...]
    gates = gate_ref[...]
    sel = jnp.zeros((rt, sr), F32)
    for kk in range(TOP_K):
        sel = jnp.where(col == pos[:, kk:kk + 1], gates[:, kk:kk + 1], sel)
    sel_hi = sel.astype(BF16)
    sel_lo = (sel - sel_hi.astype(F32)).astype(BF16)
    y = (jnp.dot(sel_hi.astype(MXU_DTYPE), staged, preferred_element_type=F32)
         + jnp.dot(sel_lo.astype(MXU_DTYPE), staged, preferred_element_type=F32))
    xo_ref[...] = x_ref[...] + g2_ref[0] * y


def _combine(eo, tables, pos, xres, gates, g2, n_rows, n_lat, seq):
    d = xres.shape[1]
    rt = ROUTE_TILE
    n_lat_tiles = n_lat // rt
    per_batch = seq // rt
    n_mod = g2.shape[0]
    sr = TOP_K * rt + N_EXPERTS * 2 * (SUBLANES - 1)
    sr = -(-sr // 256) * 256

    def mod_map(i, *_):
        return (jnp.where(i < n_lat_tiles, i // per_batch, n_mod - 1), 0, 0)

    grid_spec = pltpu.PrefetchScalarGridSpec(
        num_scalar_prefetch=3,
        grid=(n_rows // rt,),
        in_specs=[pl.BlockSpec(memory_space=pl.ANY),
                  pl.BlockSpec((rt, d), lambda i, *_: (i, 0)),
                  pl.BlockSpec((rt, TOP_K), lambda i, *_: (i, 0)),
                  pl.BlockSpec((rt, LANES), lambda i, *_: (i, 0)),
                  pl.BlockSpec((1, 1, d), mod_map)],
        out_specs=pl.BlockSpec((rt, d), lambda i, *_: (i, 0)),
        scratch_shapes=[pltpu.VMEM((sr, d), F32), pltpu.SemaphoreType.DMA],
    )
    return pl.pallas_call(
        functools.partial(_combine_kernel, rt),
        grid_spec=grid_spec,
        out_shape=jax.ShapeDtypeStruct((n_rows, d), F32),
        compiler_params=_cparams(("arbitrary",)),
        name="moe_combine",
    )(*tables, eo, xres, pos, gates, g2)


def _dest_tiles(dest, tm):
    n = dest.shape[0]
    return dest.reshape(n // tm, tm, TOP_K).transpose(0, 2, 1).reshape(n // tm, 1, TOP_K * tm)


def _route(idx, counts, n_blocks):
    bm = MOE_BLOCK
    rt = ROUTE_TILE
    n = idx.shape[0]
    tile_before = jnp.cumsum(counts, axis=0) - counts
    total = jnp.sum(counts, axis=0)
    padded = (total + bm - 1) // bm * bm
    pad_end = jnp.cumsum(padded)
    pad_start = pad_end - padded
    first_row = tile_before + pad_start[None, :]
    early = first_row % SUBLANES
    size = jnp.where(counts > 0, (counts + early + SUBLANES - 1) // SUBLANES * SUBLANES, 0)
    stage_off = jnp.cumsum(size, axis=1) - size
    experts = idx[:, :TOP_K].reshape(n // rt, rt, TOP_K)
    rank = idx[:, TOP_K:2 * TOP_K]
    onehot = experts[..., None] == lax.broadcasted_iota(jnp.int32, (1, 1, 1, N_EXPERTS), 3)
    lookup = lambda tab: jnp.sum(jnp.where(onehot, tab[:, None, None, :], 0), axis=-1).reshape(n, TOP_K)
    dest = lookup(first_row) + rank
    pos = lookup(stage_off + early) + rank
    tables = tuple(t.reshape(-1).astype(jnp.int32) for t in (first_row - early, size, stage_off))
    block_start = jnp.arange(n_blocks, dtype=jnp.int32) * bm
    block_e = jnp.minimum(jnp.sum(pad_end[None, :] <= block_start[:, None], axis=1), N_EXPERTS - 1)
    n_used = (pad_end[-1] // bm).astype(jnp.int32).reshape(1)
    return dest.astype(jnp.int32), pos.astype(jnp.int32), tables, block_e.astype(jnp.int32), n_used


def _moe(hx, idx, counts, gates, xres, g2, layer, w1, b1, w2, b2, xs_buf, n_out_rows, n_lat, seq):
    n = hx.shape[0]
    n_blocks = -(-(n * TOP_K + N_EXPERTS * (MOE_BLOCK - 1)) // MOE_BLOCK) + 1
    per_tile = TOK_TILE // ROUTE_TILE
    counts = counts[:, :per_tile, :N_EXPERTS].reshape(n // ROUTE_TILE, N_EXPERTS)
    dest, pos, tables, block_e, n_used = _route(idx, counts, n_blocks)
    xs = _dispatch(hx, _dest_tiles(dest, TOK_TILE), xs_buf, TOK_TILE)
    eo = _experts(xs, n_blocks, block_e + layer * N_EXPERTS, n_used, w1, b1, w2, b2)
    n_rt = n_out_rows // ROUTE_TILE
    tables = tuple(t[:n_rt * N_EXPERTS] for t in tables)
    out = _combine(eo, tables, pos[:n_out_rows], xres, gates, g2, n_out_rows, n_lat, seq)
    return out, xs


def kernel(x, c, ctx, c_ctx, norm_mix_w, norm_ffn_w, w_ada, b_ada, w_in, mlstm_ig_b, mlstm_fg_b, mlstm_norm_w,
           na_qnorm_w, na_knorm_w, na_rpb, conv_w, conv_b, conv_ln_w, conv_ln_b, w_out, router_w, router_b,
           exp_w1, exp_b1, exp_w2, exp_b2):
    batch, seq, d = x.shape
    ctx_len = ctx.shape[1]
    depth = w_ada.shape[0]
    n_lat = batch * seq
    n_ctx = batch * ctx_len
    n_all = n_lat + n_ctx
    assert seq % TOK_TILE == 0 and n_ctx % TOK_TILE == 0 and seq % ctx_len == 0
    assert (seq // GRID_W) % NA_QROWS == 0 and seq // GRID_W >= NA_KROWS

    mod_rows = -(-(batch + 1) // 8) * 8
    cc = jnp.zeros((mod_rows, d), F32).at[:batch].set(c).at[batch].set(c_ctx)
    mods = _ada(cc, w_ada, b_ada)[:, :batch + 1].reshape(depth, batch + 1, 1, 6, d)
    xall = jnp.concatenate([x.reshape(n_lat, d), ctx.reshape(n_ctx, d)], axis=0)
    rope = _rope_tables(seq)
    n_blocks0 = -(-(n_all * TOP_K + N_EXPERTS * (MOE_BLOCK - 1)) // MOE_BLOCK) + 1
    xs_buf = jnp.zeros((n_blocks0 * MOE_BLOCK, d), F32)
    n_exp = exp_w1.shape[1]
    ew1 = exp_w1.reshape((depth * n_exp,) + exp_w1.shape[2:])
    eb1 = exp_b1.reshape((depth * n_exp,) + exp_b1.shape[2:])
    ew2 = exp_w2.reshape((depth * n_exp,) + exp_w2.shape[2:])
    eb2 = exp_b2.reshape((depth * n_exp,) + exp_b2.shape[2:])

    for l in range(depth):
        need_ctx = l < depth - 1
        sh1, sc1, g1, sh2, sc2, g2 = [mods[l, :, :, i, :] for i in range(6)]
        pa, g, gt, pb, u = _inproj(xall, sh1, sc1, norm_mix_w[l], w_in[l], n_lat, seq)
        a_lat, a_ctx = _mlstm(pa, g, gt, mlstm_ig_b[l], mlstm_fg_b[l], mlstm_norm_w[l], rope,
                              batch, seq, ctx_len, need_ctx)
        b_lat, b_ctx = _na(pb, na_qnorm_w[l], na_knorm_w[l], na_rpb[l], batch, seq, ctx_len, need_ctx)
        c_lat = _conv(u, conv_w[l], conv_b[l], conv_ln_w[l], conv_ln_b[l], 0, batch, seq)
        if need_ctx:
            c_ctx_out = _conv(u, conv_w[l], conv_b[l], conv_ln_w[l], conv_ln_b[l], n_lat // ctx_len, batch, ctx_len)
            a_all = jnp.concatenate([a_lat, a_ctx], axis=0)
            b_all = jnp.concatenate([b_lat, b_ctx], axis=0)
            c_all = jnp.concatenate([c_lat, c_ctx_out], axis=0)
            n_rows = n_all
        else:
            a_all, b_all, c_all = a_lat, b_lat, c_lat
            n_rows = n_lat
        xmid, hx, idx, gates, counts = _outproj(a_all, b_all, c_all, xall, g1, sh2, sc2, norm_ffn_w[l], w_out[l],
                                        router_w[l], router_b[l], n_rows, n_lat, seq)
        xall, xs_buf = _moe(hx, idx, counts, gates, xmid, g2, l, ew1, eb1, ew2, eb2, xs_buf, n_rows, n_lat, seq)
    return xall[:n_lat].reshape(batch, seq, d)
```

```python
import functools

import numpy as np
import jax
import jax.numpy as jnp
from jax import lax
from jax.experimental import pallas as pl
from jax.experimental.pallas import tpu as pltpu

F32 = jnp.float32
BF16 = jnp.bfloat16
MXU_DTYPE = BF16

GRID_W = 64
HEAD_DIM = 64
MLSTM_HEADS = 4
NA_HEADS = 8
CONV_CH = 256
MLSTM_W = MLSTM_HEADS * HEAD_DIM
NA_W = NA_HEADS * HEAD_DIM
NA_WIN_R = 8
NA_WIN_C = 16
CONV_WIDTH = 31
ROPE_BASE = 10000.0
N_EXPERTS = 32
TOP_K = 4
SWIGLU_LIMIT = 7.0
SWIGLU_ALPHA = 1.702
EPS = 1e-6

A_Q = 0
A_G = 4 * MLSTM_W
B_Q = A_G + 4 * MLSTM_HEADS
C_A = B_Q + 3 * NA_W
IN_COLS = C_A + 2 * CONV_CH

LANES = 128
SUBLANES = 8
NEG_BIG = -1e30
VMEM_LIMIT = 56 * 1024 * 1024

NA_QROWS = 4
NA_KROWS = NA_QROWS + NA_WIN_R - 1
TOK_TILE = 512
MOE_BLOCK = 512
ROUTE_TILE = 256
COMBINE_CHUNKS = (256, 128, 64, 32, 16, 8)
CONV_ROWS = 64
CONV_PAD = 16


def _mm(a, b):
    return jnp.dot(a.astype(MXU_DTYPE), b.astype(MXU_DTYPE), preferred_element_type=F32)


def _mm_nt(a, b):
    return lax.dot_general(a.astype(MXU_DTYPE), b.astype(MXU_DTYPE), (((1,), (1,)), ((), ())),
                           preferred_element_type=F32)


def _mm_tn(a, b):
    return lax.dot_general(a.astype(MXU_DTYPE), b.astype(MXU_DTYPE), (((0,), (0,)), ((), ())),
                           preferred_element_type=F32)


def _cparams(sem):
    return pltpu.CompilerParams(dimension_semantics=sem, vmem_limit_bytes=VMEM_LIMIT)


def _ada_kernel(c_ref, w_ref, b_ref, o_ref):
    cc = c_ref[...]
    s = cc * jax.nn.sigmoid(cc)
    o_ref[0] = _mm(s, w_ref[0]) + b_ref[0]


def _ada(cc, w_ada, b_ada):
    depth, d, n = w_ada.shape
    rows = cc.shape[0]
    tn = 512
    return pl.pallas_call(
        _ada_kernel,
        grid=(depth, n // tn),
        in_specs=[pl.BlockSpec((rows, d), lambda l, j: (0, 0)),
                  pl.BlockSpec((1, d, tn), lambda l, j: (l, 0, j)),
                  pl.BlockSpec((1, 1, tn), lambda l, j: (l, 0, j))],
        out_specs=pl.BlockSpec((1, rows, tn), lambda l, j: (l, 0, j)),
        out_shape=jax.ShapeDtypeStruct((depth, rows, n), F32),
        compiler_params=_cparams(("parallel", "parallel")),
        name="ada_mod",
    )(cc, w_ada, b_ada.reshape(depth, 1, n))


def _token_specs(parts, tm, n_lat_tiles):
    cols = parts[0].shape[1]
    if len(parts) == 1:
        return [pl.BlockSpec((tm, cols), lambda i: (i, 0))]
    return [pl.BlockSpec((tm, cols), lambda i: (jnp.minimum(i, n_lat_tiles - 1), 0)),
            pl.BlockSpec((tm, cols), lambda i: (jnp.maximum(i - n_lat_tiles, 0), 0))]


def _token_tile(refs, n_lat_tiles):
    if len(refs) == 1:
        return refs[0][...]
    return jnp.where(pl.program_id(0) < n_lat_tiles, refs[0][...], refs[1][...])


def _inproj_kernel(n_x, n_lat_tiles, *refs):
    x_refs = refs[:n_x]
    (sh_ref, sc_ref, nw_ref, wa_ref, wg_ref, wgt_ref, wb_ref, wc_ref,
     pa_ref, g_ref, gt_ref, pb_ref, u_ref) = refs[n_x:]
    x = _token_tile(x_refs, n_lat_tiles)
    ms = jnp.mean(x * x, axis=-1, keepdims=True)
    y = x * lax.rsqrt(ms + EPS) * nw_ref[...]
    h = (y * (1.0 + sc_ref[0]) + sh_ref[0]).astype(MXU_DTYPE)
    pa_ref[...] = _mm(h, wa_ref[...]).astype(pa_ref.dtype)
    g_ref[...] = _mm(h, wg_ref[...])
    gt_ref[...] = _mm_nt(wgt_ref[...], h)
    pb_ref[...] = _mm(h, wb_ref[...]).astype(pb_ref.dtype)
    pc = _mm(h, wc_ref[...])
    u_ref[...] = (pc[:, :CONV_CH] * jax.nn.sigmoid(pc[:, CONV_CH:])).astype(u_ref.dtype)


def _inproj(x_parts, shift, scale, norm_w, w_in, n_lat, seq):
    n = sum(p.shape[0] for p in x_parts)
    d = x_parts[0].shape[1]
    tm = TOK_TILE
    n_lat_tiles = n_lat // tm
    per_batch = seq // tm
    n_mod = shift.shape[0]

    def mod_map(i):
        return (jnp.where(i < n_lat_tiles, i // per_batch, n_mod - 1), 0, 0)

    wa = w_in[:, A_Q:A_G].astype(MXU_DTYPE)
    wg = w_in[:, A_G:B_Q].astype(MXU_DTYPE)
    wb = w_in[:, B_Q:C_A].astype(MXU_DTYPE)
    wc = w_in[:, C_A:IN_COLS].astype(MXU_DTYPE)
    ng = B_Q - A_G
    full = lambda r, c: pl.BlockSpec((r, c), lambda i: (0, 0))
    return pl.pallas_call(
        functools.partial(_inproj_kernel, len(x_parts), n_lat_tiles),
        grid=(n // tm,),
        in_specs=_token_specs(x_parts, tm, n_lat_tiles) + [
                  pl.BlockSpec((1, 1, d), mod_map),
                  pl.BlockSpec((1, 1, d), mod_map),
                  full(1, d),
                  full(d, A_G), full(d, ng), full(ng, d), full(d, 3 * NA_W), full(d, 2 * CONV_CH)],
        out_specs=[pl.BlockSpec((tm, A_G), lambda i: (i, 0)),
                   pl.BlockSpec((tm, ng), lambda i: (i, 0)),
                   pl.BlockSpec((ng, tm), lambda i: (0, i)),
                   pl.BlockSpec((tm, 3 * NA_W), lambda i: (i, 0)),
                   pl.BlockSpec((tm, CONV_CH), lambda i: (i, 0))],
        out_shape=[jax.ShapeDtypeStruct((n, A_G), BF16),
                   jax.ShapeDtypeStruct((n, ng), F32),
                   jax.ShapeDtypeStruct((ng, n), F32),
                   jax.ShapeDtypeStruct((n, 3 * NA_W), BF16),
                   jax.ShapeDtypeStruct((n, CONV_CH), BF16)],
        compiler_params=_cparams(("parallel",)),
        name="in_proj",
    )(*x_parts, shift, scale, norm_w.reshape(1, d), wa, wg, wg.T, wb, wc)


def _split3(x):
    hi = x.astype(BF16)
    r1 = x - hi.astype(F32)
    mid = r1.astype(BF16)
    lo = (r1 - mid.astype(F32)).astype(BF16)
    return hi, mid, lo


def _tri_left(tri, x):
    return sum(jnp.dot(tri, p, preferred_element_type=F32) for p in _split3(x))


def _tri_right(x, tri):
    return sum(jnp.dot(p, tri, preferred_element_type=F32) for p in _split3(x))


def _log_sigmoid(x):
    return jnp.minimum(x, 0.0) - jnp.log(1.0 + jnp.exp(-jnp.abs(x)))


def _mlstm_direction(z, q, k, v, gcol, grow, c_st, m_st, h_ref, row0, lc, with_output):
    nh = MLSTM_HEADS
    ti = lax.broadcasted_iota(jnp.int32, (lc, lc), 0)
    si = lax.broadcasted_iota(jnp.int32, (lc, lc), 1)
    lower = si <= ti
    upper = si >= ti
    tl = jnp.where(lower, 1.0, 0.0).astype(BF16)
    tu = jnp.where(upper, 1.0, 0.0).astype(BF16)
    i_col = gcol[:, z * nh:(z + 1) * nh]
    f_col = _log_sigmoid(gcol[:, 2 * nh + z * nh:2 * nh + (z + 1) * nh])
    i_row = grow[z * nh:(z + 1) * nh, :]
    f_row = _log_sigmoid(grow[2 * nh + z * nh:2 * nh + (z + 1) * nh, :])
    if z == 0:
        b_col = _tri_left(tl, f_col)
        b_row = _tri_right(f_row, tu)
        b_tot = b_col[lc - 1:lc, :]
        mask = lower
    else:
        b_col = _tri_left(tu, f_col)
        b_row = _tri_right(f_row, tl)
        b_tot = b_col[0:1, :]
        mask = upper
    lane256 = lax.broadcasted_iota(jnp.int32, (1, 4 * HEAD_DIM), 1)
    lane128 = lax.broadcasted_iota(jnp.int32, (1, LANES), 1)
    k_mx = k.astype(MXU_DTYPE)
    c_all = c_st[z].astype(MXU_DTYPE)
    for h in range(nh):
        r = z * nh + h
        bc = b_col[:, h:h + 1]
        br = b_row[h:h + 1, :]
        ic = i_col[:, h:h + 1]
        ir = i_row[h:h + 1, :]
        bl = b_tot[:, h:h + 1]
        m_old = m_st[r:r + 1, 0:1]
        vh = v[:, (h // 2) * LANES:(h // 2 + 1) * LANES]
        if h % 2 == 1:
            vh = pltpu.roll(vh, HEAD_DIM, 1)
        vext = jnp.where(lane128 < HEAD_DIM, vh, jnp.where(lane128 == HEAD_DIM, 1.0, 0.0)).astype(MXU_DTYPE)
        if with_output:
            qh = jnp.where(lane256 // HEAD_DIM == h, q, 0.0).astype(MXU_DTYPE)
            log_d = jnp.where(mask, bc + (ir - br), NEG_BIG)
            inter = bc + m_old
            m_t = jnp.maximum(inter, jnp.max(log_d, axis=1, keepdims=True))
            s = _mm_nt(qh, k_mx) * jnp.exp(log_d - m_t)
            w_inter = jnp.exp(inter - m_t)
            nd = _mm(s, vext) + w_inter * _mm(qh, c_all)
            den = nd[:, HEAD_DIM:HEAD_DIM + 1]
            hval = nd / jnp.maximum(jnp.abs(den), jnp.exp(-m_t))
            h_ref[pl.ds(row0, lc), h * LANES:(h + 1) * LANES] = hval
        log_w = bl - bc + ic
        m_new = jnp.maximum(bl + m_old, jnp.max(log_w, axis=0, keepdims=True))
        w = jnp.exp(log_w - m_new)
        decay = jnp.exp(bl + m_old - m_new)
        upd = _mm_tn(k * w, vext)
        rows = slice(h * HEAD_DIM, (h + 1) * HEAD_DIM)
        c_st[z, rows, :] = decay * c_st[z, rows, :] + upd[rows, :]
        m_st[r:r + 1, :] = jnp.broadcast_to(m_new, (1, LANES))


def _mlstm_kernel(lc, nc, need_ctx,
                  pa_c, g_c, gt_c,
                  pa_f, g_f, gt_f, cos_f, sin_f,
                  pa_b, g_b, gt_b, cos_b, sin_b,
                  o_lat, o_ctx, brow_ref, bcol_ref, nw_ref, perm_ref,
                  *rest):
    if need_ctx:
        out_lat, out_ctx, hf, hb, c_st, m_st = rest
    else:
        out_lat, hf, hb, c_st, m_st = rest
        out_ctx = None
    s = pl.program_id(1)
    w = MLSTM_W
    k_scale = HEAD_DIM ** -0.5

    def load(pa, cos_ref, sin_ref):
        q = pa[:, 0:w].astype(F32)
        k = pa[:, w:2 * w].astype(F32)
        v = pa[:, 2 * w:3 * w].astype(F32)
        if cos_ref is not None:
            cs = cos_ref[...]
            sn = sin_ref[...]
            q = q * cs + _mm(q, perm_ref[...]) * sn
            k = k * cs + _mm(k, perm_ref[...]) * sn
        return q, k * k_scale, v

    @pl.when(s == 0)
    def _():
        c_st[...] = jnp.zeros(c_st.shape, F32)
        m_st[...] = jnp.zeros(m_st.shape, F32)
        q, k, v = load(pa_c, None, None)
        gcol = g_c[...] + brow_ref[...]
        grow = gt_c[...] + bcol_ref[...]
        for z, h_ref in ((0, hf), (1, hb)):
            _mlstm_direction(z, q, k, v, gcol, grow, c_st, m_st, h_ref, 0, lc, need_ctx)

    @pl.when(s > 0)
    def _():
        for z, h_ref, refs in ((0, hf, (pa_f, g_f, gt_f, cos_f, sin_f)),
                               (1, hb, (pa_b, g_b, gt_b, cos_b, sin_b))):
            pa, g, gt, cos_ref, sin_ref = refs
            j = s - 1 if z == 0 else nc - s
            row0 = pl.multiple_of(lc + j * lc, lc)
            q, k, v = load(pa, cos_ref, sin_ref)
            gcol = g[...] + brow_ref[...]
            grow = gt[...] + bcol_ref[...]
            _mlstm_direction(z, q, k, v, gcol, grow, c_st, m_st, h_ref, row0, lc, True)

    @pl.when(s == nc)
    def _():
        lane128 = lax.broadcasted_iota(jnp.int32, (1, LANES), 1)
        first = 0 if need_ctx else 1
        for ch in range(first, nc + 1):
            rows = slice(ch * lc, (ch + 1) * lc)
            if ch == 0:
                o_val, dst, dst_rows = o_ctx[...], out_ctx, slice(0, lc)
            else:
                dst_rows = slice((ch - 1) * lc, ch * lc)
                o_val, dst = o_lat[dst_rows, :], out_lat
            o_val = o_val.astype(F32)
            for p in range(MLSTM_HEADS // 2):
                pair = []
                for h in (2 * p, 2 * p + 1):
                    hv = hf[rows, h * LANES:(h + 1) * LANES] + hb[rows, h * LANES:(h + 1) * LANES]
                    hv = jnp.where(lane128 < HEAD_DIM, hv, 0.0)
                    ms = jnp.sum(hv * hv, axis=1, keepdims=True) * (1.0 / HEAD_DIM)
                    pair.append(hv * lax.rsqrt(ms + EPS))
                packed = jnp.where(lane128 < HEAD_DIM, pair[0], pltpu.roll(pair[1], HEAD_DIM, 1))
                cols = slice(p * LANES, (p + 1) * LANES)
                res = packed * nw_ref[:, cols] * jax.nn.sigmoid(o_val[:, cols])
                dst[dst_rows, cols] = res.astype(dst.dtype)


def _rope_tables(seq):
    half = HEAD_DIM // 2
    quarter = half // 2
    t = jnp.arange(seq, dtype=jnp.int32)
    inv_freq = ROPE_BASE ** (-jnp.arange(quarter, dtype=F32) / quarter)
    parts_c, parts_s = [], []
    for pos in (t // GRID_W, t % GRID_W):
        ang = pos.astype(F32)[:, None] * inv_freq[None, :]
        parts_c += [jnp.cos(ang), jnp.cos(ang)]
        parts_s += [-jnp.sin(ang), jnp.sin(ang)]
    cos = jnp.tile(jnp.concatenate(parts_c, axis=-1), (1, MLSTM_HEADS))
    sin = jnp.tile(jnp.concatenate(parts_s, axis=-1), (1, MLSTM_HEADS))
    j = np.arange(MLSTM_W)
    partner = np.where(j % half < quarter, j + quarter, j - quarter)
    perm = np.zeros((MLSTM_W, MLSTM_W), np.float32)
    perm[partner, j] = 1.0
    return cos, sin, jnp.asarray(perm, dtype=MXU_DTYPE)


def _mlstm(pa, g, gt, ig_b, fg_b, norm_w, rope, batch, seq, ctx_len, need_ctx):
    lc = ctx_len
    nc = seq // lc
    n_lat = batch * seq
    cos, sin, perm = rope
    bias = jnp.concatenate([ig_b.reshape(-1), fg_b.reshape(-1)]).astype(F32)
    ng = bias.shape[0]
    lat_blocks = n_lat // lc

    def fwd(b, s):
        return b * nc + jnp.maximum(s - 1, 0)

    def bwd(b, s):
        return b * nc + nc - jnp.maximum(s, 1)

    def fwd_c(b, s):
        return jnp.maximum(s - 1, 0)

    def bwd_c(b, s):
        return nc - jnp.maximum(s, 1)

    def lat_specs(chunk, chunk_c):
        return [pl.BlockSpec((lc, A_G), lambda b, s: (chunk(b, s), 0)),
                pl.BlockSpec((lc, ng), lambda b, s: (chunk(b, s), 0)),
                pl.BlockSpec((ng, lc), lambda b, s: (0, chunk(b, s))),
                pl.BlockSpec((lc, MLSTM_W), lambda b, s: (chunk_c(b, s), 0)),
                pl.BlockSpec((lc, MLSTM_W), lambda b, s: (chunk_c(b, s), 0))]

    in_specs = ([pl.BlockSpec((lc, A_G), lambda b, s: (lat_blocks + b, 0)),
                 pl.BlockSpec((lc, ng), lambda b, s: (lat_blocks + b, 0)),
                 pl.BlockSpec((ng, lc), lambda b, s: (0, lat_blocks + b))]
                + lat_specs(fwd, fwd_c) + lat_specs(bwd, bwd_c)
                + [pl.BlockSpec((seq, MLSTM_W), lambda b, s: (b, 3)),
                   pl.BlockSpec((lc, MLSTM_W), lambda b, s: (lat_blocks + b, 3)),
                   pl.BlockSpec((1, ng), lambda b, s: (0, 0)),
                   pl.BlockSpec((ng, 1), lambda b, s: (0, 0)),
                   pl.BlockSpec((1, MLSTM_W), lambda b, s: (0, 0)),
                   pl.BlockSpec((MLSTM_W, MLSTM_W), lambda b, s: (0, 0))])
    out_specs = [pl.BlockSpec((seq, MLSTM_W), lambda b, s: (b, 0))]
    out_shape = [jax.ShapeDtypeStruct((n_lat, MLSTM_W), BF16)]
    if need_ctx:
        out_specs.append(pl.BlockSpec((lc, MLSTM_W), lambda b, s: (b, 0)))
        out_shape.append(jax.ShapeDtypeStruct((batch * ctx_len, MLSTM_W), BF16))
    t_all = ctx_len + seq
    outs = pl.pallas_call(
        functools.partial(_mlstm_kernel, lc, nc, need_ctx),
        grid=(batch, nc + 1),
        in_specs=in_specs,
        out_specs=out_specs,
        out_shape=out_shape,
        scratch_shapes=[pltpu.VMEM((t_all, MLSTM_HEADS * LANES), F32),
                        pltpu.VMEM((t_all, MLSTM_HEADS * LANES), F32),
                        pltpu.VMEM((2, MLSTM_W, LANES), F32),
                        pltpu.VMEM((2 * MLSTM_HEADS, LANES), F32)],
        compiler_params=_cparams(("parallel", "arbitrary")),
        name="mlstm",
    )(pa, g, gt, pa, g, gt, cos, sin, pa, g, gt, cos, sin, pa, pa,
      bias.reshape(1, ng), bias.reshape(ng, 1), norm_w.reshape(1, MLSTM_W).astype(F32), perm)
    return (outs[0], outs[1]) if need_ctx else (outs[0], None)


def _na_patterns(n_rows):
    kr = min(NA_WIN_R, n_rows)
    n_dr = 2 * NA_WIN_R - 1
    pats, pat_ids, bases = [], [], []
    for gi in range(n_rows // NA_QROWS):
        base = int(np.clip(NA_QROWS * gi - NA_WIN_R // 2, 0, n_rows - NA_KROWS))
        dr = np.full((NA_QROWS, NA_KROWS), n_dr, np.int32)
        for qr in range(NA_QROWS):
            r = NA_QROWS * gi + qr
            r0 = int(np.clip(r - kr // 2, 0, n_rows - kr))
            for kj in range(NA_KROWS):
                if r0 <= base + kj < r0 + kr:
                    dr[qr, kj] = base + kj - r + NA_WIN_R - 1
        for pi, p in enumerate(pats):
            if np.array_equal(p, dr):
                pat_ids.append(pi)
                break
        else:
            pat_ids.append(len(pats))
            pats.append(dr)
        bases.append(base)
    return tuple(pat_ids), tuple(bases), np.stack(pats)


def _na_bias_table(rpb, row_idx):
    heads = rpb.shape[0]
    col = np.arange(GRID_W)
    col_start = np.clip(col - NA_WIN_C // 2, 0, GRID_W - NA_WIN_C)
    in_win = (col[None, :] >= col_start[:, None]) & (col[None, :] < col_start[:, None] + NA_WIN_C)
    dc = np.clip(col[None, :] - col[:, None] + NA_WIN_C - 1, 0, 2 * NA_WIN_C - 2)
    onehot = (dc[None] == np.arange(2 * NA_WIN_C - 1)[:, None, None]).astype(np.float32)
    planes = jnp.einsum('hdc,cqk->hdqk', rpb, onehot, precision=lax.Precision.HIGHEST)
    planes = jnp.where(in_win[None, None], planes, NEG_BIG)
    planes = jnp.concatenate([planes, jnp.full((heads, 1, GRID_W, GRID_W), NEG_BIG, F32)], axis=1)
    npat = row_idx.shape[0]
    tab = planes[:, row_idx.reshape(-1)].reshape(heads, npat, NA_QROWS, NA_KROWS, GRID_W, GRID_W)
    return tab.transpose(0, 1, 2, 4, 3, 5).reshape(heads, npat, NA_QROWS * GRID_W, NA_KROWS * GRID_W)


def _na_kernel(pat_ids, bases, need_ctx, q_ref, k_ref, v_ref, kc_ref, vc_ref, *rest):
    if need_ctx:
        qc_ref, bias_ref, qw_ref, kw_ref, out_ref, outc_ref, kn_s, kcn_s = rest
    else:
        bias_ref, qw_ref, kw_ref, out_ref, kn_s, kcn_s = rest
    lane = lax.broadcasted_iota(jnp.int32, (1, LANES), 1)
    low = lane < HEAD_DIM
    inv_d = 1.0 / HEAD_DIM

    def rmsn(x, w):
        x2 = x * x
        s0 = jnp.sum(jnp.where(low, x2, 0.0), axis=-1, keepdims=True)
        s1 = jnp.sum(jnp.where(low, 0.0, x2), axis=-1, keepdims=True)
        r = jnp.where(low, lax.rsqrt(s0 * inv_d + EPS), lax.rsqrt(s1 * inv_d + EPS))
        return x * r * w

    qw = qw_ref[...]
    kn_s[...] = rmsn(k_ref[...].astype(F32), kw_ref[...]).astype(kn_s.dtype)
    kcn_s[...] = rmsn(kc_ref[...].astype(F32), kw_ref[...]).astype(kcn_s.dtype)
    kcn = kcn_s[...]
    vc = vc_ref[...]
    scale = HEAD_DIM ** -0.5
    nq = NA_QROWS * GRID_W
    nk = NA_KROWS * GRID_W

    def attend(qn, parts):
        outs = []
        for hh in range(2):
            qh = jnp.where(low if hh == 0 else jnp.logical_not(low), qn, 0.0).astype(MXU_DTYPE)
            scores = []
            for keys, _, bias in parts:
                sc = _mm_nt(qh, keys)
                if bias is not None:
                    sc = sc + bias[hh]
                scores.append(sc)
            m = scores[0].max(axis=-1, keepdims=True)
            for sc in scores[1:]:
                m = jnp.maximum(m, sc.max(axis=-1, keepdims=True))
            acc = None
            den = None
            for sc, (_, vals, _) in zip(scores, parts):
                p = jnp.exp(sc - m)
                d = jnp.sum(p, axis=-1, keepdims=True)
                o = _mm(p, vals)
                acc = o if acc is None else acc + o
                den = d if den is None else den + d
            outs.append(acc / den)
        return jnp.where(low, outs[0], outs[1])

    for gi, (pid, base) in enumerate(zip(pat_ids, bases)):
        qn = rmsn(q_ref[gi * nq:(gi + 1) * nq, :].astype(F32), qw) * scale
        kwin = kn_s[base * GRID_W:base * GRID_W + nk, :]
        vwin = v_ref[base * GRID_W:base * GRID_W + nk, :]
        bias = (bias_ref[0, pid], bias_ref[1, pid])
        res = attend(qn, [(kwin, vwin, bias), (kcn, vc, None)])
        out_ref[gi * nq:(gi + 1) * nq, :] = res.astype(out_ref.dtype)

    if need_ctx:
        qn = rmsn(qc_ref[...].astype(F32), qw) * scale
        outc_ref[...] = attend(qn, [(kcn, vc, None)]).astype(outc_ref.dtype)


def _na(pb, qn_w, kn_w, rpb, batch, seq, ctx_len, need_ctx):
    n_rows = seq // GRID_W
    pat_ids, bases, row_idx = _na_patterns(n_rows)
    npat = row_idx.shape[0]
    nq, nk = NA_QROWS * GRID_W, NA_KROWS * GRID_W
    bias = _na_bias_table(rpb.astype(F32), row_idx)
    n_lat = batch * seq
    pairs = NA_HEADS // 2
    qoff, koff, voff = 0, pairs, 2 * pairs

    in_specs = [pl.BlockSpec((seq, LANES), lambda p, b: (b, qoff + p)),
                pl.BlockSpec((seq, LANES), lambda p, b: (b, koff + p)),
                pl.BlockSpec((seq, LANES), lambda p, b: (b, voff + p)),
                pl.BlockSpec((ctx_len, LANES), lambda p, b: (n_lat // ctx_len + b, koff + p)),
                pl.BlockSpec((ctx_len, LANES), lambda p, b: (n_lat // ctx_len + b, voff + p))]
    args = [pb, pb, pb, pb, pb]
    if need_ctx:
        in_specs.append(pl.BlockSpec((ctx_len, LANES), lambda p, b: (n_lat // ctx_len + b, qoff + p)))
        args.append(pb)
    in_specs += [pl.BlockSpec((2, npat, nq, nk), lambda p, b: (p, 0, 0, 0)),
                 pl.BlockSpec((1, LANES), lambda p, b: (0, 0)),
                 pl.BlockSpec((1, LANES), lambda p, b: (0, 0))]
    args += [bias, jnp.tile(qn_w.astype(F32), 2).reshape(1, LANES), jnp.tile(kn_w.astype(F32), 2).reshape(1, LANES)]
    out_specs = [pl.BlockSpec((seq, LANES), lambda p, b: (b, p))]
    out_shape = [jax.ShapeDtypeStruct((n_lat, NA_W), BF16)]
    if need_ctx:
        out_specs.append(pl.BlockSpec((ctx_len, LANES), lambda p, b: (b, p)))
        out_shape.append(jax.ShapeDtypeStruct((batch * ctx_len, NA_W), BF16))
    outs = pl.pallas_call(
        functools.partial(_na_kernel, pat_ids, bases, need_ctx),
        grid=(pairs, batch),
        in_specs=in_specs,
        out_specs=out_specs,
        out_shape=out_shape,
        scratch_shapes=[pltpu.VMEM((seq, LANES), MXU_DTYPE), pltpu.VMEM((ctx_len, LANES), MXU_DTYPE)],
        compiler_params=_cparams(("parallel", "parallel")),
        name="na_attn",
    )(*args)
    return (outs[0], outs[1]) if need_ctx else (outs[0], None)


def _conv_kernel(t_len, u_ref, w_ref, cb_ref, lw_ref, lb_ref, o_ref, pad_s):
    zeros = jnp.zeros((CONV_PAD, CONV_CH), F32)
    pad_s[0:CONV_PAD, :] = zeros
    pad_s[CONV_PAD + t_len:2 * CONV_PAD + t_len, :] = zeros
    pad_s[CONV_PAD:CONV_PAD + t_len, :] = u_ref[...].astype(F32)
    shift = CONV_PAD - CONV_WIDTH // 2

    def body(c, carry):
        r0 = pl.multiple_of(c * CONV_ROWS, CONV_ROWS)
        n_win = CONV_ROWS + 2 * CONV_PAD
        win = pad_s[pl.ds(r0, n_win), :]
        rot = [win] + [pltpu.roll(win, n_win - ph, 0) for ph in range(1, SUBLANES)]
        acc = jnp.zeros((CONV_ROWS, CONV_CH), F32) + cb_ref[...]
        for j in range(CONV_WIDTH):
            ph, al = (j + shift) % SUBLANES, (j + shift) // SUBLANES * SUBLANES
            acc = acc + rot[ph][al:al + CONV_ROWS, :] * w_ref[j:j + 1, :]
        mean = jnp.mean(acc, axis=-1, keepdims=True)
        xc = acc - mean
        var = jnp.mean(xc * xc, axis=-1, keepdims=True)
        y = xc * lax.rsqrt(var + EPS) * lw_ref[...] + lb_ref[...]
        o_ref[pl.ds(r0, CONV_ROWS), :] = (y * jax.nn.sigmoid(y)).astype(o_ref.dtype)
        return carry

    lax.fori_loop(0, t_len // CONV_ROWS, body, 0)


def _conv(u, conv_w, conv_b, ln_w, ln_b, first_block, n_seq, t_len):
    row = lambda a: a.reshape(1, CONV_CH).astype(F32)
    return pl.pallas_call(
        functools.partial(_conv_kernel, t_len),
        grid=(n_seq,),
        in_specs=[pl.BlockSpec((t_len, CONV_CH), lambda b: (first_block + b, 0)),
                  pl.BlockSpec((CONV_WIDTH, CONV_CH), lambda b: (0, 0)),
                  pl.BlockSpec((1, CONV_CH), lambda b: (0, 0)),
                  pl.BlockSpec((1, CONV_CH), lambda b: (0, 0)),
                  pl.BlockSpec((1, CONV_CH), lambda b: (0, 0))],
        out_specs=pl.BlockSpec((t_len, CONV_CH), lambda b: (b, 0)),
        out_shape=jax.ShapeDtypeStruct((n_seq * t_len, CONV_CH), BF16),
        scratch_shapes=[pltpu.VMEM((t_len + 2 * CONV_PAD, CONV_CH), F32)],
        compiler_params=_cparams(("parallel",)),
        name="conv_module",
    )(u, conv_w.astype(F32), row(conv_b), row(ln_w), row(ln_b))


def _outproj_kernel(rt, n_src, n_lat_tiles, *refs):
    a_refs, b_refs, c_refs, x_refs = (refs[j * n_src:(j + 1) * n_src] for j in range(4))
    (g1_ref, sh_ref, sc_ref, nw_ref, wa_ref, wb_ref, wc_ref, rwh_ref, rwl_ref, rb_ref,
     xo_ref, hx_ref, idx_ref, gate_ref, cnt_ref) = refs[4 * n_src:]
    tile = lambda parts: _token_tile(parts, n_lat_tiles)
    mix = _mm(tile(a_refs), wa_ref[...]) + _mm(tile(b_refs), wb_ref[...]) + _mm(tile(c_refs), wc_ref[...])
    xn = tile(x_refs) + g1_ref[0] * mix
    xo_ref[...] = xn
    ms = jnp.mean(xn * xn, axis=-1, keepdims=True)
    hx = xn * lax.rsqrt(ms + EPS) * nw_ref[...] * (1.0 + sc_ref[0]) + sh_ref[0]
    hx_ref[...] = hx
    h_hi = hx.astype(BF16)
    h_lo = (hx - h_hi.astype(F32)).astype(BF16)
    logits = (jnp.dot(h_hi, rwh_ref[...], preferred_element_type=F32)
              + jnp.dot(h_lo, rwh_ref[...], preferred_element_type=F32)
              + jnp.dot(h_hi, rwl_ref[...], preferred_element_type=F32)) + rb_ref[...]
    lane = lax.broadcasted_iota(jnp.int32, logits.shape, 1)
    idx_out = jnp.zeros(logits.shape, jnp.int32)
    val_out = jnp.zeros(logits.shape, F32)
    top = None
    den = None
    sels = []
    for kk in range(TOP_K):
        m = jnp.max(logits, axis=-1, keepdims=True)
        sel = jnp.min(jnp.where(logits == m, lane, LANES), axis=-1, keepdims=True)
        if kk == 0:
            top = m
        e = jnp.exp(m - top)
        den = e if den is None else den + e
        idx_out = jnp.where(lane == kk, sel, idx_out)
        val_out = jnp.where(lane == kk, e, val_out)
        logits = jnp.where(lane == sel, -jnp.inf, logits)
        sels.append(sel)
    gate_ref[...] = val_out / den
    tm = logits.shape[0]
    chosen = jnp.where(logits == -jnp.inf, 1.0, 0.0)
    ti = lax.broadcasted_iota(jnp.int32, (tm, tm), 0)
    si = lax.broadcasted_iota(jnp.int32, (tm, tm), 1)
    earlier = jnp.where(jnp.logical_and(si < ti, si // rt == ti // rt), 1.0, 0.0).astype(BF16)
    before = jnp.dot(earlier, chosen.astype(BF16), preferred_element_type=F32)
    for kk in range(TOP_K):
        rank = jnp.sum(jnp.where(lane == sels[kk], before, 0.0), axis=-1, keepdims=True)
        idx_out = jnp.where(lane == TOP_K + kk, rank.astype(jnp.int32), idx_out)
    idx_ref[...] = idx_out
    row = lax.broadcasted_iota(jnp.int32, cnt_ref.shape[1:], 0)
    cnt = jnp.zeros(cnt_ref.shape[1:], F32)
    for sub in range(tm // rt):
        cnt = jnp.where(row == sub, jnp.sum(chosen[sub * rt:(sub + 1) * rt], axis=0, keepdims=True), cnt)
    cnt_ref[0] = cnt.astype(jnp.int32)


def _outproj(a, b, c, x_parts, g1, sh2, sc2, norm_w, w_out, router_w, router_b, n_rows, n_lat, seq):
    d = x_parts[0].shape[1]
    n_src = len(x_parts)
    assert len(a) == len(b) == len(c) == n_src
    tm = TOK_TILE
    n_lat_tiles = n_lat // tm
    per_batch = seq // tm
    n_mod = g1.shape[0]

    def mod_map(i):
        return (jnp.where(i < n_lat_tiles, i // per_batch, n_mod - 1), 0, 0)

    wa = w_out[0:MLSTM_W].astype(MXU_DTYPE)
    wb = w_out[MLSTM_W:MLSTM_W + NA_W].astype(MXU_DTYPE)
    wc = w_out[MLSTM_W + NA_W:].astype(MXU_DTYPE)
    rw = jnp.zeros((d, LANES), F32).at[:, :N_EXPERTS].set(router_w.astype(F32))
    rw_hi = rw.astype(BF16)
    rw_lo = (rw - rw_hi.astype(F32)).astype(BF16)
    rb = jnp.full((1, LANES), NEG_BIG, F32).at[0, :N_EXPERTS].set(router_b.astype(F32))
    full = lambda r, cc: pl.BlockSpec((r, cc), lambda i: (0, 0))
    tile = lambda cc: pl.BlockSpec((tm, cc), lambda i: (i, 0))
    return pl.pallas_call(
        functools.partial(_outproj_kernel, ROUTE_TILE, n_src, n_lat_tiles),
        grid=(n_rows // tm,),
        in_specs=[spec for parts in (a, b, c, x_parts) for spec in _token_specs(parts, tm, n_lat_tiles)] + [
                  pl.BlockSpec((1, 1, d), mod_map), pl.BlockSpec((1, 1, d), mod_map),
                  pl.BlockSpec((1, 1, d), mod_map), full(1, d),
                  full(MLSTM_W, d), full(NA_W, d), full(CONV_CH, d), full(d, LANES), full(d, LANES),
                  full(1, LANES)],
        out_specs=[tile(d), tile(d), tile(LANES), tile(LANES),
                   pl.BlockSpec((1, SUBLANES, LANES), lambda i: (i, 0, 0))],
        out_shape=[jax.ShapeDtypeStruct((n_rows, d), F32), jax.ShapeDtypeStruct((n_rows, d), F32),
                   jax.ShapeDtypeStruct((n_rows, LANES), jnp.int32), jax.ShapeDtypeStruct((n_rows, LANES), F32),
                   jax.ShapeDtypeStruct((n_rows // tm, SUBLANES, LANES), jnp.int32)],
        compiler_params=_cparams(("parallel",)),
        name="out_proj",
    )(*a, *b, *c, *x_parts, g1, sh2, sc2, norm_w.reshape(1, d).astype(F32), wa, wb, wc, rw_hi, rw_lo, rb)


def _dispatch_kernel(tm, dest_ref, hx_ref, xs_in, xs_out, sem):
    del xs_in

    def body(r, carry):
        for kk in range(TOP_K):
            dst = dest_ref[0, 0, kk * tm + r]
            pltpu.make_async_copy(hx_ref.at[pl.ds(r, 1), :], xs_out.at[pl.ds(dst, 1), :], sem).start()
        return carry

    lax.fori_loop(0, tm, body, 0)
    for kk in range(TOP_K):
        pltpu.make_async_copy(hx_ref, xs_out.at[pl.ds(0, tm), :], sem).wait()


def _dispatch(hx, dest_tiles, xs_init, tm):
    n, d = hx.shape
    return pl.pallas_call(
        functools.partial(_dispatch_kernel, tm),
        grid=(n // tm,),
        in_specs=[pl.BlockSpec((1, 1, TOP_K * tm), lambda i: (i, 0, 0), memory_space=pltpu.SMEM),
                  pl.BlockSpec((tm, d), lambda i: (i, 0)),
                  pl.BlockSpec(memory_space=pl.ANY)],
        out_specs=pl.BlockSpec(memory_space=pl.ANY),
        out_shape=jax.ShapeDtypeStruct(xs_init.shape, xs_init.dtype),
        scratch_shapes=[pltpu.SemaphoreType.DMA],
        input_output_aliases={2: 0},
        compiler_params=_cparams(("arbitrary",)),
        name="moe_dispatch",
    )(dest_tiles, hx, xs_init)


def _expert_kernel(be_ref, nu_ref, x_ref, w1_ref, b1_ref, w2_ref, b2_ref, o_ref, w1_s, w2_s):
    i = pl.program_id(0)
    de = w2_ref.shape[1]

    @pl.when(jnp.logical_or(i == 0, be_ref[i] != be_ref[jnp.maximum(i - 1, 0)]))
    def _():
        w1_s[...] = w1_ref[0].astype(w1_s.dtype)
        w2_s[...] = w2_ref[0].astype(w2_s.dtype)

    @pl.when(i < nu_ref[0])
    def _():
        h = _mm(x_ref[...], w1_s[...]) + b1_ref[0]
        glu = jnp.minimum(h[:, :de], SWIGLU_LIMIT)
        lin = jnp.clip(h[:, de:], -SWIGLU_LIMIT, SWIGLU_LIMIT)
        act = (lin + 1.0) * glu * jax.nn.sigmoid(SWIGLU_ALPHA * glu)
        o_ref[...] = _mm(act, w2_s[...]) + b2_ref[0]

    @pl.when(i >= nu_ref[0])
    def _():
        o_ref[...] = jnp.zeros(o_ref.shape, o_ref.dtype)


def _experts(xs, n_blocks, block_e, n_used, w1, b1, w2, b2):
    d = xs.shape[1]
    ne, _, two_de = w1.shape
    de = w2.shape[1]
    bm = MOE_BLOCK
    grid_spec = pltpu.PrefetchScalarGridSpec(
        num_scalar_prefetch=2,
        grid=(n_blocks,),
        in_specs=[pl.BlockSpec((bm, d), lambda i, be, nu: (i, 0)),
                  pl.BlockSpec((1, d, two_de), lambda i, be, nu: (be[i], 0, 0)),
                  pl.BlockSpec((1, 1, two_de), lambda i, be, nu: (be[i], 0, 0)),
                  pl.BlockSpec((1, de, d), lambda i, be, nu: (be[i], 0, 0)),
                  pl.BlockSpec((1, 1, d), lambda i, be, nu: (be[i], 0, 0))],
        out_specs=pl.BlockSpec((bm, d), lambda i, be, nu: (i, 0)),
        scratch_shapes=[pltpu.VMEM((d, two_de), MXU_DTYPE), pltpu.VMEM((de, d), MXU_DTYPE)],
    )
    return pl.pallas_call(
        _expert_kernel,
        grid_spec=grid_spec,
        out_shape=jax.ShapeDtypeStruct((n_blocks * bm, d), F32),
        compiler_params=_cparams(("arbitrary",)),
        name="moe_experts",
    )(block_e, n_used, xs, w1.astype(F32), b1.reshape(ne, 1, two_de).astype(F32), w2.astype(F32),
      b2.reshape(ne, 1, d).astype(F32))


def _combine_kernel(rt, src_ref, size_ref, soff_ref, eo_hbm, x_ref, pos_ref, gate_ref, g2_ref, xo_ref, stage, sem):
    i = pl.program_id(0)
    slot = i % 2
    sr = stage.shape[1]

    def run_chunks(tile, to_slot):
        for e in range(N_EXPERTS):
            base = tile * N_EXPERTS + e
            src, size, dst = src_ref[base], size_ref[base], soff_ref[base]
            for chunk in COMBINE_CHUNKS:
                done = size & ~(2 * chunk - 1)
                copy = pltpu.make_async_copy(
                    eo_hbm.at[pl.ds(pl.multiple_of(src + done, SUBLANES), chunk), :],
                    stage.at[to_slot, pl.ds(pl.multiple_of(dst + done, SUBLANES), chunk), :], sem.at[to_slot])
                yield (size & chunk) != 0, copy

    @pl.when(i == 0)
    def _():
        stage[...] = jnp.zeros(stage.shape, stage.dtype)
        for needed, copy in run_chunks(0, 0):
            pl.when(needed)(copy.start)

    @pl.when(i + 1 < pl.num_programs(0))
    def _():
        for needed, copy in run_chunks(i + 1, 1 - slot):
            pl.when(needed)(copy.start)

    for needed, copy in run_chunks(i, slot):
        pl.when(needed)(copy.wait)

    last = i * N_EXPERTS + N_EXPERTS - 1
    n_staged = soff_ref[last] + size_ref[last]
    row = lax.broadcasted_iota(jnp.int32, (sr, 1), 0)
    staged = jnp.where(row < n_staged, stage[slot], 0.0).astype(MXU_DTYPE)
    col = lax.broadcasted_iota(jnp.int32, (rt, sr), 1)
    pos = pos_ref[...]
    gates = gate_ref[...]
    sel = jnp.zeros((rt, sr), F32)
    for kk in range(TOP_K):
        sel = jnp.where(col == pos[:, kk:kk + 1], gates[:, kk:kk + 1], sel)
    sel_hi = sel.astype(BF16)
    sel_lo = (sel - sel_hi.astype(F32)).astype(BF16)
    y = (jnp.dot(sel_hi.astype(MXU_DTYPE), staged, preferred_element_type=F32)
         + jnp.dot(sel_lo.astype(MXU_DTYPE), staged, preferred_element_type=F32))
    xo_ref[...] = x_ref[...] + g2_ref[0] * y


def _combine(eo, tables, pos, xres, gates, g2, n_rows, n_lat, seq):
    d = xres.shape[1]
    rt = ROUTE_TILE
    n_lat_tiles = n_lat // rt
    per_batch = seq // rt
    n_mod = g2.shape[0]
    sr = TOP_K * rt + N_EXPERTS * 2 * (SUBLANES - 1)
    sr = -(-sr // 256) * 256

    def mod_map(i, *_):
        return (jnp.where(i < n_lat_tiles, i // per_batch, n_mod - 1), 0, 0)

    grid_spec = pltpu.PrefetchScalarGridSpec(
        num_scalar_prefetch=3,
        grid=(n_rows // rt,),
        in_specs=[pl.BlockSpec(memory_space=pl.ANY),
                  pl.BlockSpec((rt, d), lambda i, *_: (i, 0)),
                  pl.BlockSpec((rt, TOP_K), lambda i, *_: (i, 0)),
                  pl.BlockSpec((rt, LANES), lambda i, *_: (i, 0)),
                  pl.BlockSpec((1, 1, d), mod_map)],
        out_specs=pl.BlockSpec((rt, d), lambda i, *_: (i, 0)),
        scratch_shapes=[pltpu.VMEM((2, sr, d), F32), pltpu.SemaphoreType.DMA((2,))],
    )
    return pl.pallas_call(
        functools.partial(_combine_kernel, rt),
        grid_spec=grid_spec,
        out_shape=jax.ShapeDtypeStruct((n_rows, d), F32),
        compiler_params=_cparams(("arbitrary",)),
        name="moe_combine",
    )(*tables, eo, xres, pos, gates, g2)


def _dest_tiles(dest, tm):
    n = dest.shape[0]
    return dest.reshape(n // tm, tm, TOP_K).transpose(0, 2, 1).reshape(n // tm, 1, TOP_K * tm)


def _route(idx, counts, n_blocks):
    bm = MOE_BLOCK
    rt = ROUTE_TILE
    n = idx.shape[0]
    tile_before = jnp.cumsum(counts, axis=0) - counts
    total = jnp.sum(counts, axis=0)
    padded = (total + bm - 1) // bm * bm
    pad_end = jnp.cumsum(padded)
    pad_start = pad_end - padded
    first_row = tile_before + pad_start[None, :]
    early = first_row % SUBLANES
    size = jnp.where(counts > 0, (counts + early + SUBLANES - 1) // SUBLANES * SUBLANES, 0)
    stage_off = jnp.cumsum(size, axis=1) - size
    experts = idx[:, :TOP_K].reshape(n // rt, rt, TOP_K)
    rank = idx[:, TOP_K:2 * TOP_K]
    onehot = experts[..., None] == lax.broadcasted_iota(jnp.int32, (1, 1, 1, N_EXPERTS), 3)
    lookup = lambda tab: jnp.sum(jnp.where(onehot, tab[:, None, None, :], 0), axis=-1).reshape(n, TOP_K)
    dest = lookup(first_row) + rank
    pos = lookup(stage_off + early) + rank
    tables = tuple(t.reshape(-1).astype(jnp.int32) for t in (first_row - early, size, stage_off))
    block_start = jnp.arange(n_blocks, dtype=jnp.int32) * bm
    block_e = jnp.minimum(jnp.sum(pad_end[None, :] <= block_start[:, None], axis=1), N_EXPERTS - 1)
    n_used = (pad_end[-1] // bm).astype(jnp.int32).reshape(1)
    return dest.astype(jnp.int32), pos.astype(jnp.int32), tables, block_e.astype(jnp.int32), n_used


def _moe(hx, idx, counts, gates, xres, g2, layer, w1, b1, w2, b2, xs_buf, n_out_rows, n_lat, seq):
    n = hx.shape[0]
    n_blocks = -(-(n * TOP_K + N_EXPERTS * (MOE_BLOCK - 1)) // MOE_BLOCK) + 1
    per_tile = TOK_TILE // ROUTE_TILE
    counts = counts[:, :per_tile, :N_EXPERTS].reshape(n // ROUTE_TILE, N_EXPERTS)
    dest, pos, tables, block_e, n_used = _route(idx, counts, n_blocks)
    xs = _dispatch(hx, _dest_tiles(dest, TOK_TILE), xs_buf, TOK_TILE)
    eo = _experts(xs, n_blocks, block_e + layer * N_EXPERTS, n_used, w1, b1, w2, b2)
    n_rt = n_out_rows // ROUTE_TILE
    tables = tuple(t[:n_rt * N_EXPERTS] for t in tables)
    out = _combine(eo, tables, pos[:n_out_rows], xres, gates, g2, n_out_rows, n_lat, seq)
    return out, xs


def kernel(x, c, ctx, c_ctx, norm_mix_w, norm_ffn_w, w_ada, b_ada, w_in, mlstm_ig_b, mlstm_fg_b, mlstm_norm_w,
           na_qnorm_w, na_knorm_w, na_rpb, conv_w, conv_b, conv_ln_w, conv_ln_b, w_out, router_w, router_b,
           exp_w1, exp_b1, exp_w2, exp_b2):
    batch, seq, d = x.shape
    ctx_len = ctx.shape[1]
    depth = w_ada.shape[0]
    n_lat = batch * seq
    n_ctx = batch * ctx_len
    n_all = n_lat + n_ctx
    assert seq % TOK_TILE == 0 and n_ctx % TOK_TILE == 0 and seq % ctx_len == 0
    assert (seq // GRID_W) % NA_QROWS == 0 and seq // GRID_W >= NA_KROWS

    mod_rows = -(-(batch + 1) // 8) * 8
    cc = jnp.zeros((mod_rows, d), F32).at[:batch].set(c).at[batch].set(c_ctx)
    mods = _ada(cc, w_ada, b_ada)[:, :batch + 1].reshape(depth, batch + 1, 1, 6, d)
    x_parts = (x.reshape(n_lat, d), ctx.reshape(n_ctx, d))
    rope = _rope_tables(seq)
    n_blocks0 = -(-(n_all * TOP_K + N_EXPERTS * (MOE_BLOCK - 1)) // MOE_BLOCK) + 1
    xs_buf = jnp.zeros((n_blocks0 * MOE_BLOCK, d), F32)
    n_exp = exp_w1.shape[1]
    ew1 = exp_w1.reshape((depth * n_exp,) + exp_w1.shape[2:])
    eb1 = exp_b1.reshape((depth * n_exp,) + exp_b1.shape[2:])
    ew2 = exp_w2.reshape((depth * n_exp,) + exp_w2.shape[2:])
    eb2 = exp_b2.reshape((depth * n_exp,) + exp_b2.shape[2:])

    for l in range(depth):
        need_ctx = l < depth - 1
        sh1, sc1, g1, sh2, sc2, g2 = [mods[l, :, :, i, :] for i in range(6)]
        pa, g, gt, pb, u = _inproj(x_parts, sh1, sc1, norm_mix_w[l], w_in[l], n_lat, seq)
        a_lat, a_ctx = _mlstm(pa, g, gt, mlstm_ig_b[l], mlstm_fg_b[l], mlstm_norm_w[l], rope,
                              batch, seq, ctx_len, need_ctx)
        b_lat, b_ctx = _na(pb, na_qnorm_w[l], na_knorm_w[l], na_rpb[l], batch, seq, ctx_len, need_ctx)
        c_lat = _conv(u, conv_w[l], conv_b[l], conv_ln_w[l], conv_ln_b[l], 0, batch, seq)
        if need_ctx:
            c_ctx_out = _conv(u, conv_w[l], conv_b[l], conv_ln_w[l], conv_ln_b[l], n_lat // ctx_len, batch, ctx_len)
            a_all, b_all, c_all = (a_lat, a_ctx), (b_lat, b_ctx), (c_lat, c_ctx_out)
            n_rows = n_all
        else:
            a_all, b_all, c_all = (a_lat,), (b_lat,), (c_lat,)
            n_rows = n_lat
        if len(x_parts) != len(a_all):
            join = lambda parts: (jnp.concatenate(parts, axis=0),) if len(parts) > 1 else parts
            a_all, b_all, c_all, x_parts = join(a_all), join(b_all), join(c_all), join(x_parts)
        xmid, hx, idx, gates, counts = _outproj(a_all, b_all, c_all, x_parts, g1, sh2, sc2, norm_ffn_w[l], w_out[l],
                                                router_w[l], router_b[l], n_rows, n_lat, seq)
        xall, xs_buf = _moe(hx, idx, counts, gates, xmid, g2, l, ew1, eb1, ew2, eb2, xs_buf, n_rows, n_lat, seq)
        x_parts = (xall,)
    return x_parts[0][:n_lat].reshape(batch, seq, d)
```

```python
import functools

import numpy as np
import jax
import jax.numpy as jnp
from jax import lax
from jax.experimental import pallas as pl
from jax.experimental.pallas import tpu as pltpu

F32 = jnp.float32
BF16 = jnp.bfloat16
MXU_DTYPE = BF16

GRID_W = 64
HEAD_DIM = 64
MLSTM_HEADS = 4
NA_HEADS = 8
CONV_CH = 256
MLSTM_W = MLSTM_HEADS * HEAD_DIM
NA_W = NA_HEADS * HEAD_DIM
NA_WIN_R = 8
NA_WIN_C = 16
CONV_WIDTH = 31
ROPE_BASE = 10000.0
N_EXPERTS = 32
TOP_K = 4
SWIGLU_LIMIT = 7.0
SWIGLU_ALPHA = 1.702
EPS = 1e-6

A_Q = 0
A_G = 4 * MLSTM_W
B_Q = A_G + 4 * MLSTM_HEADS
C_A = B_Q + 3 * NA_W
IN_COLS = C_A + 2 * CONV_CH

LANES = 128
SUBLANES = 8
NEG_BIG = -1e30
LOG2_E = 1.4426950408889634
VMEM_LIMIT = 56 * 1024 * 1024

NA_QROWS = 4
NA_KROWS = NA_QROWS + NA_WIN_R - 1
TOK_TILE = 512
MOE_BLOCK = 512
ROUTE_TILE = 256
COMBINE_CHUNKS = (256, 128, 64, 32, 16, 8)
CONV_ROWS = 64
CONV_PAD = 16


def _mm(a, b):
    return jnp.dot(a.astype(MXU_DTYPE), b.astype(MXU_DTYPE), preferred_element_type=F32)


def _mm_nt(a, b):
    return lax.dot_general(a.astype(MXU_DTYPE), b.astype(MXU_DTYPE), (((1,), (1,)), ((), ())),
                           preferred_element_type=F32)


def _mm_tn(a, b):
    return lax.dot_general(a.astype(MXU_DTYPE), b.astype(MXU_DTYPE), (((0,), (0,)), ((), ())),
                           preferred_element_type=F32)


def _cparams(sem):
    return pltpu.CompilerParams(dimension_semantics=sem, vmem_limit_bytes=VMEM_LIMIT)


def _ada_kernel(c_ref, w_ref, b_ref, o_ref):
    cc = c_ref[...]
    s = cc * jax.nn.sigmoid(cc)
    o_ref[0] = _mm(s, w_ref[0]) + b_ref[0]


def _ada(cc, w_ada, b_ada):
    depth, d, n = w_ada.shape
    rows = cc.shape[0]
    tn = 512
    return pl.pallas_call(
        _ada_kernel,
        grid=(depth, n // tn),
        in_specs=[pl.BlockSpec((rows, d), lambda l, j: (0, 0)),
                  pl.BlockSpec((1, d, tn), lambda l, j: (l, 0, j)),
                  pl.BlockSpec((1, 1, tn), lambda l, j: (l, 0, j))],
        out_specs=pl.BlockSpec((1, rows, tn), lambda l, j: (l, 0, j)),
        out_shape=jax.ShapeDtypeStruct((depth, rows, n), F32),
        compiler_params=_cparams(("parallel", "parallel")),
        name="ada_mod",
    )(cc, w_ada, b_ada.reshape(depth, 1, n))


def _token_specs(parts, tm, n_lat_tiles):
    cols = parts[0].shape[1]
    if len(parts) == 1:
        return [pl.BlockSpec((tm, cols), lambda i: (i, 0))]
    return [pl.BlockSpec((tm, cols), lambda i: (jnp.minimum(i, n_lat_tiles - 1), 0)),
            pl.BlockSpec((tm, cols), lambda i: (jnp.maximum(i - n_lat_tiles, 0), 0))]


def _token_tile(refs, n_lat_tiles):
    if len(refs) == 1:
        return refs[0][...]
    return jnp.where(pl.program_id(0) < n_lat_tiles, refs[0][...], refs[1][...])


def _inproj_kernel(n_x, n_lat_tiles, *refs):
    x_refs = refs[:n_x]
    (sh_ref, sc_ref, nw_ref, wa_ref, wkt_ref, wg_ref, wgt_ref, wb_ref, wc_ref,
     pa_ref, kt_ref, g_ref, gt_ref, pb_ref, u_ref) = refs[n_x:]
    x = _token_tile(x_refs, n_lat_tiles)
    ms = jnp.mean(x * x, axis=-1, keepdims=True)
    y = x * lax.rsqrt(ms + EPS) * nw_ref[...]
    h = (y * (1.0 + sc_ref[0]) + sh_ref[0]).astype(MXU_DTYPE)
    pa_ref[...] = _mm(h, wa_ref[...]).astype(pa_ref.dtype)
    kt_ref[...] = _mm_nt(wkt_ref[...], h).astype(kt_ref.dtype)
    g_ref[...] = _mm(h, wg_ref[...])
    gt_ref[...] = _mm_nt(wgt_ref[...], h)
    pb_ref[...] = _mm(h, wb_ref[...]).astype(pb_ref.dtype)
    pc = _mm(h, wc_ref[...])
    u_ref[...] = (pc[:, :CONV_CH] * jax.nn.sigmoid(pc[:, CONV_CH:])).astype(u_ref.dtype)


def _inproj(x_parts, shift, scale, norm_w, w_in, n_lat, seq):
    n = sum(p.shape[0] for p in x_parts)
    d = x_parts[0].shape[1]
    tm = TOK_TILE
    n_lat_tiles = n_lat // tm
    per_batch = seq // tm
    n_mod = shift.shape[0]

    def mod_map(i):
        return (jnp.where(i < n_lat_tiles, i // per_batch, n_mod - 1), 0, 0)

    wq, wk, wv, wo = (w_in[:, A_Q + j * MLSTM_W:A_Q + (j + 1) * MLSTM_W] for j in range(4))
    wv = jnp.pad(wv.reshape(d, MLSTM_HEADS, HEAD_DIM), ((0, 0), (0, 0), (0, LANES - HEAD_DIM)))
    wa = jnp.concatenate([wq, wv.reshape(d, MLSTM_HEADS * LANES), wo], axis=1).astype(MXU_DTYPE)
    wkt = wk.T.astype(MXU_DTYPE)
    wg = w_in[:, A_G:B_Q].astype(MXU_DTYPE)
    wb = w_in[:, B_Q:C_A].astype(MXU_DTYPE)
    wc = w_in[:, C_A:IN_COLS].astype(MXU_DTYPE)
    ng = B_Q - A_G
    full = lambda r, c: pl.BlockSpec((r, c), lambda i: (0, 0))
    return pl.pallas_call(
        functools.partial(_inproj_kernel, len(x_parts), n_lat_tiles),
        grid=(n // tm,),
        in_specs=_token_specs(x_parts, tm, n_lat_tiles) + [
                  pl.BlockSpec((1, 1, d), mod_map),
                  pl.BlockSpec((1, 1, d), mod_map),
                  full(1, d),
                  full(d, A_G), full(MLSTM_W, d), full(d, ng), full(ng, d), full(d, 3 * NA_W),
                  full(d, 2 * CONV_CH)],
        out_specs=[pl.BlockSpec((tm, A_G), lambda i: (i, 0)),
                   pl.BlockSpec((MLSTM_W, tm), lambda i: (0, i)),
                   pl.BlockSpec((tm, ng), lambda i: (i, 0)),
                   pl.BlockSpec((ng, tm), lambda i: (0, i)),
                   pl.BlockSpec((tm, 3 * NA_W), lambda i: (i, 0)),
                   pl.BlockSpec((tm, CONV_CH), lambda i: (i, 0))],
        out_shape=[jax.ShapeDtypeStruct((n, A_G), BF16),
                   jax.ShapeDtypeStruct((MLSTM_W, n), BF16),
                   jax.ShapeDtypeStruct((n, ng), F32),
                   jax.ShapeDtypeStruct((ng, n), F32),
                   jax.ShapeDtypeStruct((n, 3 * NA_W), BF16),
                   jax.ShapeDtypeStruct((n, CONV_CH), BF16)],
        compiler_params=_cparams(("parallel",)),
        name="in_proj",
    )(*x_parts, shift, scale, norm_w.reshape(1, d), wa, wkt, wg, wg.T, wb, wc)


def _split3(x):
    hi = x.astype(BF16)
    r1 = x - hi.astype(F32)
    mid = r1.astype(BF16)
    lo = (r1 - mid.astype(F32)).astype(BF16)
    return hi, mid, lo


def _tri_left(tri, x):
    return sum(jnp.dot(tri, p, preferred_element_type=F32) for p in _split3(x))


def _tri_right(x, tri):
    return sum(jnp.dot(p, tri, preferred_element_type=F32) for p in _split3(x))


def _log_sigmoid(x):
    return jnp.minimum(x, 0.0) - jnp.log(1.0 + jnp.exp(-jnp.abs(x)))


def _mlstm_direction(z, q, kt, vx, gcol, grow, c_st, m_st, h_ref, row0, lc, with_output):
    nh = MLSTM_HEADS
    ti = lax.broadcasted_iota(jnp.int32, (lc, lc), 0)
    si = lax.broadcasted_iota(jnp.int32, (lc, lc), 1)
    lower = si <= ti
    upper = si >= ti
    tl = jnp.where(lower, 1.0, 0.0).astype(BF16)
    tu = jnp.where(upper, 1.0, 0.0).astype(BF16)
    f_col = _log_sigmoid(gcol[:, 2 * nh + z * nh:2 * nh + (z + 1) * nh])
    i_row = grow[z * nh:(z + 1) * nh, :]
    f_row = _log_sigmoid(grow[2 * nh + z * nh:2 * nh + (z + 1) * nh, :])
    if z == 0:
        b_col = _tri_left(tl, f_col)
        b_row = _tri_right(f_row, tu)
        b_tot = b_col[lc - 1:lc, :]
        mask = lower
    else:
        b_col = _tri_left(tu, f_col)
        b_row = _tri_right(f_row, tl)
        b_tot = b_col[0:1, :]
        mask = upper
    a_row = i_row - b_row
    a_max = jnp.max(a_row, axis=1, keepdims=True)
    lane256 = lax.broadcasted_iota(jnp.int32, (1, MLSTM_W), 1)
    lane128 = lax.broadcasted_iota(jnp.int32, (1, LANES), 1)
    kt_mx = kt.astype(MXU_DTYPE)
    c_all = c_st[z].astype(MXU_DTYPE)
    for h in range(nh):
        r = z * nh + h
        ar = a_row[h:h + 1, :]
        bl = b_tot[:, h:h + 1]
        m_old = m_st[r:r + 1, 0:1]
        vh = vx[:, h * LANES:(h + 1) * LANES]
        vext = jnp.where(lane128 == HEAD_DIM, jnp.ones((), vh.dtype), vh).astype(MXU_DTYPE)
        if with_output:
            qh = jnp.where(lane256 // HEAD_DIM == h, q, 0.0).astype(MXU_DTYPE)
            am = jnp.where(mask, ar, NEG_BIG)
            g = jnp.maximum(m_old, jnp.max(am, axis=1, keepdims=True))
            s = _mm(qh, kt_mx) * jnp.exp(am - g)
            w_inter = jnp.exp(m_old - g)
            nd = _mm(s, vext) + w_inter * _mm(qh, c_all)
            den = nd[:, HEAD_DIM:HEAD_DIM + 1]
            hval = nd / jnp.maximum(jnp.abs(den), jnp.exp(-(b_col[:, h:h + 1] + g)))
            h_ref[pl.ds(row0, lc), h * LANES:(h + 1) * LANES] = hval
        m_new = bl + jnp.maximum(m_old, a_max[h:h + 1, :])
        w_row = jnp.exp(bl + ar - m_new)
        decay = jnp.exp(bl + m_old - m_new)
        rows = slice(h * HEAD_DIM, (h + 1) * HEAD_DIM)
        c_st[z, rows, :] = decay * c_st[z, rows, :] + _mm(kt[rows, :] * w_row, vext)
        m_st[r:r + 1, :] = jnp.broadcast_to(m_new, (1, LANES))


def _mlstm_kernel(lc, nc, need_ctx,
                  pa_c, kt_c, g_c, gt_c,
                  pa_f, kt_f, g_f, gt_f, cos_f, sin_f, cost_f, sint_f,
                  pa_b, kt_b, g_b, gt_b, cos_b, sin_b, cost_b, sint_b,
                  o_lat, o_ctx, brow_ref, bcol_ref, nw_ref, perm_ref,
                  *rest):
    if need_ctx:
        out_lat, out_ctx, hf, hb, c_st, m_st = rest
    else:
        out_lat, hf, hb, c_st, m_st = rest
        out_ctx = None
    s = pl.program_id(1)
    w = MLSTM_W
    k_scale = HEAD_DIM ** -0.5

    def load(pa, kt_ref, rope_refs):
        q = pa[:, 0:w].astype(F32)
        kt = kt_ref[...].astype(F32)
        vx = pa[:, w:w + MLSTM_HEADS * LANES]
        if rope_refs is not None:
            cos_ref, sin_ref, cost_ref, sint_ref = rope_refs
            q = q * cos_ref[...] + _mm(q, perm_ref[...]) * sin_ref[...]
            blk = HEAD_DIM // 4
            swapped = jnp.concatenate([kt[(i ^ 1) * blk:((i ^ 1) + 1) * blk, :] for i in range(w // blk)], axis=0)
            kt = kt * cost_ref[...] + swapped * sint_ref[...]
        return q, kt * k_scale, vx

    @pl.when(s == 0)
    def _():
        c_st[...] = jnp.zeros(c_st.shape, F32)
        m_st[...] = jnp.zeros(m_st.shape, F32)
        q, kt, vx = load(pa_c, kt_c, None)
        gcol = g_c[...] + brow_ref[...]
        grow = gt_c[...] + bcol_ref[...]
        for z, h_ref in ((0, hf), (1, hb)):
            _mlstm_direction(z, q, kt, vx, gcol, grow, c_st, m_st, h_ref, 0, lc, need_ctx)

    @pl.when(s > 0)
    def _():
        for z, h_ref, refs in ((0, hf, (pa_f, kt_f, g_f, gt_f, (cos_f, sin_f, cost_f, sint_f))),
                               (1, hb, (pa_b, kt_b, g_b, gt_b, (cos_b, sin_b, cost_b, sint_b)))):
            pa, kt_ref, g, gt, rope_refs = refs
            j = s - 1 if z == 0 else nc - s
            row0 = pl.multiple_of(lc + j * lc, lc)
            q, kt, vx = load(pa, kt_ref, rope_refs)
            gcol = g[...] + brow_ref[...]
            grow = gt[...] + bcol_ref[...]
            _mlstm_direction(z, q, kt, vx, gcol, grow, c_st, m_st, h_ref, row0, lc, True)

    @pl.when(s == nc)
    def _():
        lane128 = lax.broadcasted_iota(jnp.int32, (1, LANES), 1)
        mean_w = jnp.where(lax.broadcasted_iota(jnp.int32, (LANES, LANES), 0) < HEAD_DIM,
                           1.0 / HEAD_DIM, 0.0).astype(BF16)
        first = 0 if need_ctx else 1
        for ch in range(first, nc + 1):
            rows = slice(ch * lc, (ch + 1) * lc)
            if ch == 0:
                o_val, dst, dst_rows = o_ctx[...], out_ctx, slice(0, lc)
            else:
                dst_rows = slice((ch - 1) * lc, ch * lc)
                o_val, dst = o_lat[dst_rows, :], out_lat
            o_val = o_val.astype(F32)
            for p in range(MLSTM_HEADS // 2):
                pair = []
                for h in (2 * p, 2 * p + 1):
                    hv = hf[rows, h * LANES:(h + 1) * LANES] + hb[rows, h * LANES:(h + 1) * LANES]
                    hv = jnp.where(lane128 < HEAD_DIM, hv, 0.0)
                    sq = hv * hv
                    sq_hi = sq.astype(BF16)
                    sq_lo = (sq - sq_hi.astype(F32)).astype(BF16)
                    ms = (jnp.dot(sq_hi, mean_w, preferred_element_type=F32)
                          + jnp.dot(sq_lo, mean_w, preferred_element_type=F32))
                    pair.append(hv * lax.rsqrt(ms + EPS))
                packed = jnp.where(lane128 < HEAD_DIM, pair[0], pltpu.roll(pair[1], HEAD_DIM, 1))
                cols = slice(p * LANES, (p + 1) * LANES)
                res = packed * nw_ref[:, cols] * jax.nn.sigmoid(o_val[:, cols])
                dst[dst_rows, cols] = res.astype(dst.dtype)


def _rope_tables(seq):
    half = HEAD_DIM // 2
    quarter = half // 2
    t = jnp.arange(seq, dtype=jnp.int32)
    inv_freq = ROPE_BASE ** (-jnp.arange(quarter, dtype=F32) / quarter)
    parts_c, parts_s = [], []
    for pos in (t // GRID_W, t % GRID_W):
        ang = pos.astype(F32)[:, None] * inv_freq[None, :]
        parts_c += [jnp.cos(ang), jnp.cos(ang)]
        parts_s += [-jnp.sin(ang), jnp.sin(ang)]
    cos = jnp.tile(jnp.concatenate(parts_c, axis=-1), (1, MLSTM_HEADS))
    sin = jnp.tile(jnp.concatenate(parts_s, axis=-1), (1, MLSTM_HEADS))
    j = np.arange(MLSTM_W)
    partner = np.where(j % half < quarter, j + quarter, j - quarter)
    perm = np.zeros((MLSTM_W, MLSTM_W), np.float32)
    perm[partner, j] = 1.0
    return cos, sin, cos.T, sin.T, jnp.asarray(perm, dtype=MXU_DTYPE)


def _mlstm(pa, kt, g, gt, ig_b, fg_b, norm_w, rope, batch, seq, ctx_len, need_ctx):
    lc = ctx_len
    nc = seq // lc
    n_lat = batch * seq
    cos, sin, cos_t, sin_t, perm = rope
    bias = jnp.concatenate([ig_b.reshape(-1), fg_b.reshape(-1)]).astype(F32)
    ng = bias.shape[0]
    lat_blocks = n_lat // lc

    def fwd(b, s):
        return b * nc + jnp.maximum(s - 1, 0)

    def bwd(b, s):
        return b * nc + nc - jnp.maximum(s, 1)

    def fwd_c(b, s):
        return jnp.maximum(s - 1, 0)

    def bwd_c(b, s):
        return nc - jnp.maximum(s, 1)

    def lat_specs(chunk, chunk_c):
        return [pl.BlockSpec((lc, A_G), lambda b, s: (chunk(b, s), 0)),
                pl.BlockSpec((MLSTM_W, lc), lambda b, s: (0, chunk(b, s))),
                pl.BlockSpec((lc, ng), lambda b, s: (chunk(b, s), 0)),
                pl.BlockSpec((ng, lc), lambda b, s: (0, chunk(b, s))),
                pl.BlockSpec((lc, MLSTM_W), lambda b, s: (chunk_c(b, s), 0)),
                pl.BlockSpec((lc, MLSTM_W), lambda b, s: (chunk_c(b, s), 0)),
                pl.BlockSpec((MLSTM_W, lc), lambda b, s: (0, chunk_c(b, s))),
                pl.BlockSpec((MLSTM_W, lc), lambda b, s: (0, chunk_c(b, s)))]

    in_specs = ([pl.BlockSpec((lc, A_G), lambda b, s: (lat_blocks + b, 0)),
                 pl.BlockSpec((MLSTM_W, lc), lambda b, s: (0, lat_blocks + b)),
                 pl.BlockSpec((lc, ng), lambda b, s: (lat_blocks + b, 0)),
                 pl.BlockSpec((ng, lc), lambda b, s: (0, lat_blocks + b))]
                + lat_specs(fwd, fwd_c) + lat_specs(bwd, bwd_c)
                + [pl.BlockSpec((seq, MLSTM_W), lambda b, s: (b, 3)),
                   pl.BlockSpec((lc, MLSTM_W), lambda b, s: (lat_blocks + b, 3)),
                   pl.BlockSpec((1, ng), lambda b, s: (0, 0)),
                   pl.BlockSpec((ng, 1), lambda b, s: (0, 0)),
                   pl.BlockSpec((1, MLSTM_W), lambda b, s: (0, 0)),
                   pl.BlockSpec((MLSTM_W, MLSTM_W), lambda b, s: (0, 0))])
    out_specs = [pl.BlockSpec((seq, MLSTM_W), lambda b, s: (b, 0))]
    out_shape = [jax.ShapeDtypeStruct((n_lat, MLSTM_W), BF16)]
    if need_ctx:
        out_specs.append(pl.BlockSpec((lc, MLSTM_W), lambda b, s: (b, 0)))
        out_shape.append(jax.ShapeDtypeStruct((batch * ctx_len, MLSTM_W), BF16))
    t_all = ctx_len + seq
    outs = pl.pallas_call(
        functools.partial(_mlstm_kernel, lc, nc, need_ctx),
        grid=(batch, nc + 1),
        in_specs=in_specs,
        out_specs=out_specs,
        out_shape=out_shape,
        scratch_shapes=[pltpu.VMEM((t_all, MLSTM_HEADS * LANES), F32),
                        pltpu.VMEM((t_all, MLSTM_HEADS * LANES), F32),
                        pltpu.VMEM((2, MLSTM_W, LANES), F32),
                        pltpu.VMEM((2 * MLSTM_HEADS, LANES), F32)],
        compiler_params=_cparams(("parallel", "arbitrary")),
        name="mlstm",
    )(pa, kt, g, gt, pa, kt, g, gt, cos, sin, cos_t, sin_t, pa, kt, g, gt, cos, sin, cos_t, sin_t, pa, pa,
      bias.reshape(1, ng), bias.reshape(ng, 1), norm_w.reshape(1, MLSTM_W).astype(F32), perm)
    return (outs[0], outs[1]) if need_ctx else (outs[0], None)


def _na_patterns(n_rows):
    kr = min(NA_WIN_R, n_rows)
    n_dr = 2 * NA_WIN_R - 1
    pats, pat_ids, bases = [], [], []
    for gi in range(n_rows // NA_QROWS):
        base = int(np.clip(NA_QROWS * gi - NA_WIN_R // 2, 0, n_rows - NA_KROWS))
        dr = np.full((NA_QROWS, NA_KROWS), n_dr, np.int32)
        for qr in range(NA_QROWS):
            r = NA_QROWS * gi + qr
            r0 = int(np.clip(r - kr // 2, 0, n_rows - kr))
            for kj in range(NA_KROWS):
                if r0 <= base + kj < r0 + kr:
                    dr[qr, kj] = base + kj - r + NA_WIN_R - 1
        for pi, p in enumerate(pats):
            if np.array_equal(p, dr):
                pat_ids.append(pi)
                break
        else:
            pat_ids.append(len(pats))
            pats.append(dr)
        bases.append(base)
    return tuple(pat_ids), tuple(bases), np.stack(pats)


def _na_bias_table(rpb, row_idx):
    heads = rpb.shape[0]
    col = np.arange(GRID_W)
    col_start = np.clip(col - NA_WIN_C // 2, 0, GRID_W - NA_WIN_C)
    in_win = (col[None, :] >= col_start[:, None]) & (col[None, :] < col_start[:, None] + NA_WIN_C)
    dc = np.clip(col[None, :] - col[:, None] + NA_WIN_C - 1, 0, 2 * NA_WIN_C - 2)
    onehot = (dc[None] == np.arange(2 * NA_WIN_C - 1)[:, None, None]).astype(np.float32)
    planes = jnp.einsum('hdc,cqk->hdqk', rpb, onehot, precision=lax.Precision.HIGHEST)
    planes = jnp.where(in_win[None, None], planes, NEG_BIG)
    planes = jnp.concatenate([planes, jnp.full((heads, 1, GRID_W, GRID_W), NEG_BIG, F32)], axis=1)
    npat = row_idx.shape[0]
    tab = planes[:, row_idx.reshape(-1)].reshape(heads, npat, NA_QROWS, NA_KROWS, GRID_W, GRID_W)
    return tab.transpose(0, 1, 2, 4, 3, 5).reshape(heads, npat, NA_QROWS * GRID_W, NA_KROWS * GRID_W)


def _na_kernel(pat_ids, bases, need_ctx, q_ref, k_ref, v_ref, kc_ref, vc_ref, *rest):
    if need_ctx:
        qc_ref, bias_ref, qw_ref, kw_ref, out_ref, outc_ref, kn_s, kcn_s = rest
    else:
        bias_ref, qw_ref, kw_ref, out_ref, kn_s, kcn_s = rest
    lane = lax.broadcasted_iota(jnp.int32, (1, LANES), 1)
    low = lane < HEAD_DIM
    inv_d = 1.0 / HEAD_DIM

    def rmsn(x, w):
        x2 = x * x
        s0 = jnp.sum(jnp.where(low, x2, 0.0), axis=-1, keepdims=True)
        s1 = jnp.sum(jnp.where(low, 0.0, x2), axis=-1, keepdims=True)
        r = jnp.where(low, lax.rsqrt(s0 * inv_d + EPS), lax.rsqrt(s1 * inv_d + EPS))
        return x * r * w

    qw = qw_ref[...]
    kn_s[...] = rmsn(k_ref[...].astype(F32), kw_ref[...]).astype(kn_s.dtype)
    kcn_s[...] = rmsn(kc_ref[...].astype(F32), kw_ref[...]).astype(kcn_s.dtype)
    kcn = kcn_s[...]
    vc = vc_ref[...]
    scale = HEAD_DIM ** -0.5 * LOG2_E
    nq = NA_QROWS * GRID_W
    nk = NA_KROWS * GRID_W

    def attend(qn, parts):
        outs = []
        for hh in range(2):
            qh = jnp.where(low if hh == 0 else jnp.logical_not(low), qn, 0.0).astype(MXU_DTYPE)
            scores = []
            for keys, _, bias in parts:
                sc = _mm_nt(qh, keys)
                if bias is not None:
                    sc = sc + bias[hh]
                scores.append(sc)
            m = scores[0].max(axis=-1, keepdims=True)
            for sc in scores[1:]:
                m = jnp.maximum(m, sc.max(axis=-1, keepdims=True))
            acc = None
            den = None
            for sc, (_, vals, _) in zip(scores, parts):
                p = jnp.exp2(sc - m)
                d = jnp.sum(p, axis=-1, keepdims=True)
                o = _mm(p, vals)
                acc = o if acc is None else acc + o
                den = d if den is None else den + d
            outs.append(acc / den)
        return jnp.where(low, outs[0], outs[1])

    for gi, (pid, base) in enumerate(zip(pat_ids, bases)):
        qn = rmsn(q_ref[gi * nq:(gi + 1) * nq, :].astype(F32), qw) * scale
        kwin = kn_s[base * GRID_W:base * GRID_W + nk, :]
        vwin = v_ref[base * GRID_W:base * GRID_W + nk, :]
        bias = (bias_ref[0, pid], bias_ref[1, pid])
        res = attend(qn, [(kwin, vwin, bias), (kcn, vc, None)])
        out_ref[gi * nq:(gi + 1) * nq, :] = res.astype(out_ref.dtype)

    if need_ctx:
        qn = rmsn(qc_ref[...].astype(F32), qw) * scale
        outc_ref[...] = attend(qn, [(kcn, vc, None)]).astype(outc_ref.dtype)


def _na(pb, qn_w, kn_w, rpb, batch, seq, ctx_len, need_ctx):
    n_rows = seq // GRID_W
    pat_ids, bases, row_idx = _na_patterns(n_rows)
    npat = row_idx.shape[0]
    nq, nk = NA_QROWS * GRID_W, NA_KROWS * GRID_W
    bias = _na_bias_table(rpb.astype(F32) * LOG2_E, row_idx)
    n_lat = batch * seq
    pairs = NA_HEADS // 2
    qoff, koff, voff = 0, pairs, 2 * pairs

    in_specs = [pl.BlockSpec((seq, LANES), lambda p, b: (b, qoff + p)),
                pl.BlockSpec((seq, LANES), lambda p, b: (b, koff + p)),
                pl.BlockSpec((seq, LANES), lambda p, b: (b, voff + p)),
                pl.BlockSpec((ctx_len, LANES), lambda p, b: (n_lat // ctx_len + b, koff + p)),
                pl.BlockSpec((ctx_len, LANES), lambda p, b: (n_lat // ctx_len + b, voff + p))]
    args = [pb, pb, pb, pb, pb]
    if need_ctx:
        in_specs.append(pl.BlockSpec((ctx_len, LANES), lambda p, b: (n_lat // ctx_len + b, qoff + p)))
        args.append(pb)
    in_specs += [pl.BlockSpec((2, npat, nq, nk), lambda p, b: (p, 0, 0, 0)),
                 pl.BlockSpec((1, LANES), lambda p, b: (0, 0)),
                 pl.BlockSpec((1, LANES), lambda p, b: (0, 0))]
    args += [bias, jnp.tile(qn_w.astype(F32), 2).reshape(1, LANES), jnp.tile(kn_w.astype(F32), 2).reshape(1, LANES)]
    out_specs = [pl.BlockSpec((seq, LANES), lambda p, b: (b, p))]
    out_shape = [jax.ShapeDtypeStruct((n_lat, NA_W), BF16)]
    if need_ctx:
        out_specs.append(pl.BlockSpec((ctx_len, LANES), lambda p, b: (b, p)))
        out_shape.append(jax.ShapeDtypeStruct((batch * ctx_len, NA_W), BF16))
    outs = pl.pallas_call(
        functools.partial(_na_kernel, pat_ids, bases, need_ctx),
        grid=(pairs, batch),
        in_specs=in_specs,
        out_specs=out_specs,
        out_shape=out_shape,
        scratch_shapes=[pltpu.VMEM((seq, LANES), MXU_DTYPE), pltpu.VMEM((ctx_len, LANES), MXU_DTYPE)],
        compiler_params=_cparams(("parallel", "parallel")),
        name="na_attn",
    )(*args)
    return (outs[0], outs[1]) if need_ctx else (outs[0], None)


def _conv_kernel(t_len, u_ref, w_ref, cb_ref, lw_ref, lb_ref, o_ref, pad_s):
    zeros = jnp.zeros((CONV_PAD, CONV_CH), F32)
    pad_s[0:CONV_PAD, :] = zeros
    pad_s[CONV_PAD + t_len:2 * CONV_PAD + t_len, :] = zeros
    pad_s[CONV_PAD:CONV_PAD + t_len, :] = u_ref[...].astype(F32)
    shift = CONV_PAD - CONV_WIDTH // 2

    def body(c, carry):
        r0 = pl.multiple_of(c * CONV_ROWS, CONV_ROWS)
        n_win = CONV_ROWS + 2 * CONV_PAD
        win = pad_s[pl.ds(r0, n_win), :]
        rot = [win] + [pltpu.roll(win, n_win - ph, 0) for ph in range(1, SUBLANES)]
        acc = jnp.zeros((CONV_ROWS, CONV_CH), F32) + cb_ref[...]
        for j in range(CONV_WIDTH):
            ph, al = (j + shift) % SUBLANES, (j + shift) // SUBLANES * SUBLANES
            acc = acc + rot[ph][al:al + CONV_ROWS, :] * w_ref[j:j + 1, :]
        mean = jnp.mean(acc, axis=-1, keepdims=True)
        xc = acc - mean
        var = jnp.mean(xc * xc, axis=-1, keepdims=True)
        y = xc * lax.rsqrt(var + EPS) * lw_ref[...] + lb_ref[...]
        o_ref[pl.ds(r0, CONV_ROWS), :] = (y * jax.nn.sigmoid(y)).astype(o_ref.dtype)
        return carry

    lax.fori_loop(0, t_len // CONV_ROWS, body, 0)


def _conv(u, conv_w, conv_b, ln_w, ln_b, first_block, n_seq, t_len):
    row = lambda a: a.reshape(1, CONV_CH).astype(F32)
    return pl.pallas_call(
        functools.partial(_conv_kernel, t_len),
        grid=(n_seq,),
        in_specs=[pl.BlockSpec((t_len, CONV_CH), lambda b: (first_block + b, 0)),
                  pl.BlockSpec((CONV_WIDTH, CONV_CH), lambda b: (0, 0)),
                  pl.BlockSpec((1, CONV_CH), lambda b: (0, 0)),
                  pl.BlockSpec((1, CONV_CH), lambda b: (0, 0)),
                  pl.BlockSpec((1, CONV_CH), lambda b: (0, 0))],
        out_specs=pl.BlockSpec((t_len, CONV_CH), lambda b: (b, 0)),
        out_shape=jax.ShapeDtypeStruct((n_seq * t_len, CONV_CH), BF16),
        scratch_shapes=[pltpu.VMEM((t_len + 2 * CONV_PAD, CONV_CH), F32)],
        compiler_params=_cparams(("parallel",)),
        name="conv_module",
    )(u, conv_w.astype(F32), row(conv_b), row(ln_w), row(ln_b))


def _outproj_kernel(rt, n_src, n_lat_tiles, *refs):
    a_refs, b_refs, c_refs, x_refs = (refs[j * n_src:(j + 1) * n_src] for j in range(4))
    (g1_ref, sh_ref, sc_ref, nw_ref, wa_ref, wb_ref, wc_ref, rwh_ref, rwl_ref, rb_ref,
     xo_ref, hx_ref, idx_ref, gate_ref, cnt_ref) = refs[4 * n_src:]
    tile = lambda parts: _token_tile(parts, n_lat_tiles)
    mix = _mm(tile(a_refs), wa_ref[...]) + _mm(tile(b_refs), wb_ref[...]) + _mm(tile(c_refs), wc_ref[...])
    xn = tile(x_refs) + g1_ref[0] * mix
    xo_ref[...] = xn
    ms = jnp.mean(xn * xn, axis=-1, keepdims=True)
    hx = xn * lax.rsqrt(ms + EPS) * nw_ref[...] * (1.0 + sc_ref[0]) + sh_ref[0]
    hx_ref[...] = hx
    h_hi = hx.astype(BF16)
    h_lo = (hx - h_hi.astype(F32)).astype(BF16)
    logits = (jnp.dot(h_hi, rwh_ref[...], preferred_element_type=F32)
              + jnp.dot(h_lo, rwh_ref[...], preferred_element_type=F32)
              + jnp.dot(h_hi, rwl_ref[...], preferred_element_type=F32)) + rb_ref[...]
    lane = lax.broadcasted_iota(jnp.int32, logits.shape, 1)
    idx_out = jnp.zeros(logits.shape, jnp.int32)
    val_out = jnp.zeros(logits.shape, F32)
    top = None
    den = None
    sels = []
    for kk in range(TOP_K):
        m = jnp.max(logits, axis=-1, keepdims=True)
        sel = jnp.min(jnp.where(logits == m, lane, LANES), axis=-1, keepdims=True)
        if kk == 0:
            top = m
        e = jnp.exp(m - top)
        den = e if den is None else den + e
        idx_out = jnp.where(lane == kk, sel, idx_out)
        val_out = jnp.where(lane == kk, e, val_out)
        logits = jnp.where(lane == sel, -jnp.inf, logits)
        sels.append(sel)
    gate_ref[...] = val_out / den
    tm = logits.shape[0]
    chosen = jnp.where(logits == -jnp.inf, 1.0, 0.0)
    ti = lax.broadcasted_iota(jnp.int32, (tm, tm), 0)
    si = lax.broadcasted_iota(jnp.int32, (tm, tm), 1)
    earlier = jnp.where(jnp.logical_and(si < ti, si // rt == ti // rt), 1.0, 0.0).astype(BF16)
    before = jnp.dot(earlier, chosen.astype(BF16), preferred_element_type=F32)
    for kk in range(TOP_K):
        rank = jnp.sum(jnp.where(lane == sels[kk], before, 0.0), axis=-1, keepdims=True)
        idx_out = jnp.where(lane == TOP_K + kk, rank.astype(jnp.int32), idx_out)
    idx_ref[...] = idx_out
    row = lax.broadcasted_iota(jnp.int32, cnt_ref.shape[1:], 0)
    cnt = jnp.zeros(cnt_ref.shape[1:], F32)
    for sub in range(tm // rt):
        cnt = jnp.where(row == sub, jnp.sum(chosen[sub * rt:(sub + 1) * rt], axis=0, keepdims=True), cnt)
    cnt_ref[0] = cnt.astype(jnp.int32)


def _outproj(a, b, c, x_parts, g1, sh2, sc2, norm_w, w_out, router_w, router_b, n_rows, n_lat, seq):
    d = x_parts[0].shape[1]
    n_src = len(x_parts)
    assert len(a) == len(b) == len(c) == n_src
    tm = TOK_TILE
    n_lat_tiles = n_lat // tm
    per_batch = seq // tm
    n_mod = g1.shape[0]

    def mod_map(i):
        return (jnp.where(i < n_lat_tiles, i // per_batch, n_mod - 1), 0, 0)

    wa = w_out[0:MLSTM_W].astype(MXU_DTYPE)
    wb = w_out[MLSTM_W:MLSTM_W + NA_W].astype(MXU_DTYPE)
    wc = w_out[MLSTM_W + NA_W:].astype(MXU_DTYPE)
    rw = jnp.zeros((d, LANES), F32).at[:, :N_EXPERTS].set(router_w.astype(F32))
    rw_hi = rw.astype(BF16)
    rw_lo = (rw - rw_hi.astype(F32)).astype(BF16)
    rb = jnp.full((1, LANES), NEG_BIG, F32).at[0, :N_EXPERTS].set(router_b.astype(F32))
    full = lambda r, cc: pl.BlockSpec((r, cc), lambda i: (0, 0))
    tile = lambda cc: pl.BlockSpec((tm, cc), lambda i: (i, 0))
    return pl.pallas_call(
        functools.partial(_outproj_kernel, ROUTE_TILE, n_src, n_lat_tiles),
        grid=(n_rows // tm,),
        in_specs=[spec for parts in (a, b, c, x_parts) for spec in _token_specs(parts, tm, n_lat_tiles)] + [
                  pl.BlockSpec((1, 1, d), mod_map), pl.BlockSpec((1, 1, d), mod_map),
                  pl.BlockSpec((1, 1, d), mod_map), full(1, d),
                  full(MLSTM_W, d), full(NA_W, d), full(CONV_CH, d), full(d, LANES), full(d, LANES),
                  full(1, LANES)],
        out_specs=[tile(d), tile(d), tile(LANES), tile(LANES),
                   pl.BlockSpec((1, SUBLANES, LANES), lambda i: (i, 0, 0))],
        out_shape=[jax.ShapeDtypeStruct((n_rows, d), F32), jax.ShapeDtypeStruct((n_rows, d), F32),
                   jax.ShapeDtypeStruct((n_rows, LANES), jnp.int32), jax.ShapeDtypeStruct((n_rows, LANES), F32),
                   jax.ShapeDtypeStruct((n_rows // tm, SUBLANES, LANES), jnp.int32)],
        compiler_params=_cparams(("parallel",)),
        name="out_proj",
    )(*a, *b, *c, *x_parts, g1, sh2, sc2, norm_w.reshape(1, d).astype(F32), wa, wb, wc, rw_hi, rw_lo, rb)


def _dispatch_kernel(tm, dest_ref, hx_ref, xs_in, xs_out, sem):
    del xs_in

    def body(r, carry):
        for kk in range(TOP_K):
            dst = dest_ref[0, 0, kk * tm + r]
            pltpu.make_async_copy(hx_ref.at[pl.ds(r, 1), :], xs_out.at[pl.ds(dst, 1), :], sem).start()
        return carry

    lax.fori_loop(0, tm, body, 0)
    for kk in range(TOP_K):
        pltpu.make_async_copy(hx_ref, xs_out.at[pl.ds(0, tm), :], sem).wait()


def _dispatch(hx, dest_tiles, xs_init, tm):
    n, d = hx.shape
    return pl.pallas_call(
        functools.partial(_dispatch_kernel, tm),
        grid=(n // tm,),
        in_specs=[pl.BlockSpec((1, 1, TOP_K * tm), lambda i: (i, 0, 0), memory_space=pltpu.SMEM),
                  pl.BlockSpec((tm, d), lambda i: (i, 0)),
                  pl.BlockSpec(memory_space=pl.ANY)],
        out_specs=pl.BlockSpec(memory_space=pl.ANY),
        out_shape=jax.ShapeDtypeStruct(xs_init.shape, xs_init.dtype),
        scratch_shapes=[pltpu.SemaphoreType.DMA],
        input_output_aliases={2: 0},
        compiler_params=_cparams(("arbitrary",)),
        name="moe_dispatch",
    )(dest_tiles, hx, xs_init)


def _expert_kernel(be_ref, nu_ref, x_ref, w1_ref, b1_ref, w2_ref, b2_ref, o_ref, w1_s, w2_s):
    i = pl.program_id(0)
    de = w2_ref.shape[1]

    @pl.when(jnp.logical_or(i == 0, be_ref[i] != be_ref[jnp.maximum(i - 1, 0)]))
    def _():
        w1_s[...] = w1_ref[0].astype(w1_s.dtype)
        w2_s[...] = w2_ref[0].astype(w2_s.dtype)

    @pl.when(i < nu_ref[0])
    def _():
        h = _mm(x_ref[...], w1_s[...]) + b1_ref[0]
        glu = jnp.minimum(h[:, :de], SWIGLU_LIMIT)
        lin = jnp.clip(h[:, de:], -SWIGLU_LIMIT, SWIGLU_LIMIT)
        act = (lin + 1.0) * glu * jax.nn.sigmoid(SWIGLU_ALPHA * glu)
        o_ref[...] = _mm(act, w2_s[...]) + b2_ref[0]

    @pl.when(i >= nu_ref[0])
    def _():
        o_ref[...] = jnp.zeros(o_ref.shape, o_ref.dtype)


def _experts(xs, n_blocks, block_e, n_used, w1, b1, w2, b2):
    d = xs.shape[1]
    ne, _, two_de = w1.shape
    de = w2.shape[1]
    bm = MOE_BLOCK
    grid_spec = pltpu.PrefetchScalarGridSpec(
        num_scalar_prefetch=2,
        grid=(n_blocks,),
        in_specs=[pl.BlockSpec((bm, d), lambda i, be, nu: (i, 0)),
                  pl.BlockSpec((1, d, two_de), lambda i, be, nu: (be[i], 0, 0)),
                  pl.BlockSpec((1, 1, two_de), lambda i, be, nu: (be[i], 0, 0)),
                  pl.BlockSpec((1, de, d), lambda i, be, nu: (be[i], 0, 0)),
                  pl.BlockSpec((1, 1, d), lambda i, be, nu: (be[i], 0, 0))],
        out_specs=pl.BlockSpec((bm, d), lambda i, be, nu: (i, 0)),
        scratch_shapes=[pltpu.VMEM((d, two_de), MXU_DTYPE), pltpu.VMEM((de, d), MXU_DTYPE)],
    )
    return pl.pallas_call(
        _expert_kernel,
        grid_spec=grid_spec,
        out_shape=jax.ShapeDtypeStruct((n_blocks * bm, d), F32),
        compiler_params=_cparams(("arbitrary",)),
        name="moe_experts",
    )(block_e, n_used, xs, w1.astype(F32), b1.reshape(ne, 1, two_de).astype(F32), w2.astype(F32),
      b2.reshape(ne, 1, d).astype(F32))


def _combine_kernel(rt, src_ref, size_ref, soff_ref, eo_hbm, x_ref, pos_ref, gate_ref, g2_ref, xo_ref, stage, sem):
    i = pl.program_id(0)
    slot = i % 2
    sr = stage.shape[1]

    def run_chunks(tile, to_slot):
        for e in range(N_EXPERTS):
            base = tile * N_EXPERTS + e
            src, size, dst = src_ref[base], size_ref[base], soff_ref[base]
            for chunk in COMBINE_CHUNKS:
                done = size & ~(2 * chunk - 1)
                copy = pltpu.make_async_copy(
                    eo_hbm.at[pl.ds(pl.multiple_of(src + done, SUBLANES), chunk), :],
                    stage.at[to_slot, pl.ds(pl.multiple_of(dst + done, SUBLANES), chunk), :], sem.at[to_slot])
                yield (size & chunk) != 0, copy

    @pl.when(i == 0)
    def _():
        stage[...] = jnp.zeros(stage.shape, stage.dtype)
        for needed, copy in run_chunks(0, 0):
            pl.when(needed)(copy.start)

    @pl.when(i + 1 < pl.num_programs(0))
    def _():
        for needed, copy in run_chunks(i + 1, 1 - slot):
            pl.when(needed)(copy.start)

    for needed, copy in run_chunks(i, slot):
        pl.when(needed)(copy.wait)

    last = i * N_EXPERTS + N_EXPERTS - 1
    n_staged = soff_ref[last] + size_ref[last]
    row = lax.broadcasted_iota(jnp.int32, (sr, 1), 0)
    staged = jnp.where(row < n_staged, stage[slot], 0.0).astype(MXU_DTYPE)
    col = lax.broadcasted_iota(jnp.int32, (rt, sr), 1)
    pos = pos_ref[...]
    gates = gate_ref[...]
    sel = jnp.zeros((rt, sr), F32)
    for kk in range(TOP_K):
        sel = jnp.where(col == pos[:, kk:kk + 1], gates[:, kk:kk + 1], sel)
    sel_hi = sel.astype(BF16)
    sel_lo = (sel - sel_hi.astype(F32)).astype(BF16)
    y = (jnp.dot(sel_hi.astype(MXU_DTYPE), staged, preferred_element_type=F32)
         + jnp.dot(sel_lo.astype(MXU_DTYPE), staged, preferred_element_type=F32))
    xo_ref[...] = x_ref[...] + g2_ref[0] * y


def _combine(eo, tables, pos, xres, gates, g2, n_rows, n_lat, seq):
    d = xres.shape[1]
    rt = ROUTE_TILE
    n_lat_tiles = n_lat // rt
    per_batch = seq // rt
    n_mod = g2.shape[0]
    sr = TOP_K * rt + N_EXPERTS * 2 * (SUBLANES - 1)
    sr = -(-sr // 256) * 256

    def mod_map(i, *_):
        return (jnp.where(i < n_lat_tiles, i // per_batch, n_mod - 1), 0, 0)

    grid_spec = pltpu.PrefetchScalarGridSpec(
        num_scalar_prefetch=3,
        grid=(n_rows // rt,),
        in_specs=[pl.BlockSpec(memory_space=pl.ANY),
                  pl.BlockSpec((rt, d), lambda i, *_: (i, 0)),
                  pl.BlockSpec((rt, TOP_K), lambda i, *_: (i, 0)),
                  pl.BlockSpec((rt, LANES), lambda i, *_: (i, 0)),
                  pl.BlockSpec((1, 1, d), mod_map)],
        out_specs=pl.BlockSpec((rt, d), lambda i, *_: (i, 0)),
        scratch_shapes=[pltpu.VMEM((2, sr, d), F32), pltpu.SemaphoreType.DMA((2,))],
    )
    return pl.pallas_call(
        functools.partial(_combine_kernel, rt),
        grid_spec=grid_spec,
        out_shape=jax.ShapeDtypeStruct((n_rows, d), F32),
        compiler_params=_cparams(("arbitrary",)),
        name="moe_combine",
    )(*tables, eo, xres, pos, gates, g2)


def _dest_tiles(dest, tm):
    n = dest.shape[0]
    return dest.reshape(n // tm, tm, TOP_K).transpose(0, 2, 1).reshape(n // tm, 1, TOP_K * tm)


def _route(idx, counts, n_blocks):
    bm = MOE_BLOCK
    rt = ROUTE_TILE
    n = idx.shape[0]
    tile_before = jnp.cumsum(counts, axis=0) - counts
    total = jnp.sum(counts, axis=0)
    padded = (total + bm - 1) // bm * bm
    pad_end = jnp.cumsum(padded)
    pad_start = pad_end - padded
    first_row = tile_before + pad_start[None, :]
    early = first_row % SUBLANES
    size = jnp.where(counts > 0, (counts + early + SUBLANES - 1) // SUBLANES * SUBLANES, 0)
    stage_off = jnp.cumsum(size, axis=1) - size
    experts = idx[:, :TOP_K].reshape(n // rt, rt, TOP_K)
    rank = idx[:, TOP_K:2 * TOP_K]
    onehot = experts[..., None] == lax.broadcasted_iota(jnp.int32, (1, 1, 1, N_EXPERTS), 3)
    lookup = lambda tab: jnp.sum(jnp.where(onehot, tab[:, None, None, :], 0), axis=-1).reshape(n, TOP_K)
    dest = lookup(first_row) + rank
    pos = lookup(stage_off + early) + rank
    tables = tuple(t.reshape(-1).astype(jnp.int32) for t in (first_row - early, size, stage_off))
    block_start = jnp.arange(n_blocks, dtype=jnp.int32) * bm
    block_e = jnp.minimum(jnp.sum(pad_end[None, :] <= block_start[:, None], axis=1), N_EXPERTS - 1)
    n_used = (pad_end[-1] // bm).astype(jnp.int32).reshape(1)
    return dest.astype(jnp.int32), pos.astype(jnp.int32), tables, block_e.astype(jnp.int32), n_used


def _moe(hx, idx, counts, gates, xres, g2, layer, w1, b1, w2, b2, xs_buf, n_out_rows, n_lat, seq):
    n = hx.shape[0]
    n_blocks = -(-(n * TOP_K + N_EXPERTS * (MOE_BLOCK - 1)) // MOE_BLOCK) + 1
    per_tile = TOK_TILE // ROUTE_TILE
    counts = counts[:, :per_tile, :N_EXPERTS].reshape(n // ROUTE_TILE, N_EXPERTS)
    dest, pos, tables, block_e, n_used = _route(idx, counts, n_blocks)
    xs = _dispatch(hx, _dest_tiles(dest, TOK_TILE), xs_buf, TOK_TILE)
    eo = _experts(xs, n_blocks, block_e + layer * N_EXPERTS, n_used, w1, b1, w2, b2)
    n_rt = n_out_rows // ROUTE_TILE
    tables = tuple(t[:n_rt * N_EXPERTS] for t in tables)
    out = _combine(eo, tables, pos[:n_out_rows], xres, gates, g2, n_out_rows, n_lat, seq)
    return out, xs


def kernel(x, c, ctx, c_ctx, norm_mix_w, norm_ffn_w, w_ada, b_ada, w_in, mlstm_ig_b, mlstm_fg_b, mlstm_norm_w,
           na_qnorm_w, na_knorm_w, na_rpb, conv_w, conv_b, conv_ln_w, conv_ln_b, w_out, router_w, router_b,
           exp_w1, exp_b1, exp_w2, exp_b2):
    batch, seq, d = x.shape
    ctx_len = ctx.shape[1]
    depth = w_ada.shape[0]
    n_lat = batch * seq
    n_ctx = batch * ctx_len
    n_all = n_lat + n_ctx
    assert seq % TOK_TILE == 0 and n_ctx % TOK_TILE == 0 and seq % ctx_len == 0
    assert (seq // GRID_W) % NA_QROWS == 0 and seq // GRID_W >= NA_KROWS

    mod_rows = -(-(batch + 1) // 8) * 8
    cc = jnp.zeros((mod_rows, d), F32).at[:batch].set(c).at[batch].set(c_ctx)
    mods = _ada(cc, w_ada, b_ada)[:, :batch + 1].reshape(depth, batch + 1, 1, 6, d)
    x_parts = (x.reshape(n_lat, d), ctx.reshape(n_ctx, d))
    rope = _rope_tables(seq)
    n_blocks0 = -(-(n_all * TOP_K + N_EXPERTS * (MOE_BLOCK - 1)) // MOE_BLOCK) + 1
    xs_buf = jnp.zeros((n_blocks0 * MOE_BLOCK, d), F32)
    n_exp = exp_w1.shape[1]
    ew1 = exp_w1.reshape((depth * n_exp,) + exp_w1.shape[2:])
    eb1 = exp_b1.reshape((depth * n_exp,) + exp_b1.shape[2:])
    ew2 = exp_w2.reshape((depth * n_exp,) + exp_w2.shape[2:])
    eb2 = exp_b2.reshape((depth * n_exp,) + exp_b2.shape[2:])

    for l in range(depth):
        need_ctx = l < depth - 1
        sh1, sc1, g1, sh2, sc2, g2 = [mods[l, :, :, i, :] for i in range(6)]
        pa, kt, g, gt, pb, u = _inproj(x_parts, sh1, sc1, norm_mix_w[l], w_in[l], n_lat, seq)
        a_lat, a_ctx = _mlstm(pa, kt, g, gt, mlstm_ig_b[l], mlstm_fg_b[l], mlstm_norm_w[l], rope,
                              batch, seq, ctx_len, need_ctx)
        b_lat, b_ctx = _na(pb, na_qnorm_w[l], na_knorm_w[l], na_rpb[l], batch, seq, ctx_len, need_ctx)
        c_lat = _conv(u, conv_w[l], conv_b[l], conv_ln_w[l], conv_ln_b[l], 0, batch, seq)
        if need_ctx:
            c_ctx_out = _conv(u, conv_w[l], conv_b[l], conv_ln_w[l], conv_ln_b[l], n_lat // ctx_len, batch, ctx_len)
            a_all, b_all, c_all = (a_lat, a_ctx), (b_lat, b_ctx), (c_lat, c_ctx_out)
            n_rows = n_all
        else:
            a_all, b_all, c_all = (a_lat,), (b_lat,), (c_lat,)
            n_rows = n_lat
        if len(x_parts) != len(a_all):
            join = lambda parts: (jnp.concatenate(parts, axis=0),) if len(parts) > 1 else parts
            a_all, b_all, c_all, x_parts = join(a_all), join(b_all), join(c_all), join(x_parts)
        xmid, hx, idx, gates, counts = _outproj(a_all, b_all, c_all, x_parts, g1, sh2, sc2, norm_ffn_w[l], w_out[l],
                                                router_w[l], router_b[l], n_rows, n_lat, seq)
        xall, xs_buf = _moe(hx, idx, counts, gates, xmid, g2, l, ew1, eb1, ew2, eb2, xs_buf, n_rows, n_lat, seq)
        x_parts = (xall,)
    return x_parts[0][:n_lat].reshape(batch, seq, d)
```

```python
import functools

import numpy as np
import jax
import jax.numpy as jnp
from jax import lax
from jax.experimental import pallas as pl
from jax.experimental.pallas import tpu as pltpu

F32 = jnp.float32
BF16 = jnp.bfloat16
MXU_DTYPE = BF16

GRID_W = 64
HEAD_DIM = 64
MLSTM_HEADS = 4
NA_HEADS = 8
CONV_CH = 256
MLSTM_W = MLSTM_HEADS * HEAD_DIM
NA_W = NA_HEADS * HEAD_DIM
NA_WIN_R = 8
NA_WIN_C = 16
CONV_WIDTH = 31
ROPE_BASE = 10000.0
N_EXPERTS = 32
TOP_K = 4
SWIGLU_LIMIT = 7.0
SWIGLU_ALPHA = 1.702
EPS = 1e-6

A_Q = 0
A_G = 4 * MLSTM_W
B_Q = A_G + 4 * MLSTM_HEADS
C_A = B_Q + 3 * NA_W
IN_COLS = C_A + 2 * CONV_CH

LANES = 128
SUBLANES = 8
NEG_BIG = -1e30
LOG2_E = 1.4426950408889634
VMEM_LIMIT = 56 * 1024 * 1024

NA_QROWS = 4
NA_KROWS = NA_QROWS + NA_WIN_R - 1
TOK_TILE = 512
MOE_BLOCK = 512
ROUTE_TILE = 256
COMBINE_CHUNKS = (256, 128, 64, 32, 16, 8)
CONV_ROWS = 64
CONV_PAD = 16


def _mm(a, b):
    return jnp.dot(a.astype(MXU_DTYPE), b.astype(MXU_DTYPE), preferred_element_type=F32)


def _mm_nt(a, b):
    return lax.dot_general(a.astype(MXU_DTYPE), b.astype(MXU_DTYPE), (((1,), (1,)), ((), ())),
                           preferred_element_type=F32)


def _mm_tn(a, b):
    return lax.dot_general(a.astype(MXU_DTYPE), b.astype(MXU_DTYPE), (((0,), (0,)), ((), ())),
                           preferred_element_type=F32)


def _cparams(sem):
    return pltpu.CompilerParams(dimension_semantics=sem, vmem_limit_bytes=VMEM_LIMIT)


def _ada_kernel(c_ref, w_ref, b_ref, o_ref):
    cc = c_ref[...]
    s = cc * jax.nn.sigmoid(cc)
    o_ref[0] = _mm(s, w_ref[0]) + b_ref[0]


def _ada(cc, w_ada, b_ada):
    depth, d, n = w_ada.shape
    rows = cc.shape[0]
    tn = 512
    return pl.pallas_call(
        _ada_kernel,
        grid=(depth, n // tn),
        in_specs=[pl.BlockSpec((rows, d), lambda l, j: (0, 0)),
                  pl.BlockSpec((1, d, tn), lambda l, j: (l, 0, j)),
                  pl.BlockSpec((1, 1, tn), lambda l, j: (l, 0, j))],
        out_specs=pl.BlockSpec((1, rows, tn), lambda l, j: (l, 0, j)),
        out_shape=jax.ShapeDtypeStruct((depth, rows, n), F32),
        compiler_params=_cparams(("parallel", "parallel")),
        name="ada_mod",
    )(cc, w_ada, b_ada.reshape(depth, 1, n))


def _token_specs(parts, tm, n_lat_tiles):
    cols = parts[0].shape[1]
    if len(parts) == 1:
        return [pl.BlockSpec((tm, cols), lambda i: (i, 0))]
    return [pl.BlockSpec((tm, cols), lambda i: (jnp.minimum(i, n_lat_tiles - 1), 0)),
            pl.BlockSpec((tm, cols), lambda i: (jnp.maximum(i - n_lat_tiles, 0), 0))]


def _token_tile(refs, n_lat_tiles):
    if len(refs) == 1:
        return refs[0][...]
    return jnp.where(pl.program_id(0) < n_lat_tiles, refs[0][...], refs[1][...])


def _inproj_kernel(n_x, n_lat_tiles, *refs):
    x_refs = refs[:n_x]
    (sh_ref, sc_ref, nw_ref, wa_ref, wkt_ref, wg_ref, wgt_ref, wb_ref, wc_ref,
     pa_ref, kt_ref, g_ref, gt_ref, pb_ref, u_ref) = refs[n_x:]
    x = _token_tile(x_refs, n_lat_tiles)
    ms = jnp.mean(x * x, axis=-1, keepdims=True)
    y = x * lax.rsqrt(ms + EPS) * nw_ref[...]
    h = (y * (1.0 + sc_ref[0]) + sh_ref[0]).astype(MXU_DTYPE)
    pa_ref[...] = _mm(h, wa_ref[...]).astype(pa_ref.dtype)
    kt_ref[...] = _mm_nt(wkt_ref[...], h).astype(kt_ref.dtype)
    g_ref[...] = _mm(h, wg_ref[...])
    gt_ref[...] = _mm_nt(wgt_ref[...], h)
    pb_ref[...] = _mm(h, wb_ref[...]).astype(pb_ref.dtype)
    pc = _mm(h, wc_ref[...])
    u_ref[...] = (pc[:, :CONV_CH] * jax.nn.sigmoid(pc[:, CONV_CH:])).astype(u_ref.dtype)


def _inproj(x_parts, shift, scale, norm_w, w_in, n_lat, seq):
    n = sum(p.shape[0] for p in x_parts)
    d = x_parts[0].shape[1]
    tm = TOK_TILE
    n_lat_tiles = n_lat // tm
    per_batch = seq // tm
    n_mod = shift.shape[0]

    def mod_map(i):
        return (jnp.where(i < n_lat_tiles, i // per_batch, n_mod - 1), 0, 0)

    wq, wk, wv, wo = (w_in[:, A_Q + j * MLSTM_W:A_Q + (j + 1) * MLSTM_W] for j in range(4))
    wv = jnp.pad(wv.reshape(d, MLSTM_HEADS, HEAD_DIM), ((0, 0), (0, 0), (0, LANES - HEAD_DIM)))
    wa = jnp.concatenate([wq, wv.reshape(d, MLSTM_HEADS * LANES), wo], axis=1).astype(MXU_DTYPE)
    wkt = wk.T.astype(MXU_DTYPE)
    wg = w_in[:, A_G:B_Q].astype(MXU_DTYPE)
    wb = w_in[:, B_Q:C_A].astype(MXU_DTYPE)
    wc = w_in[:, C_A:IN_COLS].astype(MXU_DTYPE)
    ng = B_Q - A_G
    full = lambda r, c: pl.BlockSpec((r, c), lambda i: (0, 0))
    return pl.pallas_call(
        functools.partial(_inproj_kernel, len(x_parts), n_lat_tiles),
        grid=(n // tm,),
        in_specs=_token_specs(x_parts, tm, n_lat_tiles) + [
                  pl.BlockSpec((1, 1, d), mod_map),
                  pl.BlockSpec((1, 1, d), mod_map),
                  full(1, d),
                  full(d, A_G), full(MLSTM_W, d), full(d, ng), full(ng, d), full(d, 3 * NA_W),
                  full(d, 2 * CONV_CH)],
        out_specs=[pl.BlockSpec((tm, A_G), lambda i: (i, 0)),
                   pl.BlockSpec((MLSTM_W, tm), lambda i: (0, i)),
                   pl.BlockSpec((tm, ng), lambda i: (i, 0)),
                   pl.BlockSpec((ng, tm), lambda i: (0, i)),
                   pl.BlockSpec((tm, 3 * NA_W), lambda i: (i, 0)),
                   pl.BlockSpec((tm, CONV_CH), lambda i: (i, 0))],
        out_shape=[jax.ShapeDtypeStruct((n, A_G), BF16),
                   jax.ShapeDtypeStruct((MLSTM_W, n), BF16),
                   jax.ShapeDtypeStruct((n, ng), F32),
                   jax.ShapeDtypeStruct((ng, n), F32),
                   jax.ShapeDtypeStruct((n, 3 * NA_W), BF16),
                   jax.ShapeDtypeStruct((n, CONV_CH), BF16)],
        compiler_params=_cparams(("parallel",)),
        name="in_proj",
    )(*x_parts, shift, scale, norm_w.reshape(1, d), wa, wkt, wg, wg.T, wb, wc)


def _split3(x):
    hi = x.astype(BF16)
    r1 = x - hi.astype(F32)
    mid = r1.astype(BF16)
    lo = (r1 - mid.astype(F32)).astype(BF16)
    return hi, mid, lo


def _tri_left(tri, x):
    return sum(jnp.dot(tri, p, preferred_element_type=F32) for p in _split3(x))


def _tri_right(x, tri):
    return sum(jnp.dot(p, tri, preferred_element_type=F32) for p in _split3(x))


def _log_sigmoid(x):
    return jnp.minimum(x, 0.0) - jnp.log(1.0 + jnp.exp(-jnp.abs(x)))


def _mlstm_direction(z, q, kt, vx, gcol, grow, c_st, m_st, h_ref, row0, lc, with_output):
    nh = MLSTM_HEADS
    ti = lax.broadcasted_iota(jnp.int32, (lc, lc), 0)
    si = lax.broadcasted_iota(jnp.int32, (lc, lc), 1)
    lower = si <= ti
    upper = si >= ti
    tl = jnp.where(lower, 1.0, 0.0).astype(BF16)
    tu = jnp.where(upper, 1.0, 0.0).astype(BF16)
    f_col = _log_sigmoid(gcol[:, 2 * nh + z * nh:2 * nh + (z + 1) * nh])
    i_row = grow[z * nh:(z + 1) * nh, :]
    f_row = _log_sigmoid(grow[2 * nh + z * nh:2 * nh + (z + 1) * nh, :])
    if z == 0:
        b_col = _tri_left(tl, f_col)
        b_row = _tri_right(f_row, tu)
        b_tot = b_col[lc - 1:lc, :]
        mask = lower
    else:
        b_col = _tri_left(tu, f_col)
        b_row = _tri_right(f_row, tl)
        b_tot = b_col[0:1, :]
        mask = upper
    a_row = i_row - b_row
    a_max = jnp.max(a_row, axis=1, keepdims=True)
    lane256 = lax.broadcasted_iota(jnp.int32, (1, MLSTM_W), 1)
    lane128 = lax.broadcasted_iota(jnp.int32, (1, LANES), 1)
    kt_mx = kt.astype(MXU_DTYPE)
    c_all = c_st[z].astype(MXU_DTYPE)
    for h in range(nh):
        r = z * nh + h
        ar = a_row[h:h + 1, :]
        bl = b_tot[:, h:h + 1]
        m_old = m_st[r:r + 1, 0:1]
        vh = vx[:, h * LANES:(h + 1) * LANES]
        vext = jnp.where(lane128 == HEAD_DIM, jnp.ones((), vh.dtype), vh).astype(MXU_DTYPE)
        if with_output:
            qh = jnp.where(lane256 // HEAD_DIM == h, q, 0.0).astype(MXU_DTYPE)
            am = jnp.where(mask, ar, NEG_BIG)
            g = jnp.maximum(m_old, jnp.max(am, axis=1, keepdims=True))
            s = _mm(qh, kt_mx) * jnp.exp(am - g)
            w_inter = jnp.exp(m_old - g)
            nd = _mm(s, vext) + w_inter * _mm(qh, c_all)
            den = nd[:, HEAD_DIM:HEAD_DIM + 1]
            hval = nd / jnp.maximum(jnp.abs(den), jnp.exp(-(b_col[:, h:h + 1] + g)))
            h_ref[pl.ds(row0, lc), h * LANES:(h + 1) * LANES] = hval
        m_new = bl + jnp.maximum(m_old, a_max[h:h + 1, :])
        w_row = jnp.exp(bl + ar - m_new)
        decay = jnp.exp(bl + m_old - m_new)
        rows = slice(h * HEAD_DIM, (h + 1) * HEAD_DIM)
        c_st[z, rows, :] = decay * c_st[z, rows, :] + _mm(kt[rows, :] * w_row, vext)
        m_st[r:r + 1, :] = jnp.broadcast_to(m_new, (1, LANES))


def _mlstm_kernel(lc, nc, need_ctx,
                  pa_c, kt_c, g_c, gt_c,
                  pa_f, kt_f, g_f, gt_f, cos_f, sin_f, cost_f, sint_f,
                  pa_b, kt_b, g_b, gt_b, cos_b, sin_b, cost_b, sint_b,
                  o_lat, o_ctx, brow_ref, bcol_ref, nw_ref, perm_ref,
                  *rest):
    if need_ctx:
        out_lat, out_ctx, hf, hb, c_st, m_st = rest
    else:
        out_lat, hf, hb, c_st, m_st = rest
        out_ctx = None
    s = pl.program_id(1)
    w = MLSTM_W
    k_scale = HEAD_DIM ** -0.5

    def load(pa, kt_ref, rope_refs):
        q = pa[:, 0:w].astype(F32)
        kt = kt_ref[...].astype(F32)
        vx = pa[:, w:w + MLSTM_HEADS * LANES]
        if rope_refs is not None:
            cos_ref, sin_ref, cost_ref, sint_ref = rope_refs
            q = q * cos_ref[...] + _mm(q, perm_ref[...]) * sin_ref[...]
            blk = HEAD_DIM // 4
            swapped = jnp.concatenate([kt[(i ^ 1) * blk:((i ^ 1) + 1) * blk, :] for i in range(w // blk)], axis=0)
            kt = kt * cost_ref[...] + swapped * sint_ref[...]
        return q, kt * k_scale, vx

    @pl.when(s == 0)
    def _():
        c_st[...] = jnp.zeros(c_st.shape, F32)
        m_st[...] = jnp.zeros(m_st.shape, F32)
        q, kt, vx = load(pa_c, kt_c, None)
        gcol = g_c[...] + brow_ref[...]
        grow = gt_c[...] + bcol_ref[...]
        for z, h_ref in ((0, hf), (1, hb)):
            _mlstm_direction(z, q, kt, vx, gcol, grow, c_st, m_st, h_ref, 0, lc, need_ctx)

    @pl.when(s > 0)
    def _():
        for z, h_ref, refs in ((0, hf, (pa_f, kt_f, g_f, gt_f, (cos_f, sin_f, cost_f, sint_f))),
                               (1, hb, (pa_b, kt_b, g_b, gt_b, (cos_b, sin_b, cost_b, sint_b)))):
            pa, kt_ref, g, gt, rope_refs = refs
            j = s - 1 if z == 0 else nc - s
            row0 = pl.multiple_of(lc + j * lc, lc)
            q, kt, vx = load(pa, kt_ref, rope_refs)
            gcol = g[...] + brow_ref[...]
            grow = gt[...] + bcol_ref[...]
            _mlstm_direction(z, q, kt, vx, gcol, grow, c_st, m_st, h_ref, row0, lc, True)

    @pl.when(s == nc)
    def _():
        lane128 = lax.broadcasted_iota(jnp.int32, (1, LANES), 1)
        mean_w = jnp.where(lax.broadcasted_iota(jnp.int32, (LANES, LANES), 0) < HEAD_DIM,
                           1.0 / HEAD_DIM, 0.0).astype(BF16)
        first = 0 if need_ctx else 1
        for ch in range(first, nc + 1):
            rows = slice(ch * lc, (ch + 1) * lc)
            if ch == 0:
                o_val, dst, dst_rows = o_ctx[...], out_ctx, slice(0, lc)
            else:
                dst_rows = slice((ch - 1) * lc, ch * lc)
                o_val, dst = o_lat[dst_rows, :], out_lat
            o_val = o_val.astype(F32)
            for p in range(MLSTM_HEADS // 2):
                pair = []
                for h in (2 * p, 2 * p + 1):
                    hv = hf[rows, h * LANES:(h + 1) * LANES] + hb[rows, h * LANES:(h + 1) * LANES]
                    hv = jnp.where(lane128 < HEAD_DIM, hv, 0.0)
                    sq = hv * hv
                    sq_hi = sq.astype(BF16)
                    sq_lo = (sq - sq_hi.astype(F32)).astype(BF16)
                    ms = (jnp.dot(sq_hi, mean_w, preferred_element_type=F32)
                          + jnp.dot(sq_lo, mean_w, preferred_element_type=F32))
                    pair.append(hv * lax.rsqrt(ms + EPS))
                packed = jnp.where(lane128 < HEAD_DIM, pair[0], pltpu.roll(pair[1], HEAD_DIM, 1))
                cols = slice(p * LANES, (p + 1) * LANES)
                res = packed * nw_ref[:, cols] * jax.nn.sigmoid(o_val[:, cols])
                dst[dst_rows, cols] = res.astype(dst.dtype)


def _rope_tables(seq):
    half = HEAD_DIM // 2
    quarter = half // 2
    t = jnp.arange(seq, dtype=jnp.int32)
    inv_freq = ROPE_BASE ** (-jnp.arange(quarter, dtype=F32) / quarter)
    parts_c, parts_s = [], []
    for pos in (t // GRID_W, t % GRID_W):
        ang = pos.astype(F32)[:, None] * inv_freq[None, :]
        parts_c += [jnp.cos(ang), jnp.cos(ang)]
        parts_s += [-jnp.sin(ang), jnp.sin(ang)]
    cos = jnp.tile(jnp.concatenate(parts_c, axis=-1), (1, MLSTM_HEADS))
    sin = jnp.tile(jnp.concatenate(parts_s, axis=-1), (1, MLSTM_HEADS))
    j = np.arange(MLSTM_W)
    partner = np.where(j % half < quarter, j + quarter, j - quarter)
    perm = np.zeros((MLSTM_W, MLSTM_W), np.float32)
    perm[partner, j] = 1.0
    return cos, sin, cos.T, sin.T, jnp.asarray(perm, dtype=MXU_DTYPE)


def _mlstm(pa, kt, g, gt, ig_b, fg_b, norm_w, rope, batch, seq, ctx_len, need_ctx):
    lc = ctx_len
    nc = seq // lc
    n_lat = batch * seq
    cos, sin, cos_t, sin_t, perm = rope
    bias = jnp.concatenate([ig_b.reshape(-1), fg_b.reshape(-1)]).astype(F32)
    ng = bias.shape[0]
    lat_blocks = n_lat // lc

    def fwd(b, s):
        return b * nc + jnp.maximum(s - 1, 0)

    def bwd(b, s):
        return b * nc + nc - jnp.maximum(s, 1)

    def fwd_c(b, s):
        return jnp.maximum(s - 1, 0)

    def bwd_c(b, s):
        return nc - jnp.maximum(s, 1)

    def lat_specs(chunk, chunk_c):
        return [pl.BlockSpec((lc, A_G), lambda b, s: (chunk(b, s), 0)),
                pl.BlockSpec((MLSTM_W, lc), lambda b, s: (0, chunk(b, s))),
                pl.BlockSpec((lc, ng), lambda b, s: (chunk(b, s), 0)),
                pl.BlockSpec((ng, lc), lambda b, s: (0, chunk(b, s))),
                pl.BlockSpec((lc, MLSTM_W), lambda b, s: (chunk_c(b, s), 0)),
                pl.BlockSpec((lc, MLSTM_W), lambda b, s: (chunk_c(b, s), 0)),
                pl.BlockSpec((MLSTM_W, lc), lambda b, s: (0, chunk_c(b, s))),
                pl.BlockSpec((MLSTM_W, lc), lambda b, s: (0, chunk_c(b, s)))]

    in_specs = ([pl.BlockSpec((lc, A_G), lambda b, s: (lat_blocks + b, 0)),
                 pl.BlockSpec((MLSTM_W, lc), lambda b, s: (0, lat_blocks + b)),
                 pl.BlockSpec((lc, ng), lambda b, s: (lat_blocks + b, 0)),
                 pl.BlockSpec((ng, lc), lambda b, s: (0, lat_blocks + b))]
                + lat_specs(fwd, fwd_c) + lat_specs(bwd, bwd_c)
                + [pl.BlockSpec((seq, MLSTM_W), lambda b, s: (b, 3)),
                   pl.BlockSpec((lc, MLSTM_W), lambda b, s: (lat_blocks + b, 3)),
                   pl.BlockSpec((1, ng), lambda b, s: (0, 0)),
                   pl.BlockSpec((ng, 1), lambda b, s: (0, 0)),
                   pl.BlockSpec((1, MLSTM_W), lambda b, s: (0, 0)),
                   pl.BlockSpec((MLSTM_W, MLSTM_W), lambda b, s: (0, 0))])
    out_specs = [pl.BlockSpec((seq, MLSTM_W), lambda b, s: (b, 0))]
    out_shape = [jax.ShapeDtypeStruct((n_lat, MLSTM_W), BF16)]
    if need_ctx:
        out_specs.append(pl.BlockSpec((lc, MLSTM_W), lambda b, s: (b, 0)))
        out_shape.append(jax.ShapeDtypeStruct((batch * ctx_len, MLSTM_W), BF16))
    t_all = ctx_len + seq
    outs = pl.pallas_call(
        functools.partial(_mlstm_kernel, lc, nc, need_ctx),
        grid=(batch, nc + 1),
        in_specs=in_specs,
        out_specs=out_specs,
        out_shape=out_shape,
        scratch_shapes=[pltpu.VMEM((t_all, MLSTM_HEADS * LANES), F32),
                        pltpu.VMEM((t_all, MLSTM_HEADS * LANES), F32),
                        pltpu.VMEM((2, MLSTM_W, LANES), F32),
                        pltpu.VMEM((2 * MLSTM_HEADS, LANES), F32)],
        compiler_params=_cparams(("parallel", "arbitrary")),
        name="mlstm",
    )(pa, kt, g, gt, pa, kt, g, gt, cos, sin, cos_t, sin_t, pa, kt, g, gt, cos, sin, cos_t, sin_t, pa, pa,
      bias.reshape(1, ng), bias.reshape(ng, 1), norm_w.reshape(1, MLSTM_W).astype(F32), perm)
    return (outs[0], outs[1]) if need_ctx else (outs[0], None)


def _na_patterns(n_rows):
    kr = min(NA_WIN_R, n_rows)
    n_dr = 2 * NA_WIN_R - 1
    pats, pat_ids, bases = [], [], []
    for gi in range(n_rows // NA_QROWS):
        base = int(np.clip(NA_QROWS * gi - NA_WIN_R // 2, 0, n_rows - NA_KROWS))
        dr = np.full((NA_QROWS, NA_KROWS), n_dr, np.int32)
        for qr in range(NA_QROWS):
            r = NA_QROWS * gi + qr
            r0 = int(np.clip(r - kr // 2, 0, n_rows - kr))
            for kj in range(NA_KROWS):
                if r0 <= base + kj < r0 + kr:
                    dr[qr, kj] = base + kj - r + NA_WIN_R - 1
        for pi, p in enumerate(pats):
            if np.array_equal(p, dr):
                pat_ids.append(pi)
                break
        else:
            pat_ids.append(len(pats))
            pats.append(dr)
        bases.append(base)
    return tuple(pat_ids), tuple(bases), np.stack(pats)


def _na_bias_table(rpb, row_idx):
    heads = rpb.shape[0]
    col = np.arange(GRID_W)
    col_start = np.clip(col - NA_WIN_C // 2, 0, GRID_W - NA_WIN_C)
    in_win = (col[None, :] >= col_start[:, None]) & (col[None, :] < col_start[:, None] + NA_WIN_C)
    dc = np.clip(col[None, :] - col[:, None] + NA_WIN_C - 1, 0, 2 * NA_WIN_C - 2)
    onehot = (dc[None] == np.arange(2 * NA_WIN_C - 1)[:, None, None]).astype(np.float32)
    planes = jnp.einsum('hdc,cqk->hdqk', rpb, onehot, precision=lax.Precision.HIGHEST)
    planes = jnp.where(in_win[None, None], planes, NEG_BIG)
    planes = jnp.concatenate([planes, jnp.full((heads, 1, GRID_W, GRID_W), NEG_BIG, F32)], axis=1)
    npat = row_idx.shape[0]
    tab = planes[:, row_idx.reshape(-1)].reshape(heads, npat, NA_QROWS, NA_KROWS, GRID_W, GRID_W)
    return tab.transpose(0, 1, 2, 4, 3, 5).reshape(heads, npat, NA_QROWS * GRID_W, NA_KROWS * GRID_W)


def _na_kernel(pat_ids, bases, need_ctx, q_ref, k_ref, v_ref, kc_ref, vc_ref, *rest):
    if need_ctx:
        qc_ref, bias_ref, qw_ref, kw_ref, out_ref, outc_ref, kn_s, kcn_s = rest
    else:
        bias_ref, qw_ref, kw_ref, out_ref, kn_s, kcn_s = rest
    lane = lax.broadcasted_iota(jnp.int32, (1, LANES), 1)
    low = lane < HEAD_DIM
    inv_d = 1.0 / HEAD_DIM

    def rmsn(x, w):
        x2 = x * x
        s0 = jnp.sum(jnp.where(low, x2, 0.0), axis=-1, keepdims=True)
        s1 = jnp.sum(jnp.where(low, 0.0, x2), axis=-1, keepdims=True)
        r = jnp.where(low, lax.rsqrt(s0 * inv_d + EPS), lax.rsqrt(s1 * inv_d + EPS))
        return x * r * w

    qw = qw_ref[...]
    kn_s[...] = rmsn(k_ref[...].astype(F32), kw_ref[...]).astype(kn_s.dtype)
    kcn_s[...] = rmsn(kc_ref[...].astype(F32), kw_ref[...]).astype(kcn_s.dtype)
    kcn = kcn_s[...]
    vc = vc_ref[...]
    scale = HEAD_DIM ** -0.5 * LOG2_E
    nq = NA_QROWS * GRID_W
    nk = NA_KROWS * GRID_W

    def attend(qn, parts):
        outs = []
        for hh in range(2):
            qh = jnp.where(low if hh == 0 else jnp.logical_not(low), qn, 0.0).astype(MXU_DTYPE)
            scores = []
            for keys, _, bias in parts:
                sc = _mm_nt(qh, keys)
                if bias is not None:
                    sc = sc + bias[hh]
                scores.append(sc)
            m = scores[0].max(axis=-1, keepdims=True)
            for sc in scores[1:]:
                m = jnp.maximum(m, sc.max(axis=-1, keepdims=True))
            acc = None
            den = None
            for sc, (_, vals, _) in zip(scores, parts):
                p = jnp.exp2(sc - m)
                d = jnp.sum(p, axis=-1, keepdims=True)
                o = _mm(p, vals)
                acc = o if acc is None else acc + o
                den = d if den is None else den + d
            outs.append(acc / den)
        return jnp.where(low, outs[0], outs[1])

    for gi, (pid, base) in enumerate(zip(pat_ids, bases)):
        qn = rmsn(q_ref[gi * nq:(gi + 1) * nq, :].astype(F32), qw) * scale
        kwin = kn_s[base * GRID_W:base * GRID_W + nk, :]
        vwin = v_ref[base * GRID_W:base * GRID_W + nk, :]
        bias = (bias_ref[0, pid], bias_ref[1, pid])
        res = attend(qn, [(kwin, vwin, bias), (kcn, vc, None)])
        out_ref[gi * nq:(gi + 1) * nq, :] = res.astype(out_ref.dtype)

    if need_ctx:
        qn = rmsn(qc_ref[...].astype(F32), qw) * scale
        outc_ref[...] = attend(qn, [(kcn, vc, None)]).astype(outc_ref.dtype)


def _na(pb, qn_w, kn_w, rpb, batch, seq, ctx_len, need_ctx):
    n_rows = seq // GRID_W
    pat_ids, bases, row_idx = _na_patterns(n_rows)
    npat = row_idx.shape[0]
    nq, nk = NA_QROWS * GRID_W, NA_KROWS * GRID_W
    bias = _na_bias_table(rpb.astype(F32) * LOG2_E, row_idx)
    n_lat = batch * seq
    pairs = NA_HEADS // 2
    qoff, koff, voff = 0, pairs, 2 * pairs

    in_specs = [pl.BlockSpec((seq, LANES), lambda p, b: (b, qoff + p)),
                pl.BlockSpec((seq, LANES), lambda p, b: (b, koff + p)),
                pl.BlockSpec((seq, LANES), lambda p, b: (b, voff + p)),
                pl.BlockSpec((ctx_len, LANES), lambda p, b: (n_lat // ctx_len + b, koff + p)),
                pl.BlockSpec((ctx_len, LANES), lambda p, b: (n_lat // ctx_len + b, voff + p))]
    args = [pb, pb, pb, pb, pb]
    if need_ctx:
        in_specs.append(pl.BlockSpec((ctx_len, LANES), lambda p, b: (n_lat // ctx_len + b, qoff + p)))
        args.append(pb)
    in_specs += [pl.BlockSpec((2, npat, nq, nk), lambda p, b: (p, 0, 0, 0)),
                 pl.BlockSpec((1, LANES), lambda p, b: (0, 0)),
                 pl.BlockSpec((1, LANES), lambda p, b: (0, 0))]
    args += [bias, jnp.tile(qn_w.astype(F32), 2).reshape(1, LANES), jnp.tile(kn_w.astype(F32), 2).reshape(1, LANES)]
    out_specs = [pl.BlockSpec((seq, LANES), lambda p, b: (b, p))]
    out_shape = [jax.ShapeDtypeStruct((n_lat, NA_W), BF16)]
    if need_ctx:
        out_specs.append(pl.BlockSpec((ctx_len, LANES), lambda p, b: (b, p)))
        out_shape.append(jax.ShapeDtypeStruct((batch * ctx_len, NA_W), BF16))
    outs = pl.pallas_call(
        functools.partial(_na_kernel, pat_ids, bases, need_ctx),
        grid=(pairs, batch),
        in_specs=in_specs,
        out_specs=out_specs,
        out_shape=out_shape,
        scratch_shapes=[pltpu.VMEM((seq, LANES), MXU_DTYPE), pltpu.VMEM((ctx_len, LANES), MXU_DTYPE)],
        compiler_params=_cparams(("parallel", "parallel")),
        name="na_attn",
    )(*args)
    return (outs[0], outs[1]) if need_ctx else (outs[0], None)


def _conv_kernel(t_len, u_ref, w_ref, cb_ref, lw_ref, lb_ref, o_ref, pad_s):
    zeros = jnp.zeros((CONV_PAD, CONV_CH), F32)
    pad_s[0:CONV_PAD, :] = zeros
    pad_s[CONV_PAD + t_len:2 * CONV_PAD + t_len, :] = zeros
    pad_s[CONV_PAD:CONV_PAD + t_len, :] = u_ref[...].astype(F32)
    shift = CONV_PAD - CONV_WIDTH // 2

    def body(c, carry):
        r0 = pl.multiple_of(c * CONV_ROWS, CONV_ROWS)
        n_win = CONV_ROWS + 2 * CONV_PAD
        win = pad_s[pl.ds(r0, n_win), :]
        rot = [win] + [pltpu.roll(win, n_win - ph, 0) for ph in range(1, SUBLANES)]
        acc = jnp.zeros((CONV_ROWS, CONV_CH), F32) + cb_ref[...]
        for j in range(CONV_WIDTH):
            ph, al = (j + shift) % SUBLANES, (j + shift) // SUBLANES * SUBLANES
            acc = acc + rot[ph][al:al + CONV_ROWS, :] * w_ref[j:j + 1, :]
        mean = jnp.mean(acc, axis=-1, keepdims=True)
        xc = acc - mean
        var = jnp.mean(xc * xc, axis=-1, keepdims=True)
        y = xc * lax.rsqrt(var + EPS) * lw_ref[...] + lb_ref[...]
        o_ref[pl.ds(r0, CONV_ROWS), :] = (y * jax.nn.sigmoid(y)).astype(o_ref.dtype)
        return carry

    lax.fori_loop(0, t_len // CONV_ROWS, body, 0)


def _conv(u, conv_w, conv_b, ln_w, ln_b, first_block, n_seq, t_len):
    row = lambda a: a.reshape(1, CONV_CH).astype(F32)
    return pl.pallas_call(
        functools.partial(_conv_kernel, t_len),
        grid=(n_seq,),
        in_specs=[pl.BlockSpec((t_len, CONV_CH), lambda b: (first_block + b, 0)),
                  pl.BlockSpec((CONV_WIDTH, CONV_CH), lambda b: (0, 0)),
                  pl.BlockSpec((1, CONV_CH), lambda b: (0, 0)),
                  pl.BlockSpec((1, CONV_CH), lambda b: (0, 0)),
                  pl.BlockSpec((1, CONV_CH), lambda b: (0, 0))],
        out_specs=pl.BlockSpec((t_len, CONV_CH), lambda b: (b, 0)),
        out_shape=jax.ShapeDtypeStruct((n_seq * t_len, CONV_CH), BF16),
        scratch_shapes=[pltpu.VMEM((t_len + 2 * CONV_PAD, CONV_CH), F32)],
        compiler_params=_cparams(("parallel",)),
        name="conv_module",
    )(u, conv_w.astype(F32), row(conv_b), row(ln_w), row(ln_b))


def _outproj_kernel(rt, n_src, n_lat_tiles, *refs):
    a_refs, b_refs, c_refs, x_refs = (refs[j * n_src:(j + 1) * n_src] for j in range(4))
    (g1_ref, sh_ref, sc_ref, nw_ref, wa_ref, wb_ref, wc_ref, rwh_ref, rwl_ref, rb_ref,
     xo_ref, hx_ref, idx_ref, gate_ref, cnt_ref) = refs[4 * n_src:]
    tile = lambda parts: _token_tile(parts, n_lat_tiles)
    mix = _mm(tile(a_refs), wa_ref[...]) + _mm(tile(b_refs), wb_ref[...]) + _mm(tile(c_refs), wc_ref[...])
    xn = tile(x_refs) + g1_ref[0] * mix
    xo_ref[...] = xn
    ms = jnp.mean(xn * xn, axis=-1, keepdims=True)
    hx = xn * lax.rsqrt(ms + EPS) * nw_ref[...] * (1.0 + sc_ref[0]) + sh_ref[0]
    hx_ref[...] = hx
    h_hi = hx.astype(BF16)
    h_lo = (hx - h_hi.astype(F32)).astype(BF16)
    logits = (jnp.dot(h_hi, rwh_ref[...], preferred_element_type=F32)
              + jnp.dot(h_lo, rwh_ref[...], preferred_element_type=F32)
              + jnp.dot(h_hi, rwl_ref[...], preferred_element_type=F32)) + rb_ref[...]
    lane = lax.broadcasted_iota(jnp.int32, logits.shape, 1)
    idx_out = jnp.zeros(logits.shape, jnp.int32)
    val_out = jnp.zeros(logits.shape, F32)
    top = None
    den = None
    sels = []
    for kk in range(TOP_K):
        m = jnp.max(logits, axis=-1, keepdims=True)
        sel = jnp.min(jnp.where(logits == m, lane, LANES), axis=-1, keepdims=True)
        if kk == 0:
            top = m
        e = jnp.exp(m - top)
        den = e if den is None else den + e
        idx_out = jnp.where(lane == kk, sel, idx_out)
        val_out = jnp.where(lane == kk, e, val_out)
        logits = jnp.where(lane == sel, -jnp.inf, logits)
        sels.append(sel)
    gate_ref[...] = val_out / den
    tm = logits.shape[0]
    chosen = jnp.where(logits == -jnp.inf, 1.0, 0.0)
    ti = lax.broadcasted_iota(jnp.int32, (tm, tm), 0)
    si = lax.broadcasted_iota(jnp.int32, (tm, tm), 1)
    earlier = jnp.where(jnp.logical_and(si < ti, si // rt == ti // rt), 1.0, 0.0).astype(BF16)
    before = jnp.dot(earlier, chosen.astype(BF16), preferred_element_type=F32)
    for kk in range(TOP_K):
        rank = jnp.sum(jnp.where(lane == sels[kk], before, 0.0), axis=-1, keepdims=True)
        idx_out = jnp.where(lane == TOP_K + kk, rank.astype(jnp.int32), idx_out)
    idx_ref[...] = idx_out
    row = lax.broadcasted_iota(jnp.int32, cnt_ref.shape[1:], 0)
    cnt = jnp.zeros(cnt_ref.shape[1:], F32)
    for sub in range(tm // rt):
        cnt = jnp.where(row == sub, jnp.sum(chosen[sub * rt:(sub + 1) * rt], axis=0, keepdims=True), cnt)
    cnt_ref[0] = cnt.astype(jnp.int32)


def _outproj(a, b, c, x_parts, g1, sh2, sc2, norm_w, w_out, router_w, router_b, n_rows, n_lat, seq):
    d = x_parts[0].shape[1]
    n_src = len(x_parts)
    assert len(a) == len(b) == len(c) == n_src
    tm = TOK_TILE
    n_lat_tiles = n_lat // tm
    per_batch = seq // tm
    n_mod = g1.shape[0]

    def mod_map(i):
        return (jnp.where(i < n_lat_tiles, i // per_batch, n_mod - 1), 0, 0)

    wa = w_out[0:MLSTM_W].astype(MXU_DTYPE)
    wb = w_out[MLSTM_W:MLSTM_W + NA_W].astype(MXU_DTYPE)
    wc = w_out[MLSTM_W + NA_W:].astype(MXU_DTYPE)
    rw = jnp.zeros((d, LANES), F32).at[:, :N_EXPERTS].set(router_w.astype(F32))
    rw_hi = rw.astype(BF16)
    rw_lo = (rw - rw_hi.astype(F32)).astype(BF16)
    rb = jnp.full((1, LANES), NEG_BIG, F32).at[0, :N_EXPERTS].set(router_b.astype(F32))
    full = lambda r, cc: pl.BlockSpec((r, cc), lambda i: (0, 0))
    tile = lambda cc: pl.BlockSpec((tm, cc), lambda i: (i, 0))
    return pl.pallas_call(
        functools.partial(_outproj_kernel, ROUTE_TILE, n_src, n_lat_tiles),
        grid=(n_rows // tm,),
        in_specs=[spec for parts in (a, b, c, x_parts) for spec in _token_specs(parts, tm, n_lat_tiles)] + [
                  pl.BlockSpec((1, 1, d), mod_map), pl.BlockSpec((1, 1, d), mod_map),
                  pl.BlockSpec((1, 1, d), mod_map), full(1, d),
                  full(MLSTM_W, d), full(NA_W, d), full(CONV_CH, d), full(d, LANES), full(d, LANES),
                  full(1, LANES)],
        out_specs=[tile(d), tile(d), tile(LANES), tile(LANES),
                   pl.BlockSpec((1, SUBLANES, LANES), lambda i: (i, 0, 0))],
        out_shape=[jax.ShapeDtypeStruct((n_rows, d), F32), jax.ShapeDtypeStruct((n_rows, d), F32),
                   jax.ShapeDtypeStruct((n_rows, LANES), jnp.int32), jax.ShapeDtypeStruct((n_rows, LANES), F32),
                   jax.ShapeDtypeStruct((n_rows // tm, SUBLANES, LANES), jnp.int32)],
        compiler_params=_cparams(("parallel",)),
        name="out_proj",
    )(*a, *b, *c, *x_parts, g1, sh2, sc2, norm_w.reshape(1, d).astype(F32), wa, wb, wc, rw_hi, rw_lo, rb)


def _expert_kernel(bm, be_ref, nu_ref, tok0_ref, tok1_ref, hx_hbm, w1_ref, b1_ref, w2_ref, b2_ref, o_ref,
                   w1_s, w2_s, xbuf0, xbuf1, sem):
    i = pl.program_id(0)
    n_used = nu_ref[0]
    de = w2_ref.shape[1]
    bufs = (xbuf0, xbuf1)

    def gather(tok_ref, to):
        for r in range(bm):
            pltpu.make_async_copy(hx_hbm.at[pl.ds(tok_ref[0, 0, r], 1), :],
                                  bufs[to].at[pl.ds(r, 1), :], sem.at[to]).start()

    def wait_rows(of):
        pltpu.make_async_copy(hx_hbm.at[pl.ds(0, bm), :], bufs[of], sem.at[of]).wait()

    @pl.when(i == 0)
    def _():
        gather(tok0_ref, 0)

    @pl.when(jnp.logical_or(i == 0, be_ref[i] != be_ref[jnp.maximum(i - 1, 0)]))
    def _():
        w1_s[...] = w1_ref[0].astype(w1_s.dtype)
        w2_s[...] = w2_ref[0].astype(w2_s.dtype)

    for parity in range(2):
        @pl.when(jnp.logical_and(i < n_used, i % 2 == parity))
        def _():
            wait_rows(parity)
            gather(tok1_ref, 1 - parity)
            h = _mm(bufs[parity][...], w1_s[...]) + b1_ref[0]
            glu = jnp.minimum(h[:, :de], SWIGLU_LIMIT)
            lin = jnp.clip(h[:, de:], -SWIGLU_LIMIT, SWIGLU_LIMIT)
            act = (lin + 1.0) * glu * jax.nn.sigmoid(SWIGLU_ALPHA * glu)
            o_ref[...] = _mm(act, w2_s[...]) + b2_ref[0]

        @pl.when(jnp.logical_and(i == n_used, i % 2 == parity))
        def _():
            wait_rows(parity)

    @pl.when(i >= n_used)
    def _():
        o_ref[...] = jnp.zeros(o_ref.shape, o_ref.dtype)


def _experts(hx, row_tok, n_blocks, block_e, n_used, w1, b1, w2, b2):
    d = hx.shape[1]
    ne, _, two_de = w1.shape
    de = w2.shape[1]
    bm = MOE_BLOCK
    smem_block = lambda index_map: pl.BlockSpec((1, 1, bm), index_map, memory_space=pltpu.SMEM)
    grid_spec = pltpu.PrefetchScalarGridSpec(
        num_scalar_prefetch=2,
        grid=(n_blocks,),
        in_specs=[smem_block(lambda i, be, nu: (i, 0, 0)),
                  smem_block(lambda i, be, nu: (jnp.minimum(i + 1, n_blocks - 1), 0, 0)),
                  pl.BlockSpec(memory_space=pl.ANY),
                  pl.BlockSpec((1, d, two_de), lambda i, be, nu: (be[i], 0, 0)),
                  pl.BlockSpec((1, 1, two_de), lambda i, be, nu: (be[i], 0, 0)),
                  pl.BlockSpec((1, de, d), lambda i, be, nu: (be[i], 0, 0)),
                  pl.BlockSpec((1, 1, d), lambda i, be, nu: (be[i], 0, 0))],
        out_specs=pl.BlockSpec((bm, d), lambda i, be, nu: (i, 0)),
        scratch_shapes=[pltpu.VMEM((d, two_de), MXU_DTYPE), pltpu.VMEM((de, d), MXU_DTYPE),
                        pltpu.VMEM((bm, d), F32), pltpu.VMEM((bm, d), F32), pltpu.SemaphoreType.DMA((2,))],
    )
    row_tok = row_tok.reshape(n_blocks, 1, bm)
    return pl.pallas_call(
        functools.partial(_expert_kernel, bm),
        grid_spec=grid_spec,
        out_shape=jax.ShapeDtypeStruct((n_blocks * bm, d), F32),
        compiler_params=_cparams(("arbitrary",)),
        name="moe_experts",
    )(block_e, n_used, row_tok, row_tok, hx, w1.astype(F32), b1.reshape(ne, 1, two_de).astype(F32),
      w2.astype(F32), b2.reshape(ne, 1, d).astype(F32))


def _combine_kernel(rt, src_ref, size_ref, soff_ref, eo_hbm, x_ref, pos_ref, gate_ref, g2_ref, xo_ref, stage, sem):
    i = pl.program_id(0)
    slot = i % 2
    sr = stage.shape[1]

    def run_chunks(tile, to_slot):
        for e in range(N_EXPERTS):
            base = tile * N_EXPERTS + e
            src, size, dst = src_ref[base], size_ref[base], soff_ref[base]
            for chunk in COMBINE_CHUNKS:
                done = size & ~(2 * chunk - 1)
                copy = pltpu.make_async_copy(
                    eo_hbm.at[pl.ds(pl.multiple_of(src + done, SUBLANES), chunk), :],
                    stage.at[to_slot, pl.ds(pl.multiple_of(dst + done, SUBLANES), chunk), :], sem.at[to_slot])
                yield (size & chunk) != 0, copy

    @pl.when(i == 0)
    def _():
        stage[...] = jnp.zeros(stage.shape, stage.dtype)
        for needed, copy in run_chunks(0, 0):
            pl.when(needed)(copy.start)

    @pl.when(i + 1 < pl.num_programs(0))
    def _():
        for needed, copy in run_chunks(i + 1, 1 - slot):
            pl.when(needed)(copy.start)

    for needed, copy in run_chunks(i, slot):
        pl.when(needed)(copy.wait)

    last = i * N_EXPERTS + N_EXPERTS - 1
    n_staged = soff_ref[last] + size_ref[last]
    row = lax.broadcasted_iota(jnp.int32, (sr, 1), 0)
    staged = jnp.where(row < n_staged, stage[slot], 0.0).astype(MXU_DTYPE)
    col = lax.broadcasted_iota(jnp.int32, (rt, sr), 1)
    pos = pos_ref[...]
    gates = gate_ref[...]
    sel = jnp.zeros((rt, sr), F32)
    for kk in range(TOP_K):
        sel = jnp.where(col == pos[:, kk:kk + 1], gates[:, kk:kk + 1], sel)
    sel_hi = sel.astype(BF16)
    sel_lo = (sel - sel_hi.astype(F32)).astype(BF16)
    y = (jnp.dot(sel_hi.astype(MXU_DTYPE), staged, preferred_element_type=F32)
         + jnp.dot(sel_lo.astype(MXU_DTYPE), staged, preferred_element_type=F32))
    xo_ref[...] = x_ref[...] + g2_ref[0] * y


def _combine(eo, tables, pos, xres, gates, g2, n_rows, n_lat, seq):
    d = xres.shape[1]
    rt = ROUTE_TILE
    n_lat_tiles = n_lat // rt
    per_batch = seq // rt
    n_mod = g2.shape[0]
    sr = TOP_K * rt + N_EXPERTS * 2 * (SUBLANES - 1)
    sr = -(-sr // 256) * 256

    def mod_map(i, *_):
        return (jnp.where(i < n_lat_tiles, i // per_batch, n_mod - 1), 0, 0)

    grid_spec = pltpu.PrefetchScalarGridSpec(
        num_scalar_prefetch=3,
        grid=(n_rows // rt,),
        in_specs=[pl.BlockSpec(memory_space=pl.ANY),
                  pl.BlockSpec((rt, d), lambda i, *_: (i, 0)),
                  pl.BlockSpec((rt, TOP_K), lambda i, *_: (i, 0)),
                  pl.BlockSpec((rt, LANES), lambda i, *_: (i, 0)),
                  pl.BlockSpec((1, 1, d), mod_map)],
        out_specs=pl.BlockSpec((rt, d), lambda i, *_: (i, 0)),
        scratch_shapes=[pltpu.VMEM((2, sr, d), F32), pltpu.SemaphoreType.DMA((2,))],
    )
    return pl.pallas_call(
        functools.partial(_combine_kernel, rt),
        grid_spec=grid_spec,
        out_shape=jax.ShapeDtypeStruct((n_rows, d), F32),
        compiler_params=_cparams(("arbitrary",)),
        name="moe_combine",
    )(*tables, eo, xres, pos, gates, g2)


def _route(idx, counts, n_blocks):
    bm = MOE_BLOCK
    rt = ROUTE_TILE
    n = idx.shape[0]
    tile_before = jnp.cumsum(counts, axis=0) - counts
    total = jnp.sum(counts, axis=0)
    padded = (total + bm - 1) // bm * bm
    pad_end = jnp.cumsum(padded)
    pad_start = pad_end - padded
    first_row = tile_before + pad_start[None, :]
    early = first_row % SUBLANES
    size = jnp.where(counts > 0, (counts + early + SUBLANES - 1) // SUBLANES * SUBLANES, 0)
    stage_off = jnp.cumsum(size, axis=1) - size
    experts = idx[:, :TOP_K].reshape(n // rt, rt, TOP_K)
    rank = idx[:, TOP_K:2 * TOP_K]
    onehot = experts[..., None] == lax.broadcasted_iota(jnp.int32, (1, 1, 1, N_EXPERTS), 3)
    lookup = lambda tab: jnp.sum(jnp.where(onehot, tab[:, None, None, :], 0), axis=-1).reshape(n, TOP_K)
    dest = lookup(first_row) + rank
    pos = lookup(stage_off + early) + rank
    tables = tuple(t.reshape(-1).astype(jnp.int32) for t in (first_row - early, size, stage_off))
    block_start = jnp.arange(n_blocks, dtype=jnp.int32) * bm
    block_e = jnp.minimum(jnp.sum(pad_end[None, :] <= block_start[:, None], axis=1), N_EXPERTS - 1)
    n_used = (pad_end[-1] // bm).astype(jnp.int32).reshape(1)
    return dest.astype(jnp.int32), pos.astype(jnp.int32), tables, block_e.astype(jnp.int32), n_used


def _moe(hx, idx, counts, gates, xres, g2, layer, w1, b1, w2, b2, n_out_rows, n_lat, seq):
    n = hx.shape[0]
    n_blocks = -(-(n * TOP_K + N_EXPERTS * (MOE_BLOCK - 1)) // MOE_BLOCK) + 1
    per_tile = TOK_TILE // ROUTE_TILE
    counts = counts[:, :per_tile, :N_EXPERTS].reshape(n // ROUTE_TILE, N_EXPERTS)
    dest, pos, tables, block_e, n_used = _route(idx, counts, n_blocks)
    tok = jnp.broadcast_to(jnp.arange(n, dtype=jnp.int32)[:, None], (n, TOP_K))
    row_tok = jnp.zeros((n_blocks * MOE_BLOCK,), jnp.int32).at[dest.reshape(-1)].set(
        tok.reshape(-1), unique_indices=True)
    eo = _experts(hx, row_tok, n_blocks, block_e + layer * N_EXPERTS, n_used, w1, b1, w2, b2)
    n_rt = n_out_rows // ROUTE_TILE
    tables = tuple(t[:n_rt * N_EXPERTS] for t in tables)
    return _combine(eo, tables, pos[:n_out_rows], xres, gates, g2, n_out_rows, n_lat, seq)


def kernel(x, c, ctx, c_ctx, norm_mix_w, norm_ffn_w, w_ada, b_ada, w_in, mlstm_ig_b, mlstm_fg_b, mlstm_norm_w,
           na_qnorm_w, na_knorm_w, na_rpb, conv_w, conv_b, conv_ln_w, conv_ln_b, w_out, router_w, router_b,
           exp_w1, exp_b1, exp_w2, exp_b2):
    batch, seq, d = x.shape
    ctx_len = ctx.shape[1]
    depth = w_ada.shape[0]
    n_lat = batch * seq
    n_ctx = batch * ctx_len
    n_all = n_lat + n_ctx
    assert seq % TOK_TILE == 0 and n_ctx % TOK_TILE == 0 and seq % ctx_len == 0
    assert (seq // GRID_W) % NA_QROWS == 0 and seq // GRID_W >= NA_KROWS

    mod_rows = -(-(batch + 1) // 8) * 8
    cc = jnp.zeros((mod_rows, d), F32).at[:batch].set(c).at[batch].set(c_ctx)
    mods = _ada(cc, w_ada, b_ada)[:, :batch + 1].reshape(depth, batch + 1, 1, 6, d)
    x_parts = (x.reshape(n_lat, d), ctx.reshape(n_ctx, d))
    rope = _rope_tables(seq)
    n_exp = exp_w1.shape[1]
    ew1 = exp_w1.reshape((depth * n_exp,) + exp_w1.shape[2:])
    eb1 = exp_b1.reshape((depth * n_exp,) + exp_b1.shape[2:])
    ew2 = exp_w2.reshape((depth * n_exp,) + exp_w2.shape[2:])
    eb2 = exp_b2.reshape((depth * n_exp,) + exp_b2.shape[2:])

    for l in range(depth):
        need_ctx = l < depth - 1
        sh1, sc1, g1, sh2, sc2, g2 = [mods[l, :, :, i, :] for i in range(6)]
        pa, kt, g, gt, pb, u = _inproj(x_parts, sh1, sc1, norm_mix_w[l], w_in[l], n_lat, seq)
        a_lat, a_ctx = _mlstm(pa, kt, g, gt, mlstm_ig_b[l], mlstm_fg_b[l], mlstm_norm_w[l], rope,
                              batch, seq, ctx_len, need_ctx)
        b_lat, b_ctx = _na(pb, na_qnorm_w[l], na_knorm_w[l], na_rpb[l], batch, seq, ctx_len, need_ctx)
        c_lat = _conv(u, conv_w[l], conv_b[l], conv_ln_w[l], conv_ln_b[l], 0, batch, seq)
        if need_ctx:
            c_ctx_out = _conv(u, conv_w[l], conv_b[l], conv_ln_w[l], conv_ln_b[l], n_lat // ctx_len, batch, ctx_len)
            a_all, b_all, c_all = (a_lat, a_ctx), (b_lat, b_ctx), (c_lat, c_ctx_out)
            n_rows = n_all
        else:
            a_all, b_all, c_all = (a_lat,), (b_lat,), (c_lat,)
            n_rows = n_lat
        if len(x_parts) != len(a_all):
            join = lambda parts: (jnp.concatenate(parts, axis=0),) if len(parts) > 1 else parts
            a_all, b_all, c_all, x_parts = join(a_all), join(b_all), join(c_all), join(x_parts)
        xmid, hx, idx, gates, counts = _outproj(a_all, b_all, c_all, x_parts, g1, sh2, sc2, norm_ffn_w[l], w_out[l],
                                                router_w[l], router_b[l], n_rows, n_lat, seq)
        xall = _moe(hx, idx, counts, gates, xmid, g2, l, ew1, eb1, ew2, eb2, n_rows, n_lat, seq)
        x_parts = (xall,)
    return x_parts[0][:n_lat].reshape(batch, seq, d)
```

```python
import functools

import numpy as np
import jax
import jax.numpy as jnp
from jax import lax
from jax.experimental import pallas as pl
from jax.experimental.pallas import tpu as pltpu

F32 = jnp.float32
BF16 = jnp.bfloat16
MXU_DTYPE = BF16

GRID_W = 64
HEAD_DIM = 64
MLSTM_HEADS = 4
NA_HEADS = 8
CONV_CH = 256
MLSTM_W = MLSTM_HEADS * HEAD_DIM
NA_W = NA_HEADS * HEAD_DIM
NA_WIN_R = 8
NA_WIN_C = 16
CONV_WIDTH = 31
ROPE_BASE = 10000.0
N_EXPERTS = 32
TOP_K = 4
SWIGLU_LIMIT = 7.0
SWIGLU_ALPHA = 1.702
EPS = 1e-6

A_Q = 0
A_G = 4 * MLSTM_W
B_Q = A_G + 4 * MLSTM_HEADS
C_A = B_Q + 3 * NA_W
IN_COLS = C_A + 2 * CONV_CH

LANES = 128
SUBLANES = 8
MXU_TILE = 256
NEG_BIG = -1e30
LOG2_E = 1.4426950408889634
VMEM_LIMIT = 56 * 1024 * 1024

NA_QROWS = 4
NA_KROWS = NA_QROWS + NA_WIN_R - 1
TOK_TILE = 512
ADA_TILE = 512
MOE_BLOCK = 512
ROUTE_TILE = 256
ROW_CHUNKS = (256, 128, 64, 32, 16, 8)
LONG_RUN = 64
CONV_ROWS = 64
CONV_PAD = 16


def _mm(a, b):
    return jnp.dot(a.astype(MXU_DTYPE), b.astype(MXU_DTYPE), preferred_element_type=F32)


def _mm_nt(a, b):
    return lax.dot_general(a.astype(MXU_DTYPE), b.astype(MXU_DTYPE), (((1,), (1,)), ((), ())),
                           preferred_element_type=F32)


def _cparams(sem):
    return pltpu.CompilerParams(dimension_semantics=sem, vmem_limit_bytes=VMEM_LIMIT)


def _ada_kernel(c_ref, w_ref, b_ref, o_ref):
    cc = c_ref[...]
    s = cc * jax.nn.sigmoid(cc)
    o_ref[0] = _mm(s, w_ref[0]) + b_ref[0]


def _ada(cc, w_ada, b_ada):
    depth, d, n = w_ada.shape
    rows = cc.shape[0]
    tn = ADA_TILE
    return pl.pallas_call(
        _ada_kernel,
        grid=(depth, n // tn),
        in_specs=[pl.BlockSpec((rows, d), lambda l, j: (0, 0)),
                  pl.BlockSpec((1, d, tn), lambda l, j: (l, 0, j)),
                  pl.BlockSpec((1, 1, tn), lambda l, j: (l, 0, j))],
        out_specs=pl.BlockSpec((1, rows, tn), lambda l, j: (l, 0, j)),
        out_shape=jax.ShapeDtypeStruct((depth, rows, n), F32),
        compiler_params=_cparams(("parallel", "parallel")),
        name="ada_mod",
    )(cc, w_ada, b_ada.reshape(depth, 1, n))


def _token_specs(parts, tm, n_lat_tiles):
    cols = parts[0].shape[1]
    if len(parts) == 1:
        return [pl.BlockSpec((tm, cols), lambda i: (i, 0))]
    return [pl.BlockSpec((tm, cols), lambda i: (jnp.minimum(i, n_lat_tiles - 1), 0)),
            pl.BlockSpec((tm, cols), lambda i: (jnp.maximum(i - n_lat_tiles, 0), 0))]


def _token_tile(refs, n_lat_tiles):
    if len(refs) == 1:
        return refs[0][...]
    return jnp.where(pl.program_id(0) < n_lat_tiles, refs[0][...], refs[1][...])


def _inproj_kernel(n_x, n_lat_tiles, *refs):
    x_refs = refs[:n_x]
    (sh_ref, sc_ref, nw_ref, wa_ref, wkt_ref, wg_ref, wgt_ref, wb_ref, wc_ref,
     pa_ref, kt_ref, g_ref, gt_ref, pb_ref, u_ref) = refs[n_x:]
    x = _token_tile(x_refs, n_lat_tiles)
    ms = jnp.mean(x * x, axis=-1, keepdims=True)
    y = x * lax.rsqrt(ms + EPS) * nw_ref[...]
    h = (y * (1.0 + sc_ref[0]) + sh_ref[0]).astype(MXU_DTYPE)
    pa_ref[...] = _mm(h, wa_ref[...]).astype(pa_ref.dtype)
    kt_ref[...] = _mm_nt(wkt_ref[...], h).astype(kt_ref.dtype)
    g_ref[...] = _mm(h, wg_ref[...])
    gt_ref[...] = _mm_nt(wgt_ref[...], h)
    pb_ref[...] = _mm(h, wb_ref[...]).astype(pb_ref.dtype)
    pc = _mm(h, wc_ref[...])
    u_ref[...] = (pc[:, :CONV_CH] * jax.nn.sigmoid(pc[:, CONV_CH:])).astype(u_ref.dtype)


def _inproj(x_parts, shift, scale, norm_w, w_in, n_lat, seq):
    n = sum(p.shape[0] for p in x_parts)
    d = x_parts[0].shape[1]
    tm = TOK_TILE
    n_lat_tiles = n_lat // tm
    per_batch = seq // tm
    n_mod = shift.shape[0]

    def mod_map(i):
        return (jnp.where(i < n_lat_tiles, i // per_batch, n_mod - 1), 0, 0)

    wq, wk, wv, wo = (w_in[:, A_Q + j * MLSTM_W:A_Q + (j + 1) * MLSTM_W] for j in range(4))
    wv = jnp.pad(wv.reshape(d, MLSTM_HEADS, HEAD_DIM), ((0, 0), (0, 0), (0, LANES - HEAD_DIM)))
    wa = jnp.concatenate([wq, wv.reshape(d, MLSTM_HEADS * LANES), wo], axis=1).astype(MXU_DTYPE)
    wkt = wk.T.astype(MXU_DTYPE)
    wg = w_in[:, A_G:B_Q].astype(MXU_DTYPE)
    wb = w_in[:, B_Q:C_A].astype(MXU_DTYPE)
    wc = w_in[:, C_A:IN_COLS].astype(MXU_DTYPE)
    ng = B_Q - A_G
    full = lambda r, c: pl.BlockSpec((r, c), lambda i: (0, 0))
    return pl.pallas_call(
        functools.partial(_inproj_kernel, len(x_parts), n_lat_tiles),
        grid=(n // tm,),
        in_specs=_token_specs(x_parts, tm, n_lat_tiles) + [
                  pl.BlockSpec((1, 1, d), mod_map),
                  pl.BlockSpec((1, 1, d), mod_map),
                  full(1, d),
                  full(d, A_G), full(MLSTM_W, d), full(d, ng), full(ng, d), full(d, 3 * NA_W),
                  full(d, 2 * CONV_CH)],
        out_specs=[pl.BlockSpec((tm, A_G), lambda i: (i, 0)),
                   pl.BlockSpec((MLSTM_W, tm), lambda i: (0, i)),
                   pl.BlockSpec((tm, ng), lambda i: (i, 0)),
                   pl.BlockSpec((ng, tm), lambda i: (0, i)),
                   pl.BlockSpec((tm, 3 * NA_W), lambda i: (i, 0)),
                   pl.BlockSpec((tm, CONV_CH), lambda i: (i, 0))],
        out_shape=[jax.ShapeDtypeStruct((n, A_G), BF16),
                   jax.ShapeDtypeStruct((MLSTM_W, n), BF16),
                   jax.ShapeDtypeStruct((n, ng), F32),
                   jax.ShapeDtypeStruct((ng, n), F32),
                   jax.ShapeDtypeStruct((n, 3 * NA_W), BF16),
                   jax.ShapeDtypeStruct((n, CONV_CH), BF16)],
        compiler_params=_cparams(("parallel",)),
        name="in_proj",
    )(*x_parts, shift, scale, norm_w.reshape(1, d), wa, wkt, wg, wg.T, wb, wc)


def _split3(x):
    hi = x.astype(BF16)
    r1 = x - hi.astype(F32)
    mid = r1.astype(BF16)
    lo = (r1 - mid.astype(F32)).astype(BF16)
    return hi, mid, lo


def _tri_left(tri, x):
    return sum(jnp.dot(tri, p, preferred_element_type=F32) for p in _split3(x))


def _tri_right(x, tri):
    return sum(jnp.dot(p, tri, preferred_element_type=F32) for p in _split3(x))


def _log_sigmoid(x):
    return jnp.minimum(x, 0.0) - jnp.log(1.0 + jnp.exp(-jnp.abs(x)))


def _mlstm_direction(z, q, kt, vx, gcol, grow, c_st, m_st, h_ref, row0, lc, with_output):
    nh = MLSTM_HEADS
    ti = lax.broadcasted_iota(jnp.int32, (lc, lc), 0)
    si = lax.broadcasted_iota(jnp.int32, (lc, lc), 1)
    lower = si <= ti
    upper = si >= ti
    tl = jnp.where(lower, 1.0, 0.0).astype(BF16)
    tu = jnp.where(upper, 1.0, 0.0).astype(BF16)
    f_col = _log_sigmoid(gcol[:, 2 * nh + z * nh:2 * nh + (z + 1) * nh])
    i_row = grow[z * nh:(z + 1) * nh, :]
    f_row = _log_sigmoid(grow[2 * nh + z * nh:2 * nh + (z + 1) * nh, :])
    if z == 0:
        b_col = _tri_left(tl, f_col)
        b_row = _tri_right(f_row, tu)
        b_tot = b_col[lc - 1:lc, :]
        mask = lower
    else:
        b_col = _tri_left(tu, f_col)
        b_row = _tri_right(f_row, tl)
        b_tot = b_col[0:1, :]
        mask = upper
    a_row = i_row - b_row
    a_max = jnp.max(a_row, axis=1, keepdims=True)
    lane256 = lax.broadcasted_iota(jnp.int32, (1, MLSTM_W), 1)
    lane128 = lax.broadcasted_iota(jnp.int32, (1, LANES), 1)
    kt_mx = kt.astype(MXU_DTYPE)
    c_all = c_st[z].astype(MXU_DTYPE)
    for h in range(nh):
        r = z * nh + h
        ar = a_row[h:h + 1, :]
        bl = b_tot[:, h:h + 1]
        m_old = m_st[r:r + 1, 0:1]
        vh = vx[:, h * LANES:(h + 1) * LANES]
        vext = jnp.where(lane128 == HEAD_DIM, jnp.ones((), vh.dtype), vh).astype(MXU_DTYPE)
        if with_output:
            qh = jnp.where(lane256 // HEAD_DIM == h, q, 0.0).astype(MXU_DTYPE)
            am = jnp.where(mask, ar, NEG_BIG)
            g = jnp.maximum(m_old, jnp.max(am, axis=1, keepdims=True))
            s = _mm(qh, kt_mx) * jnp.exp(am - g)
            w_inter = jnp.exp(m_old - g)
            nd = _mm(s, vext) + w_inter * _mm(qh, c_all)
            den = nd[:, HEAD_DIM:HEAD_DIM + 1]
            hval = nd / jnp.maximum(jnp.abs(den), jnp.exp(-(b_col[:, h:h + 1] + g)))
            h_ref[pl.ds(row0, lc), h * LANES:(h + 1) * LANES] = hval
        m_new = bl + jnp.maximum(m_old, a_max[h:h + 1, :])
        w_row = jnp.exp(bl + ar - m_new)
        decay = jnp.exp(bl + m_old - m_new)
        rows = slice(h * HEAD_DIM, (h + 1) * HEAD_DIM)
        c_st[z, rows, :] = decay * c_st[z, rows, :] + _mm(kt[rows, :] * w_row, vext)
        m_st[r:r + 1, :] = jnp.broadcast_to(m_new, (1, LANES))


def _mlstm_kernel(lc, nc, need_ctx,
                  pa_c, kt_c, g_c, gt_c,
                  pa_f, kt_f, g_f, gt_f, cos_f, sin_f, cost_f, sint_f,
                  pa_b, kt_b, g_b, gt_b, cos_b, sin_b, cost_b, sint_b,
                  o_lat, o_ctx, brow_ref, bcol_ref, nw_ref, perm_ref,
                  *rest):
    if need_ctx:
        out_lat, out_ctx, hf, hb, c_st, m_st = rest
    else:
        out_lat, hf, hb, c_st, m_st = rest
        out_ctx = None
    s = pl.program_id(1)
    w = MLSTM_W
    k_scale = HEAD_DIM ** -0.5

    def load(pa, kt_ref, rope_refs):
        q = pa[:, 0:w].astype(F32)
        kt = kt_ref[...].astype(F32)
        vx = pa[:, w:w + MLSTM_HEADS * LANES]
        if rope_refs is not None:
            cos_ref, sin_ref, cost_ref, sint_ref = rope_refs
            q = q * cos_ref[...] + _mm(q, perm_ref[...]) * sin_ref[...]
            blk = HEAD_DIM // 4
            swapped = jnp.concatenate([kt[(i ^ 1) * blk:((i ^ 1) + 1) * blk, :] for i in range(w // blk)], axis=0)
            kt = kt * cost_ref[...] + swapped * sint_ref[...]
        return q, kt * k_scale, vx

    @pl.when(s == 0)
    def _():
        c_st[...] = jnp.zeros(c_st.shape, F32)
        m_st[...] = jnp.zeros(m_st.shape, F32)
        q, kt, vx = load(pa_c, kt_c, None)
        gcol = g_c[...] + brow_ref[...]
        grow = gt_c[...] + bcol_ref[...]
        for z, h_ref in ((0, hf), (1, hb)):
            _mlstm_direction(z, q, kt, vx, gcol, grow, c_st, m_st, h_ref, 0, lc, need_ctx)

    @pl.when(s > 0)
    def _():
        for z, h_ref, refs in ((0, hf, (pa_f, kt_f, g_f, gt_f, (cos_f, sin_f, cost_f, sint_f))),
                               (1, hb, (pa_b, kt_b, g_b, gt_b, (cos_b, sin_b, cost_b, sint_b)))):
            pa, kt_ref, g, gt, rope_refs = refs
            j = s - 1 if z == 0 else nc - s
            row0 = pl.multiple_of(lc + j * lc, lc)
            q, kt, vx = load(pa, kt_ref, rope_refs)
            gcol = g[...] + brow_ref[...]
            grow = gt[...] + bcol_ref[...]
            _mlstm_direction(z, q, kt, vx, gcol, grow, c_st, m_st, h_ref, row0, lc, True)

    @pl.when(s == nc)
    def _():
        lane128 = lax.broadcasted_iota(jnp.int32, (1, LANES), 1)
        mean_w = jnp.where(lax.broadcasted_iota(jnp.int32, (LANES, LANES), 0) < HEAD_DIM,
                           1.0 / HEAD_DIM, 0.0).astype(BF16)
        first = 0 if need_ctx else 1
        for ch in range(first, nc + 1):
            rows = slice(ch * lc, (ch + 1) * lc)
            if ch == 0:
                o_val, dst, dst_rows = o_ctx[...], out_ctx, slice(0, lc)
            else:
                dst_rows = slice((ch - 1) * lc, ch * lc)
                o_val, dst = o_lat[dst_rows, :], out_lat
            o_val = o_val.astype(F32)
            for p in range(MLSTM_HEADS // 2):
                pair = []
                for h in (2 * p, 2 * p + 1):
                    hv = hf[rows, h * LANES:(h + 1) * LANES] + hb[rows, h * LANES:(h + 1) * LANES]
                    hv = jnp.where(lane128 < HEAD_DIM, hv, 0.0)
                    sq = hv * hv
                    sq_hi = sq.astype(BF16)
                    sq_lo = (sq - sq_hi.astype(F32)).astype(BF16)
                    ms = (jnp.dot(sq_hi, mean_w, preferred_element_type=F32)
                          + jnp.dot(sq_lo, mean_w, preferred_element_type=F32))
                    pair.append(hv * lax.rsqrt(ms + EPS))
                packed = jnp.where(lane128 < HEAD_DIM, pair[0], pltpu.roll(pair[1], HEAD_DIM, 1))
                cols = slice(p * LANES, (p + 1) * LANES)
                res = packed * nw_ref[:, cols] * jax.nn.sigmoid(o_val[:, cols])
                dst[dst_rows, cols] = res.astype(dst.dtype)


def _rope_tables(seq):
    half = HEAD_DIM // 2
    quarter = half // 2
    t = jnp.arange(seq, dtype=jnp.int32)
    inv_freq = ROPE_BASE ** (-jnp.arange(quarter, dtype=F32) / quarter)
    parts_c, parts_s = [], []
    for pos in (t // GRID_W, t % GRID_W):
        ang = pos.astype(F32)[:, None] * inv_freq[None, :]
        parts_c += [jnp.cos(ang), jnp.cos(ang)]
        parts_s += [-jnp.sin(ang), jnp.sin(ang)]
    cos = jnp.tile(jnp.concatenate(parts_c, axis=-1), (1, MLSTM_HEADS))
    sin = jnp.tile(jnp.concatenate(parts_s, axis=-1), (1, MLSTM_HEADS))
    j = np.arange(MLSTM_W)
    partner = np.where(j % half < quarter, j + quarter, j - quarter)
    perm = np.zeros((MLSTM_W, MLSTM_W), np.float32)
    perm[partner, j] = 1.0
    return cos, sin, cos.T, sin.T, jnp.asarray(perm, dtype=MXU_DTYPE)


def _mlstm(pa, kt, g, gt, ig_b, fg_b, norm_w, rope, batch, seq, ctx_len, need_ctx):
    lc = ctx_len
    nc = seq // lc
    n_lat = batch * seq
    cos, sin, cos_t, sin_t, perm = rope
    bias = jnp.concatenate([ig_b.reshape(-1), fg_b.reshape(-1)]).astype(F32)
    ng = bias.shape[0]
    lat_blocks = n_lat // lc

    def fwd(b, s):
        return b * nc + jnp.maximum(s - 1, 0)

    def bwd(b, s):
        return b * nc + nc - jnp.maximum(s, 1)

    def fwd_c(b, s):
        return jnp.maximum(s - 1, 0)

    def bwd_c(b, s):
        return nc - jnp.maximum(s, 1)

    def lat_specs(chunk, chunk_c):
        return [pl.BlockSpec((lc, A_G), lambda b, s: (chunk(b, s), 0)),
                pl.BlockSpec((MLSTM_W, lc), lambda b, s: (0, chunk(b, s))),
                pl.BlockSpec((lc, ng), lambda b, s: (chunk(b, s), 0)),
                pl.BlockSpec((ng, lc), lambda b, s: (0, chunk(b, s))),
                pl.BlockSpec((lc, MLSTM_W), lambda b, s: (chunk_c(b, s), 0)),
                pl.BlockSpec((lc, MLSTM_W), lambda b, s: (chunk_c(b, s), 0)),
                pl.BlockSpec((MLSTM_W, lc), lambda b, s: (0, chunk_c(b, s))),
                pl.BlockSpec((MLSTM_W, lc), lambda b, s: (0, chunk_c(b, s)))]

    in_specs = ([pl.BlockSpec((lc, A_G), lambda b, s: (lat_blocks + b, 0)),
                 pl.BlockSpec((MLSTM_W, lc), lambda b, s: (0, lat_blocks + b)),
                 pl.BlockSpec((lc, ng), lambda b, s: (lat_blocks + b, 0)),
                 pl.BlockSpec((ng, lc), lambda b, s: (0, lat_blocks + b))]
                + lat_specs(fwd, fwd_c) + lat_specs(bwd, bwd_c)
                + [pl.BlockSpec((seq, MLSTM_W), lambda b, s: (b, 3)),
                   pl.BlockSpec((lc, MLSTM_W), lambda b, s: (lat_blocks + b, 3)),
                   pl.BlockSpec((1, ng), lambda b, s: (0, 0)),
                   pl.BlockSpec((ng, 1), lambda b, s: (0, 0)),
                   pl.BlockSpec((1, MLSTM_W), lambda b, s: (0, 0)),
                   pl.BlockSpec((MLSTM_W, MLSTM_W), lambda b, s: (0, 0))])
    out_specs = [pl.BlockSpec((seq, MLSTM_W), lambda b, s: (b, 0))]
    out_shape = [jax.ShapeDtypeStruct((n_lat, MLSTM_W), BF16)]
    if need_ctx:
        out_specs.append(pl.BlockSpec((lc, MLSTM_W), lambda b, s: (b, 0)))
        out_shape.append(jax.ShapeDtypeStruct((batch * ctx_len, MLSTM_W), BF16))
    t_all = ctx_len + seq
    outs = pl.pallas_call(
        functools.partial(_mlstm_kernel, lc, nc, need_ctx),
        grid=(batch, nc + 1),
        in_specs=in_specs,
        out_specs=out_specs,
        out_shape=out_shape,
        scratch_shapes=[pltpu.VMEM((t_all, MLSTM_HEADS * LANES), F32),
                        pltpu.VMEM((t_all, MLSTM_HEADS * LANES), F32),
                        pltpu.VMEM((2, MLSTM_W, LANES), F32),
                        pltpu.VMEM((2 * MLSTM_HEADS, LANES), F32)],
        compiler_params=_cparams(("parallel", "arbitrary")),
        name="mlstm",
    )(pa, kt, g, gt, pa, kt, g, gt, cos, sin, cos_t, sin_t, pa, kt, g, gt, cos, sin, cos_t, sin_t, pa, pa,
      bias.reshape(1, ng), bias.reshape(ng, 1), norm_w.reshape(1, MLSTM_W).astype(F32), perm)
    return (outs[0], outs[1]) if need_ctx else (outs[0], None)


def _na_patterns(n_rows):
    kr = min(NA_WIN_R, n_rows)
    n_dr = 2 * NA_WIN_R - 1
    pats, pat_ids, bases = [], [], []
    for gi in range(n_rows // NA_QROWS):
        base = int(np.clip(NA_QROWS * gi - NA_WIN_R // 2, 0, n_rows - NA_KROWS))
        dr = np.full((NA_QROWS, NA_KROWS), n_dr, np.int32)
        for qr in range(NA_QROWS):
            r = NA_QROWS * gi + qr
            r0 = int(np.clip(r - kr // 2, 0, n_rows - kr))
            for kj in range(NA_KROWS):
                if r0 <= base + kj < r0 + kr:
                    dr[qr, kj] = base + kj - r + NA_WIN_R - 1
        for pi, p in enumerate(pats):
            if np.array_equal(p, dr):
                pat_ids.append(pi)
                break
        else:
            pat_ids.append(len(pats))
            pats.append(dr)
        bases.append(base)
    return tuple(pat_ids), tuple(bases), np.stack(pats)


def _na_bias_table(rpb, row_idx):
    heads = rpb.shape[0]
    col = np.arange(GRID_W)
    col_start = np.clip(col - NA_WIN_C // 2, 0, GRID_W - NA_WIN_C)
    in_win = (col[None, :] >= col_start[:, None]) & (col[None, :] < col_start[:, None] + NA_WIN_C)
    dc = np.clip(col[None, :] - col[:, None] + NA_WIN_C - 1, 0, 2 * NA_WIN_C - 2)
    onehot = (dc[None] == np.arange(2 * NA_WIN_C - 1)[:, None, None]).astype(np.float32)
    planes = jnp.einsum('hdc,cqk->hdqk', rpb, onehot, precision=lax.Precision.HIGHEST)
    planes = jnp.where(in_win[None, None], planes, NEG_BIG)
    planes = jnp.concatenate([planes, jnp.full((heads, 1, GRID_W, GRID_W), NEG_BIG, F32)], axis=1)
    npat = row_idx.shape[0]
    tab = planes[:, row_idx.reshape(-1)].reshape(heads, npat, NA_QROWS, NA_KROWS, GRID_W, GRID_W)
    return tab.transpose(0, 1, 2, 4, 3, 5).reshape(heads, npat, NA_QROWS * GRID_W, NA_KROWS * GRID_W)


def _na_kernel(pat_ids, bases, need_ctx, q_ref, k_ref, v_ref, kc_ref, vc_ref, *rest):
    if need_ctx:
        qc_ref, bias_ref, qw_ref, kw_ref, out_ref, outc_ref, kn_s, kcn_s = rest
    else:
        bias_ref, qw_ref, kw_ref, out_ref, kn_s, kcn_s = rest
    lane = lax.broadcasted_iota(jnp.int32, (1, LANES), 1)
    low = lane < HEAD_DIM
    inv_d = 1.0 / HEAD_DIM

    def rmsn(x, w):
        x2 = x * x
        s0 = jnp.sum(jnp.where(low, x2, 0.0), axis=-1, keepdims=True)
        s1 = jnp.sum(jnp.where(low, 0.0, x2), axis=-1, keepdims=True)
        r = jnp.where(low, lax.rsqrt(s0 * inv_d + EPS), lax.rsqrt(s1 * inv_d + EPS))
        return x * r * w

    qw = qw_ref[...]
    kn_s[...] = rmsn(k_ref[...].astype(F32), kw_ref[...]).astype(kn_s.dtype)
    kcn_s[...] = rmsn(kc_ref[...].astype(F32), kw_ref[...]).astype(kcn_s.dtype)
    kcn = kcn_s[...]
    vc = vc_ref[...]
    scale = HEAD_DIM ** -0.5 * LOG2_E
    nq = NA_QROWS * GRID_W
    nk = NA_KROWS * GRID_W

    def attend(qn, parts):
        outs = []
        for hh in range(2):
            qh = jnp.where(low if hh == 0 else jnp.logical_not(low), qn, 0.0).astype(MXU_DTYPE)
            scores = []
            for keys, _, bias in parts:
                sc = _mm_nt(qh, keys)
                if bias is not None:
                    sc = sc + bias[hh]
                scores.append(sc)
            m = scores[0].max(axis=-1, keepdims=True)
            for sc in scores[1:]:
                m = jnp.maximum(m, sc.max(axis=-1, keepdims=True))
            acc = None
            den = None
            for sc, (_, vals, _) in zip(scores, parts):
                p = jnp.exp2(sc - m)
                d = jnp.sum(p, axis=-1, keepdims=True)
                o = _mm(p, vals)
                acc = o if acc is None else acc + o
                den = d if den is None else den + d
            outs.append(acc / den)
        return jnp.where(low, outs[0], outs[1])

    for gi, (pid, base) in enumerate(zip(pat_ids, bases)):
        qn = rmsn(q_ref[gi * nq:(gi + 1) * nq, :].astype(F32), qw) * scale
        kwin = kn_s[base * GRID_W:base * GRID_W + nk, :]
        vwin = v_ref[base * GRID_W:base * GRID_W + nk, :]
        bias = (bias_ref[0, pid], bias_ref[1, pid])
        res = attend(qn, [(kwin, vwin, bias), (kcn, vc, None)])
        out_ref[gi * nq:(gi + 1) * nq, :] = res.astype(out_ref.dtype)

    if need_ctx:
        qn = rmsn(qc_ref[...].astype(F32), qw) * scale
        outc_ref[...] = attend(qn, [(kcn, vc, None)]).astype(outc_ref.dtype)


def _na(pb, qn_w, kn_w, rpb, batch, seq, ctx_len, need_ctx):
    n_rows = seq // GRID_W
    pat_ids, bases, row_idx = _na_patterns(n_rows)
    npat = row_idx.shape[0]
    nq, nk = NA_QROWS * GRID_W, NA_KROWS * GRID_W
    bias = _na_bias_table(rpb.astype(F32) * LOG2_E, row_idx)
    n_lat = batch * seq
    pairs = NA_HEADS // 2
    qoff, koff, voff = 0, pairs, 2 * pairs

    in_specs = [pl.BlockSpec((seq, LANES), lambda p, b: (b, qoff + p)),
                pl.BlockSpec((seq, LANES), lambda p, b: (b, koff + p)),
                pl.BlockSpec((seq, LANES), lambda p, b: (b, voff + p)),
                pl.BlockSpec((ctx_len, LANES), lambda p, b: (n_lat // ctx_len + b, koff + p)),
                pl.BlockSpec((ctx_len, LANES), lambda p, b: (n_lat // ctx_len + b, voff + p))]
    args = [pb, pb, pb, pb, pb]
    if need_ctx:
        in_specs.append(pl.BlockSpec((ctx_len, LANES), lambda p, b: (n_lat // ctx_len + b, qoff + p)))
        args.append(pb)
    in_specs += [pl.BlockSpec((2, npat, nq, nk), lambda p, b: (p, 0, 0, 0)),
                 pl.BlockSpec((1, LANES), lambda p, b: (0, 0)),
                 pl.BlockSpec((1, LANES), lambda p, b: (0, 0))]
    args += [bias, jnp.tile(qn_w.astype(F32), 2).reshape(1, LANES), jnp.tile(kn_w.astype(F32), 2).reshape(1, LANES)]
    out_specs = [pl.BlockSpec((seq, LANES), lambda p, b: (b, p))]
    out_shape = [jax.ShapeDtypeStruct((n_lat, NA_W), BF16)]
    if need_ctx:
        out_specs.append(pl.BlockSpec((ctx_len, LANES), lambda p, b: (b, p)))
        out_shape.append(jax.ShapeDtypeStruct((batch * ctx_len, NA_W), BF16))
    outs = pl.pallas_call(
        functools.partial(_na_kernel, pat_ids, bases, need_ctx),
        grid=(pairs, batch),
        in_specs=in_specs,
        out_specs=out_specs,
        out_shape=out_shape,
        scratch_shapes=[pltpu.VMEM((seq, LANES), MXU_DTYPE), pltpu.VMEM((ctx_len, LANES), MXU_DTYPE)],
        compiler_params=_cparams(("parallel", "parallel")),
        name="na_attn",
    )(*args)
    return (outs[0], outs[1]) if need_ctx else (outs[0], None)


def _conv_kernel(t_len, u_ref, w_ref, cb_ref, lw_ref, lb_ref, o_ref, pad_s):
    zeros = jnp.zeros((CONV_PAD, CONV_CH), F32)
    pad_s[0:CONV_PAD, :] = zeros
    pad_s[CONV_PAD + t_len:2 * CONV_PAD + t_len, :] = zeros
    pad_s[CONV_PAD:CONV_PAD + t_len, :] = u_ref[...].astype(F32)
    shift = CONV_PAD - CONV_WIDTH // 2

    def body(c, carry):
        r0 = pl.multiple_of(c * CONV_ROWS, CONV_ROWS)
        n_win = CONV_ROWS + 2 * CONV_PAD
        win = pad_s[pl.ds(r0, n_win), :]
        rot = [win] + [pltpu.roll(win, n_win - ph, 0) for ph in range(1, SUBLANES)]
        acc = jnp.zeros((CONV_ROWS, CONV_CH), F32) + cb_ref[...]
        for j in range(CONV_WIDTH):
            ph, al = (j + shift) % SUBLANES, (j + shift) // SUBLANES * SUBLANES
            acc = acc + rot[ph][al:al + CONV_ROWS, :] * w_ref[j:j + 1, :]
        mean = jnp.mean(acc, axis=-1, keepdims=True)
        xc = acc - mean
        var = jnp.mean(xc * xc, axis=-1, keepdims=True)
        y = xc * lax.rsqrt(var + EPS) * lw_ref[...] + lb_ref[...]
        o_ref[pl.ds(r0, CONV_ROWS), :] = (y * jax.nn.sigmoid(y)).astype(o_ref.dtype)
        return carry

    lax.fori_loop(0, t_len // CONV_ROWS, body, 0)


def _conv(u, conv_w, conv_b, ln_w, ln_b, first_block, n_seq, t_len):
    row = lambda a: a.reshape(1, CONV_CH).astype(F32)
    return pl.pallas_call(
        functools.partial(_conv_kernel, t_len),
        grid=(n_seq,),
        in_specs=[pl.BlockSpec((t_len, CONV_CH), lambda b: (first_block + b, 0)),
                  pl.BlockSpec((CONV_WIDTH, CONV_CH), lambda b: (0, 0)),
                  pl.BlockSpec((1, CONV_CH), lambda b: (0, 0)),
                  pl.BlockSpec((1, CONV_CH), lambda b: (0, 0)),
                  pl.BlockSpec((1, CONV_CH), lambda b: (0, 0))],
        out_specs=pl.BlockSpec((t_len, CONV_CH), lambda b: (b, 0)),
        out_shape=jax.ShapeDtypeStruct((n_seq * t_len, CONV_CH), BF16),
        scratch_shapes=[pltpu.VMEM((t_len + 2 * CONV_PAD, CONV_CH), F32)],
        compiler_params=_cparams(("parallel",)),
        name="conv_module",
    )(u, conv_w.astype(F32), row(conv_b), row(ln_w), row(ln_b))


def _outproj_kernel(rt, n_src, n_lat_tiles, *refs):
    a_refs, b_refs, c_refs, x_refs = (refs[j * n_src:(j + 1) * n_src] for j in range(4))
    (g1_ref, sh_ref, sc_ref, nw_ref, wa_ref, wb_ref, wc_ref, rwh_ref, rwl_ref, rb_ref,
     xo_ref, hx_ref, idx_ref, gate_ref, cnt_ref) = refs[4 * n_src:]
    tile = lambda parts: _token_tile(parts, n_lat_tiles)
    mix = _mm(tile(a_refs), wa_ref[...]) + _mm(tile(b_refs), wb_ref[...]) + _mm(tile(c_refs), wc_ref[...])
    xn = tile(x_refs) + g1_ref[0] * mix
    xo_ref[...] = xn
    ms = jnp.mean(xn * xn, axis=-1, keepdims=True)
    hx = xn * lax.rsqrt(ms + EPS) * nw_ref[...] * (1.0 + sc_ref[0]) + sh_ref[0]
    hx_ref[...] = hx
    h_hi = hx.astype(BF16)
    h_lo = (hx - h_hi.astype(F32)).astype(BF16)
    logits = (jnp.dot(h_hi, rwh_ref[...], preferred_element_type=F32)
              + jnp.dot(h_lo, rwh_ref[...], preferred_element_type=F32)
              + jnp.dot(h_hi, rwl_ref[...], preferred_element_type=F32)) + rb_ref[...]
    lane = lax.broadcasted_iota(jnp.int32, logits.shape, 1)
    idx_out = jnp.zeros(logits.shape, jnp.int32)
    val_out = jnp.zeros(logits.shape, F32)
    top = None
    den = None
    sels = []
    for kk in range(TOP_K):
        m = jnp.max(logits, axis=-1, keepdims=True)
        sel = jnp.min(jnp.where(logits == m, lane, LANES), axis=-1, keepdims=True)
        if kk == 0:
            top = m
        e = jnp.exp(m - top)
        den = e if den is None else den + e
        idx_out = jnp.where(lane == kk, sel, idx_out)
        val_out = jnp.where(lane == kk, e, val_out)
        logits = jnp.where(lane == sel, -jnp.inf, logits)
        sels.append(sel)
    gate_ref[...] = val_out / den
    tm = logits.shape[0]
    chosen = jnp.where(logits == -jnp.inf, 1.0, 0.0)
    ti = lax.broadcasted_iota(jnp.int32, (tm, tm), 0)
    si = lax.broadcasted_iota(jnp.int32, (tm, tm), 1)
    earlier = jnp.where(jnp.logical_and(si < ti, si // rt == ti // rt), 1.0, 0.0).astype(BF16)
    before = jnp.dot(earlier, chosen.astype(BF16), preferred_element_type=F32)
    for kk in range(TOP_K):
        rank = jnp.sum(jnp.where(lane == sels[kk], before, 0.0), axis=-1, keepdims=True)
        idx_out = jnp.where(lane == TOP_K + kk, rank.astype(jnp.int32), idx_out)
    idx_ref[...] = idx_out
    row = lax.broadcasted_iota(jnp.int32, cnt_ref.shape[1:], 0)
    cnt = jnp.zeros(cnt_ref.shape[1:], F32)
    for sub in range(tm // rt):
        cnt = jnp.where(row == sub, jnp.sum(chosen[sub * rt:(sub + 1) * rt], axis=0, keepdims=True), cnt)
    cnt_ref[0] = cnt.astype(jnp.int32)


def _outproj(a, b, c, x_parts, g1, sh2, sc2, norm_w, w_out, router_w, router_b, n_rows, n_lat, seq):
    d = x_parts[0].shape[1]
    n_src = len(x_parts)
    assert len(a) == len(b) == len(c) == n_src
    tm = TOK_TILE
    n_lat_tiles = n_lat // tm
    per_batch = seq // tm
    n_mod = g1.shape[0]

    def mod_map(i):
        return (jnp.where(i < n_lat_tiles, i // per_batch, n_mod - 1), 0, 0)

    wa = w_out[0:MLSTM_W].astype(MXU_DTYPE)
    wb = w_out[MLSTM_W:MLSTM_W + NA_W].astype(MXU_DTYPE)
    wc = w_out[MLSTM_W + NA_W:].astype(MXU_DTYPE)
    rw = jnp.zeros((d, LANES), F32).at[:, :N_EXPERTS].set(router_w.astype(F32))
    rw_hi = rw.astype(BF16)
    rw_lo = (rw - rw_hi.astype(F32)).astype(BF16)
    rb = jnp.full((1, LANES), NEG_BIG, F32).at[0, :N_EXPERTS].set(router_b.astype(F32))
    full = lambda r, cc: pl.BlockSpec((r, cc), lambda i: (0, 0))
    tile = lambda cc: pl.BlockSpec((tm, cc), lambda i: (i, 0))
    return pl.pallas_call(
        functools.partial(_outproj_kernel, ROUTE_TILE, n_src, n_lat_tiles),
        grid=(n_rows // tm,),
        in_specs=[spec for parts in (a, b, c, x_parts) for spec in _token_specs(parts, tm, n_lat_tiles)] + [
                  pl.BlockSpec((1, 1, d), mod_map), pl.BlockSpec((1, 1, d), mod_map),
                  pl.BlockSpec((1, 1, d), mod_map), full(1, d),
                  full(MLSTM_W, d), full(NA_W, d), full(CONV_CH, d), full(d, LANES), full(d, LANES),
                  full(1, LANES)],
        out_specs=[tile(d), tile(d), tile(LANES), tile(LANES),
                   pl.BlockSpec((1, SUBLANES, LANES), lambda i: (i, 0, 0))],
        out_shape=[jax.ShapeDtypeStruct((n_rows, d), F32), jax.ShapeDtypeStruct((n_rows, d), F32),
                   jax.ShapeDtypeStruct((n_rows, LANES), jnp.int32), jax.ShapeDtypeStruct((n_rows, LANES), F32),
                   jax.ShapeDtypeStruct((n_rows // tm, SUBLANES, LANES), jnp.int32)],
        compiler_params=_cparams(("parallel",)),
        name="out_proj",
    )(*a, *b, *c, *x_parts, g1, sh2, sc2, norm_w.reshape(1, d).astype(F32), wa, wb, wc, rw_hi, rw_lo, rb)


def _dispatch_kernel(tm, bm, lo_ref, hi_ref, nu_ref, dest_ref, hx_ref, xs_out, zeros, sem, zsem):
    n_blocks = xs_out.shape[0] // bm

    def pad_copies():
        for e in range(N_EXPERTS):
            lo, hi = lo_ref[e], hi_ref[e]
            head = jnp.minimum((SUBLANES - lo % SUBLANES) % SUBLANES, hi - lo)
            for j in range(SUBLANES - 1):
                yield j < head, pltpu.make_async_copy(zeros.at[pl.ds(0, 1), :], xs_out.at[pl.ds(lo + j, 1), :], zsem)
            rest = hi - lo - head
            for chunk in ROW_CHUNKS:
                at = pl.multiple_of(lo + head + (rest & ~(2 * chunk - 1)), SUBLANES)
                yield (rest & chunk) != 0, pltpu.make_async_copy(
                    zeros.at[pl.ds(0, chunk), :], xs_out.at[pl.ds(at, chunk), :], zsem)

    def block_copy(b):
        return pltpu.make_async_copy(zeros, xs_out.at[pl.ds(pl.multiple_of(b * bm, bm), bm), :], zsem)

    def start_block(b, carry):
        block_copy(b).start()
        return carry

    def wait_block(b, carry):
        block_copy(b).wait()
        return carry

    @pl.when(pl.program_id(0) == 0)
    def _():
        zeros[...] = jnp.zeros(zeros.shape, zeros.dtype)
        for needed, copy in pad_copies():
            pl.when(needed)(copy.start)
        lax.fori_loop(nu_ref[0], n_blocks, start_block, 0)
        for needed, copy in pad_copies():
            pl.when(needed)(copy.wait)
        lax.fori_loop(nu_ref[0], n_blocks, wait_block, 0)

    def body(r, carry):
        for kk in range(TOP_K):
            dst = dest_ref[0, 0, kk * tm + r]
            pltpu.make_async_copy(hx_ref.at[pl.ds(r, 1), :], xs_out.at[pl.ds(dst, 1), :], sem).start()
        return carry

    lax.fori_loop(0, tm, body, 0)
    for kk in range(TOP_K):
        pltpu.make_async_copy(hx_ref, xs_out.at[pl.ds(0, tm), :], sem).wait()


def _dispatch(hx, dest_tiles, pad_lo, pad_hi, n_used, n_blocks, tm):
    n, d = hx.shape
    bm = MOE_BLOCK
    grid_spec = pltpu.PrefetchScalarGridSpec(
        num_scalar_prefetch=3,
        grid=(n // tm,),
        in_specs=[pl.BlockSpec((1, 1, TOP_K * tm), lambda i, *_: (i, 0, 0), memory_space=pltpu.SMEM),
                  pl.BlockSpec((tm, d), lambda i, *_: (i, 0))],
        out_specs=pl.BlockSpec(memory_space=pl.ANY),
        scratch_shapes=[pltpu.VMEM((bm, d), hx.dtype), pltpu.SemaphoreType.DMA, pltpu.SemaphoreType.DMA],
    )
    return pl.pallas_call(
        functools.partial(_dispatch_kernel, tm, bm),
        grid_spec=grid_spec,
        out_shape=jax.ShapeDtypeStruct((n_blocks * bm, d), hx.dtype),
        compiler_params=_cparams(("arbitrary",)),
        name="moe_dispatch",
    )(pad_lo, pad_hi, n_used, dest_tiles, hx)


def _expert_kernel(be_ref, nu_ref, x_ref, w1_ref, b1_ref, w2_ref, b2_ref, o_ref, w1_s, w2_s):
    i = pl.program_id(0)
    de = w2_ref.shape[1]

    @pl.when(jnp.logical_or(i == 0, be_ref[i] != be_ref[jnp.maximum(i - 1, 0)]))
    def _():
        w1_s[...] = w1_ref[0].astype(w1_s.dtype)
        w2_s[...] = w2_ref[0].astype(w2_s.dtype)

    @pl.when(i < nu_ref[0])
    def _():
        h = _mm(x_ref[...], w1_s[...]) + b1_ref[0]
        glu = jnp.minimum(h[:, :de], SWIGLU_LIMIT)
        lin = jnp.clip(h[:, de:], -SWIGLU_LIMIT, SWIGLU_LIMIT)
        act = (lin + 1.0) * glu * jax.nn.sigmoid(SWIGLU_ALPHA * glu)
        o_ref[...] = _mm(act, w2_s[...]) + b2_ref[0]

    @pl.when(i >= nu_ref[0])
    def _():
        o_ref[...] = jnp.zeros(o_ref.shape, o_ref.dtype)


def _experts(xs, n_blocks, block_e, n_used, w1, b1, w2, b2):
    d = xs.shape[1]
    ne, _, two_de = w1.shape
    de = w2.shape[1]
    bm = MOE_BLOCK
    grid_spec = pltpu.PrefetchScalarGridSpec(
        num_scalar_prefetch=2,
        grid=(n_blocks,),
        in_specs=[pl.BlockSpec((bm, d), lambda i, *_: (i, 0)),
                  pl.BlockSpec((1, d, two_de), lambda i, be, *_: (be[i], 0, 0)),
                  pl.BlockSpec((1, 1, two_de), lambda i, be, *_: (be[i], 0, 0)),
                  pl.BlockSpec((1, de, d), lambda i, be, *_: (be[i], 0, 0)),
                  pl.BlockSpec((1, 1, d), lambda i, be, *_: (be[i], 0, 0))],
        out_specs=pl.BlockSpec((bm, d), lambda i, *_: (i, 0)),
        scratch_shapes=[pltpu.VMEM((d, two_de), MXU_DTYPE), pltpu.VMEM((de, d), MXU_DTYPE)],
    )
    return pl.pallas_call(
        _expert_kernel,
        grid_spec=grid_spec,
        out_shape=jax.ShapeDtypeStruct((n_blocks * bm, d), F32),
        compiler_params=_cparams(("arbitrary",)),
        name="moe_experts",
    )(block_e, n_used, xs, w1.astype(F32), b1.reshape(ne, 1, two_de).astype(F32), w2.astype(F32),
      b2.reshape(ne, 1, d).astype(F32))


def _combine_kernel(rt, src_ref, size_ref, soff_ref, eo_hbm, x_ref, pos_ref, post_ref, gatet_ref, g2_ref, xo_ref,
                    stage, sem):
    i = pl.program_id(0)
    slot = i % 2
    sr = stage.shape[1]

    def run_copies(tile, to_slot, op):
        for e in range(N_EXPERTS):
            base = tile * N_EXPERTS + e
            src, size, dst = src_ref[base], size_ref[base], soff_ref[base]

            def chunk_ops(chunks):
                for chunk in chunks:
                    done = size & ~(2 * chunk - 1)
                    copy = pltpu.make_async_copy(
                        eo_hbm.at[pl.ds(pl.multiple_of(src + done, SUBLANES), chunk), :],
                        stage.at[to_slot, pl.ds(pl.multiple_of(dst + done, SUBLANES), chunk), :],
                        sem.at[to_slot])
                    pl.when((size & chunk) != 0)(getattr(copy, op))

            pl.when(size >= LONG_RUN)(functools.partial(chunk_ops, [c for c in ROW_CHUNKS if c >= LONG_RUN]))
            chunk_ops([c for c in ROW_CHUNKS if c < LONG_RUN])

    @pl.when(i == 0)
    def _():
        stage[...] = jnp.zeros(stage.shape, stage.dtype)
        run_copies(0, 0, "start")

    @pl.when(i + 1 < pl.num_programs(0))
    def _():
        run_copies(i + 1, 1 - slot, "start")

    run_copies(i, slot, "wait")

    last = i * N_EXPERTS + N_EXPERTS - 1
    n_staged = soff_ref[last] + size_ref[last]
    pos_t = post_ref[0]
    gate_t = gatet_ref[0]
    row_t = lax.broadcasted_iota(jnp.int32, (sr, rt), 0)
    gate_sel = jnp.zeros((sr, rt), F32)
    for kk in range(TOP_K):
        gate_sel = jnp.where(row_t == pos_t[kk:kk + 1, :], gate_t[kk:kk + 1, :], gate_sel)
    row_gate = jnp.sum(gate_sel, axis=1, keepdims=True)
    row = lax.broadcasted_iota(jnp.int32, (sr, 1), 0)
    staged = jnp.where(row < n_staged, stage[slot], 0.0)
    staged = (staged * row_gate).astype(MXU_DTYPE)
    col = lax.broadcasted_iota(jnp.int32, (rt, sr), 1)
    pos = pos_ref[...]
    picked = col == pos[:, 0:1]
    for kk in range(1, TOP_K):
        picked = jnp.logical_or(picked, col == pos[:, kk:kk + 1])
    sel = jnp.where(picked, 1.0, 0.0).astype(MXU_DTYPE)
    y = jnp.dot(sel, staged, preferred_element_type=F32)
    xo_ref[...] = x_ref[...] + g2_ref[0] * y


def _combine(eo, tables, pos, xres, gates, g2, n_rows, n_lat, seq):
    d = xres.shape[1]
    rt = ROUTE_TILE
    n_lat_tiles = n_lat // rt
    per_batch = seq // rt
    n_mod = g2.shape[0]
    sr = TOP_K * rt + N_EXPERTS * 2 * (SUBLANES - 1)
    sr = -(-sr // MXU_TILE) * MXU_TILE

    def mod_map(i, *_):
        return (jnp.where(i < n_lat_tiles, i // per_batch, n_mod - 1), 0, 0)

    def by_tile(a):
        return a[:n_rows].reshape(n_rows // rt, rt, TOP_K).transpose(0, 2, 1)

    grid_spec = pltpu.PrefetchScalarGridSpec(
        num_scalar_prefetch=3,
        grid=(n_rows // rt,),
        in_specs=[pl.BlockSpec(memory_space=pl.ANY),
                  pl.BlockSpec((rt, d), lambda i, *_: (i, 0)),
                  pl.BlockSpec((rt, TOP_K), lambda i, *_: (i, 0)),
                  pl.BlockSpec((1, TOP_K, rt), lambda i, *_: (i, 0, 0)),
                  pl.BlockSpec((1, TOP_K, rt), lambda i, *_: (i, 0, 0)),
                  pl.BlockSpec((1, 1, d), mod_map)],
        out_specs=pl.BlockSpec((rt, d), lambda i, *_: (i, 0)),
        scratch_shapes=[pltpu.VMEM((2, sr, d), F32), pltpu.SemaphoreType.DMA((2,))],
    )
    return pl.pallas_call(
        functools.partial(_combine_kernel, rt),
        grid_spec=grid_spec,
        out_shape=jax.ShapeDtypeStruct((n_rows, d), F32),
        compiler_params=_cparams(("arbitrary",)),
        name="moe_combine",
    )(*tables, eo, xres, pos, by_tile(pos), by_tile(gates[:, :TOP_K]), g2)


def _dest_tiles(dest, tm):
    n = dest.shape[0]
    return dest.reshape(n // tm, tm, TOP_K).transpose(0, 2, 1).reshape(n // tm, 1, TOP_K * tm)


def _route(idx, counts, n_blocks):
    bm = MOE_BLOCK
    rt = ROUTE_TILE
    n = idx.shape[0]
    tile_before = jnp.cumsum(counts, axis=0) - counts
    total = jnp.sum(counts, axis=0)
    padded = (total + bm - 1) // bm * bm
    pad_end = jnp.cumsum(padded)
    pad_start = pad_end - padded
    first_row = tile_before + pad_start[None, :]
    early = first_row % SUBLANES
    size = jnp.where(counts > 0, (counts + early + SUBLANES - 1) // SUBLANES * SUBLANES, 0)
    stage_off = jnp.cumsum(size, axis=1) - size
    experts = idx[:, :TOP_K].reshape(n // rt, rt, TOP_K)
    rank = idx[:, TOP_K:2 * TOP_K]
    onehot = experts[..., None] == lax.broadcasted_iota(jnp.int32, (1, 1, 1, N_EXPERTS), 3)
    lookup = lambda tab: jnp.sum(jnp.where(onehot, tab[:, None, None, :], 0), axis=-1).reshape(n, TOP_K)
    dest = lookup(first_row) + rank
    pos = lookup(stage_off + early) + rank
    tables = tuple(t.reshape(-1).astype(jnp.int32) for t in (first_row - early, size, stage_off))
    block_start = jnp.arange(n_blocks, dtype=jnp.int32) * bm
    block_e = jnp.minimum(jnp.sum(pad_end[None, :] <= block_start[:, None], axis=1), N_EXPERTS - 1)
    n_used = (pad_end[-1] // bm).astype(jnp.int32).reshape(1)
    padding = ((pad_start + total).astype(jnp.int32), pad_end.astype(jnp.int32))
    return dest.astype(jnp.int32), pos.astype(jnp.int32), tables, block_e.astype(jnp.int32), n_used, padding


def _moe(hx, idx, counts, gates, xres, g2, layer, w1, b1, w2, b2, n_out_rows, n_lat, seq):
    n = hx.shape[0]
    n_blocks = -(-(n * TOP_K + N_EXPERTS * (MOE_BLOCK - 1)) // MOE_BLOCK) + 1
    per_tile = TOK_TILE // ROUTE_TILE
    counts = counts[:, :per_tile, :N_EXPERTS].reshape(n // ROUTE_TILE, N_EXPERTS)
    dest, pos, tables, block_e, n_used, (pad_lo, pad_hi) = _route(idx, counts, n_blocks)
    xs = _dispatch(hx, _dest_tiles(dest, TOK_TILE), pad_lo, pad_hi, n_used, n_blocks, TOK_TILE)
    eo = _experts(xs, n_blocks, block_e + layer * N_EXPERTS, n_used, w1, b1, w2, b2)
    n_rt = n_out_rows // ROUTE_TILE
    tables = tuple(t[:n_rt * N_EXPERTS] for t in tables)
    return _combine(eo, tables, pos[:n_out_rows], xres, gates, g2, n_out_rows, n_lat, seq)


def kernel(x, c, ctx, c_ctx, norm_mix_w, norm_ffn_w, w_ada, b_ada, w_in, mlstm_ig_b, mlstm_fg_b, mlstm_norm_w,
           na_qnorm_w, na_knorm_w, na_rpb, conv_w, conv_b, conv_ln_w, conv_ln_b, w_out, router_w, router_b,
           exp_w1, exp_b1, exp_w2, exp_b2):
    batch, seq, d = x.shape
    ctx_len = ctx.shape[1]
    depth = w_ada.shape[0]
    n_lat = batch * seq
    n_ctx = batch * ctx_len
    n_all = n_lat + n_ctx
    assert depth == 2
    assert seq % TOK_TILE == 0 and n_ctx % TOK_TILE == 0 and seq % ctx_len == 0
    assert (seq // GRID_W) % NA_QROWS == 0 and seq // GRID_W >= NA_KROWS

    mod_rows = -(-(batch + 1) // SUBLANES) * SUBLANES
    cc = jnp.zeros((mod_rows, d), F32).at[:batch].set(c).at[batch].set(c_ctx)
    mods = _ada(cc, w_ada, b_ada)[:, :batch + 1].reshape(depth, batch + 1, 1, 6, d)
    x_parts = (x.reshape(n_lat, d), ctx.reshape(n_ctx, d))
    rope = _rope_tables(seq)
    n_exp = exp_w1.shape[1]
    ew1 = exp_w1.reshape((depth * n_exp,) + exp_w1.shape[2:])
    eb1 = exp_b1.reshape((depth * n_exp,) + exp_b1.shape[2:])
    ew2 = exp_w2.reshape((depth * n_exp,) + exp_w2.shape[2:])
    eb2 = exp_b2.reshape((depth * n_exp,) + exp_b2.shape[2:])

    for l in range(depth):
        need_ctx = l < depth - 1
        sh1, sc1, g1, sh2, sc2, g2 = [mods[l, :, :, i, :] for i in range(6)]
        pa, kt, g, gt, pb, u = _inproj(x_parts, sh1, sc1, norm_mix_w[l], w_in[l], n_lat, seq)
        a_lat, a_ctx = _mlstm(pa, kt, g, gt, mlstm_ig_b[l], mlstm_fg_b[l], mlstm_norm_w[l], rope,
                              batch, seq, ctx_len, need_ctx)
        b_lat, b_ctx = _na(pb, na_qnorm_w[l], na_knorm_w[l], na_rpb[l], batch, seq, ctx_len, need_ctx)
        c_lat = _conv(u, conv_w[l], conv_b[l], conv_ln_w[l], conv_ln_b[l], 0, batch, seq)
        if need_ctx:
            c_ctx_out = _conv(u, conv_w[l], conv_b[l], conv_ln_w[l], conv_ln_b[l], n_lat // ctx_len, batch, ctx_len)
            a_all, b_all, c_all = (a_lat, a_ctx), (b_lat, b_ctx), (c_lat, c_ctx_out)
            n_rows = n_all
        else:
            a_all, b_all, c_all = (a_lat,), (b_lat,), (c_lat,)
            n_rows = n_lat
        xmid, hx, idx, gates, counts = _outproj(a_all, b_all, c_all, x_parts, g1, sh2, sc2, norm_ffn_w[l], w_out[l],
                                                router_w[l], router_b[l], n_rows, n_lat, seq)
        xall = _moe(hx, idx, counts, gates, xmid, g2, l, ew1, eb1, ew2, eb2, n_rows, n_lat, seq)
        x_parts = (xall,)
    return x_parts[0][:n_lat].reshape(batch, seq, d)
```

```python
import functools

import numpy as np
import jax
import jax.numpy as jnp
from jax import lax
from jax.experimental import pallas as pl
from jax.experimental.pallas import tpu as pltpu

F32 = jnp.float32
BF16 = jnp.bfloat16
MXU_DTYPE = BF16

GRID_W = 64
HEAD_DIM = 64
MLSTM_HEADS = 4
NA_HEADS = 8
CONV_CH = 256
MLSTM_W = MLSTM_HEADS * HEAD_DIM
NA_W = NA_HEADS * HEAD_DIM
NA_WIN_R = 8
NA_WIN_C = 16
CONV_WIDTH = 31
ROPE_BASE = 10000.0
N_EXPERTS = 32
TOP_K = 4
SWIGLU_LIMIT = 7.0
SWIGLU_ALPHA = 1.702
EPS = 1e-6

A_Q = 0
A_G = 4 * MLSTM_W
B_Q = A_G + 4 * MLSTM_HEADS
C_A = B_Q + 3 * NA_W
IN_COLS = C_A + 2 * CONV_CH

LANES = 128
SUBLANES = 8
MXU_TILE = 256
NEG_BIG = -1e30
LOG2_E = 1.4426950408889634
VMEM_LIMIT = 56 * 1024 * 1024

NA_QROWS = 4
NA_KROWS = NA_QROWS + NA_WIN_R - 1
TOK_TILE = 512
ADA_TILE = 512
MOE_BLOCK = 512
ROUTE_TILE = 256
ROW_CHUNKS = (256, 128, 64, 32, 16, 8)
LONG_RUN = 64
CONV_ROWS = 64
CONV_PAD = 16


def _mm(a, b):
    return jnp.dot(a.astype(MXU_DTYPE), b.astype(MXU_DTYPE), preferred_element_type=F32)


def _mm_nt(a, b):
    return lax.dot_general(a.astype(MXU_DTYPE), b.astype(MXU_DTYPE), (((1,), (1,)), ((), ())),
                           preferred_element_type=F32)


def _cparams(sem):
    return pltpu.CompilerParams(dimension_semantics=sem, vmem_limit_bytes=VMEM_LIMIT)


def _ada_kernel(c_ref, w_ref, b_ref, o_ref):
    cc = c_ref[...]
    s = cc * jax.nn.sigmoid(cc)
    o_ref[0] = _mm(s, w_ref[0]) + b_ref[0]


def _ada(cc, w_ada, b_ada):
    depth, d, n = w_ada.shape
    rows = cc.shape[0]
    tn = ADA_TILE
    return pl.pallas_call(
        _ada_kernel,
        grid=(depth, n // tn),
        in_specs=[pl.BlockSpec((rows, d), lambda l, j: (0, 0)),
                  pl.BlockSpec((1, d, tn), lambda l, j: (l, 0, j)),
                  pl.BlockSpec((1, 1, tn), lambda l, j: (l, 0, j))],
        out_specs=pl.BlockSpec((1, rows, tn), lambda l, j: (l, 0, j)),
        out_shape=jax.ShapeDtypeStruct((depth, rows, n), F32),
        compiler_params=_cparams(("parallel", "parallel")),
        name="ada_mod",
    )(cc, w_ada, b_ada.reshape(depth, 1, n))


def _token_specs(parts, tm, n_lat_tiles):
    cols = parts[0].shape[1]
    if len(parts) == 1:
        return [pl.BlockSpec((tm, cols), lambda i: (i, 0))]
    return [pl.BlockSpec((tm, cols), lambda i: (jnp.minimum(i, n_lat_tiles - 1), 0)),
            pl.BlockSpec((tm, cols), lambda i: (jnp.maximum(i - n_lat_tiles, 0), 0))]


def _token_tile(refs, n_lat_tiles):
    if len(refs) == 1:
        return refs[0][...]
    return jnp.where(pl.program_id(0) < n_lat_tiles, refs[0][...], refs[1][...])


def _inproj_kernel(n_x, n_lat_tiles, *refs):
    x_refs = refs[:n_x]
    (sh_ref, sc_ref, nw_ref, wa_ref, wkt_ref, wg_ref, wgt_ref, wb_ref, wc_ref,
     pa_ref, kt_ref, g_ref, gt_ref, pb_ref, u_ref) = refs[n_x:]
    x = _token_tile(x_refs, n_lat_tiles)
    ms = jnp.mean(x * x, axis=-1, keepdims=True)
    y = x * lax.rsqrt(ms + EPS) * nw_ref[...]
    h = (y * (1.0 + sc_ref[0]) + sh_ref[0]).astype(MXU_DTYPE)
    pa_ref[...] = _mm(h, wa_ref[...]).astype(pa_ref.dtype)
    kt_ref[...] = _mm_nt(wkt_ref[...], h).astype(kt_ref.dtype)
    g_ref[...] = _mm(h, wg_ref[...])
    gt_ref[...] = _mm_nt(wgt_ref[...], h)
    pb_ref[...] = _mm(h, wb_ref[...]).astype(pb_ref.dtype)
    pc = _mm(h, wc_ref[...])
    u_ref[...] = (pc[:, :CONV_CH] * jax.nn.sigmoid(pc[:, CONV_CH:])).astype(u_ref.dtype)


def _inproj(x_parts, shift, scale, norm_w, w_in, n_lat, seq):
    n = sum(p.shape[0] for p in x_parts)
    d = x_parts[0].shape[1]
    tm = TOK_TILE
    n_lat_tiles = n_lat // tm
    per_batch = seq // tm
    n_mod = shift.shape[0]

    def mod_map(i):
        return (jnp.where(i < n_lat_tiles, i // per_batch, n_mod - 1), 0, 0)

    wq, wk, wv, wo = (w_in[:, A_Q + j * MLSTM_W:A_Q + (j + 1) * MLSTM_W] for j in range(4))
    wv = jnp.pad(wv.reshape(d, MLSTM_HEADS, HEAD_DIM), ((0, 0), (0, 0), (0, LANES - HEAD_DIM)))
    wa = jnp.concatenate([wq, wv.reshape(d, MLSTM_HEADS * LANES), wo], axis=1).astype(MXU_DTYPE)
    wkt = wk.T.astype(MXU_DTYPE)
    wg = w_in[:, A_G:B_Q].astype(MXU_DTYPE)
    wb = w_in[:, B_Q:C_A].astype(MXU_DTYPE)
    wc = w_in[:, C_A:IN_COLS].astype(MXU_DTYPE)
    ng = B_Q - A_G
    full = lambda r, c: pl.BlockSpec((r, c), lambda i: (0, 0))
    return pl.pallas_call(
        functools.partial(_inproj_kernel, len(x_parts), n_lat_tiles),
        grid=(n // tm,),
        in_specs=_token_specs(x_parts, tm, n_lat_tiles) + [
                  pl.BlockSpec((1, 1, d), mod_map),
                  pl.BlockSpec((1, 1, d), mod_map),
                  full(1, d),
                  full(d, A_G), full(MLSTM_W, d), full(d, ng), full(ng, d), full(d, 3 * NA_W),
                  full(d, 2 * CONV_CH)],
        out_specs=[pl.BlockSpec((tm, A_G), lambda i: (i, 0)),
                   pl.BlockSpec((MLSTM_W, tm), lambda i: (0, i)),
                   pl.BlockSpec((tm, ng), lambda i: (i, 0)),
                   pl.BlockSpec((ng, tm), lambda i: (0, i)),
                   pl.BlockSpec((tm, 3 * NA_W), lambda i: (i, 0)),
                   pl.BlockSpec((tm, CONV_CH), lambda i: (i, 0))],
        out_shape=[jax.ShapeDtypeStruct((n, A_G), BF16),
                   jax.ShapeDtypeStruct((MLSTM_W, n), BF16),
                   jax.ShapeDtypeStruct((n, ng), F32),
                   jax.ShapeDtypeStruct((ng, n), F32),
                   jax.ShapeDtypeStruct((n, 3 * NA_W), BF16),
                   jax.ShapeDtypeStruct((n, CONV_CH), BF16)],
        compiler_params=_cparams(("parallel",)),
        name="in_proj",
    )(*x_parts, shift, scale, norm_w.reshape(1, d), wa, wkt, wg, wg.T, wb, wc)


def _split3(x):
    hi = x.astype(BF16)
    r1 = x - hi.astype(F32)
    mid = r1.astype(BF16)
    lo = (r1 - mid.astype(F32)).astype(BF16)
    return hi, mid, lo


def _tri_left(tri, x):
    return sum(jnp.dot(tri, p, preferred_element_type=F32) for p in _split3(x))


def _tri_right(x, tri):
    return sum(jnp.dot(p, tri, preferred_element_type=F32) for p in _split3(x))


def _log_sigmoid(x):
    return jnp.minimum(x, 0.0) - jnp.log(1.0 + jnp.exp(-jnp.abs(x)))


def _mlstm_direction(z, q, kt, vx, gcol, grow, c_st, m_st, h_ref, row0, lc, with_output):
    nh = MLSTM_HEADS
    ti = lax.broadcasted_iota(jnp.int32, (lc, lc), 0)
    si = lax.broadcasted_iota(jnp.int32, (lc, lc), 1)
    lower = si <= ti
    upper = si >= ti
    tl = jnp.where(lower, 1.0, 0.0).astype(BF16)
    tu = jnp.where(upper, 1.0, 0.0).astype(BF16)
    f_col = _log_sigmoid(gcol[:, 2 * nh + z * nh:2 * nh + (z + 1) * nh])
    i_row = grow[z * nh:(z + 1) * nh, :]
    f_row = _log_sigmoid(grow[2 * nh + z * nh:2 * nh + (z + 1) * nh, :])
    if z == 0:
        b_col = _tri_left(tl, f_col)
        b_row = _tri_right(f_row, tu)
        b_tot = b_col[lc - 1:lc, :]
        mask = lower
    else:
        b_col = _tri_left(tu, f_col)
        b_row = _tri_right(f_row, tl)
        b_tot = b_col[0:1, :]
        mask = upper
    a_row = i_row - b_row
    a_max = jnp.max(a_row, axis=1, keepdims=True)
    lane256 = lax.broadcasted_iota(jnp.int32, (1, MLSTM_W), 1)
    lane128 = lax.broadcasted_iota(jnp.int32, (1, LANES), 1)
    kt_mx = kt.astype(MXU_DTYPE)
    c_all = c_st[z].astype(MXU_DTYPE)
    for h in range(nh):
        r = z * nh + h
        ar = a_row[h:h + 1, :]
        bl = b_tot[:, h:h + 1]
        m_old = m_st[r:r + 1, 0:1]
        vh = vx[:, h * LANES:(h + 1) * LANES]
        vext = jnp.where(lane128 == HEAD_DIM, jnp.ones((), vh.dtype), vh).astype(MXU_DTYPE)
        if with_output:
            qh = jnp.where(lane256 // HEAD_DIM == h, q, 0.0).astype(MXU_DTYPE)
            am = jnp.where(mask, ar, NEG_BIG)
            g = jnp.maximum(m_old, jnp.max(am, axis=1, keepdims=True))
            s = _mm(qh, kt_mx) * jnp.exp(am - g)
            w_inter = jnp.exp(m_old - g)
            nd = _mm(s, vext) + w_inter * _mm(qh, c_all)
            den = nd[:, HEAD_DIM:HEAD_DIM + 1]
            hval = nd / jnp.maximum(jnp.abs(den), jnp.exp(-(b_col[:, h:h + 1] + g)))
            h_ref[pl.ds(row0, lc), h * LANES:(h + 1) * LANES] = hval
        m_new = bl + jnp.maximum(m_old, a_max[h:h + 1, :])
        w_row = jnp.exp(bl + ar - m_new)
        decay = jnp.exp(bl + m_old - m_new)
        rows = slice(h * HEAD_DIM, (h + 1) * HEAD_DIM)
        c_st[z, rows, :] = decay * c_st[z, rows, :] + _mm(kt[rows, :] * w_row, vext)
        m_st[r:r + 1, :] = jnp.broadcast_to(m_new, (1, LANES))


def _mlstm_kernel(lc, nc, need_ctx,
                  pa_c, kt_c, g_c, gt_c,
                  pa_f, kt_f, g_f, gt_f, cos_f, sin_f, cost_f, sint_f,
                  pa_b, kt_b, g_b, gt_b, cos_b, sin_b, cost_b, sint_b,
                  o_lat, o_ctx, brow_ref, bcol_ref, nw_ref, perm_ref,
                  *rest):
    if need_ctx:
        out_lat, out_ctx, hf, hb, c_st, m_st = rest
    else:
        out_lat, hf, hb, c_st, m_st = rest
        out_ctx = None
    s = pl.program_id(1)
    w = MLSTM_W
    k_scale = HEAD_DIM ** -0.5

    def load(pa, kt_ref, rope_refs):
        q = pa[:, 0:w].astype(F32)
        kt = kt_ref[...].astype(F32)
        vx = pa[:, w:w + MLSTM_HEADS * LANES]
        if rope_refs is not None:
            cos_ref, sin_ref, cost_ref, sint_ref = rope_refs
            q = q * cos_ref[...] + _mm(q, perm_ref[...]) * sin_ref[...]
            blk = HEAD_DIM // 4
            swapped = jnp.concatenate([kt[(i ^ 1) * blk:((i ^ 1) + 1) * blk, :] for i in range(w // blk)], axis=0)
            kt = kt * cost_ref[...] + swapped * sint_ref[...]
        return q, kt * k_scale, vx

    @pl.when(s == 0)
    def _():
        c_st[...] = jnp.zeros(c_st.shape, F32)
        m_st[...] = jnp.zeros(m_st.shape, F32)
        q, kt, vx = load(pa_c, kt_c, None)
        gcol = g_c[...] + brow_ref[...]
        grow = gt_c[...] + bcol_ref[...]
        for z, h_ref in ((0, hf), (1, hb)):
            _mlstm_direction(z, q, kt, vx, gcol, grow, c_st, m_st, h_ref, 0, lc, need_ctx)

    @pl.when(s > 0)
    def _():
        for z, h_ref, refs in ((0, hf, (pa_f, kt_f, g_f, gt_f, (cos_f, sin_f, cost_f, sint_f))),
                               (1, hb, (pa_b, kt_b, g_b, gt_b, (cos_b, sin_b, cost_b, sint_b)))):
            pa, kt_ref, g, gt, rope_refs = refs
            j = s - 1 if z == 0 else nc - s
            row0 = pl.multiple_of(lc + j * lc, lc)
            q, kt, vx = load(pa, kt_ref, rope_refs)
            gcol = g[...] + brow_ref[...]
            grow = gt[...] + bcol_ref[...]
            _mlstm_direction(z, q, kt, vx, gcol, grow, c_st, m_st, h_ref, row0, lc, True)

    @pl.when(s == nc)
    def _():
        lane128 = lax.broadcasted_iota(jnp.int32, (1, LANES), 1)
        mean_w = jnp.where(lax.broadcasted_iota(jnp.int32, (LANES, LANES), 0) < HEAD_DIM,
                           1.0 / HEAD_DIM, 0.0).astype(BF16)
        first = 0 if need_ctx else 1
        for ch in range(first, nc + 1):
            rows = slice(ch * lc, (ch + 1) * lc)
            if ch == 0:
                o_val, dst, dst_rows = o_ctx[...], out_ctx, slice(0, lc)
            else:
                dst_rows = slice((ch - 1) * lc, ch * lc)
                o_val, dst = o_lat[dst_rows, :], out_lat
            o_val = o_val.astype(F32)
            for p in range(MLSTM_HEADS // 2):
                pair = []
                for h in (2 * p, 2 * p + 1):
                    hv = hf[rows, h * LANES:(h + 1) * LANES] + hb[rows, h * LANES:(h + 1) * LANES]
                    hv = jnp.where(lane128 < HEAD_DIM, hv, 0.0)
                    sq = hv * hv
                    sq_hi = sq.astype(BF16)
                    sq_lo = (sq - sq_hi.astype(F32)).astype(BF16)
                    ms = (jnp.dot(sq_hi, mean_w, preferred_element_type=F32)
                          + jnp.dot(sq_lo, mean_w, preferred_element_type=F32))
                    pair.append(hv * lax.rsqrt(ms + EPS))
                packed = jnp.where(lane128 < HEAD_DIM, pair[0], pltpu.roll(pair[1], HEAD_DIM, 1))
                cols = slice(p * LANES, (p + 1) * LANES)
                res = packed * nw_ref[:, cols] * jax.nn.sigmoid(o_val[:, cols])
                dst[dst_rows, cols] = res.astype(dst.dtype)


def _rope_tables(seq):
    half = HEAD_DIM // 2
    quarter = half // 2
    t = jnp.arange(seq, dtype=jnp.int32)
    inv_freq = ROPE_BASE ** (-jnp.arange(quarter, dtype=F32) / quarter)
    parts_c, parts_s = [], []
    for pos in (t // GRID_W, t % GRID_W):
        ang = pos.astype(F32)[:, None] * inv_freq[None, :]
        parts_c += [jnp.cos(ang), jnp.cos(ang)]
        parts_s += [-jnp.sin(ang), jnp.sin(ang)]
    cos = jnp.tile(jnp.concatenate(parts_c, axis=-1), (1, MLSTM_HEADS))
    sin = jnp.tile(jnp.concatenate(parts_s, axis=-1), (1, MLSTM_HEADS))
    j = np.arange(MLSTM_W)
    partner = np.where(j % half < quarter, j + quarter, j - quarter)
    perm = np.zeros((MLSTM_W, MLSTM_W), np.float32)
    perm[partner, j] = 1.0
    return cos, sin, cos.T, sin.T, jnp.asarray(perm, dtype=MXU_DTYPE)


def _mlstm(pa, kt, g, gt, ig_b, fg_b, norm_w, rope, batch, seq, ctx_len, need_ctx):
    lc = ctx_len
    nc = seq // lc
    n_lat = batch * seq
    cos, sin, cos_t, sin_t, perm = rope
    bias = jnp.concatenate([ig_b.reshape(-1), fg_b.reshape(-1)]).astype(F32)
    ng = bias.shape[0]
    lat_blocks = n_lat // lc

    def fwd(b, s):
        return b * nc + jnp.maximum(s - 1, 0)

    def bwd(b, s):
        return b * nc + nc - jnp.maximum(s, 1)

    def fwd_c(b, s):
        return jnp.maximum(s - 1, 0)

    def bwd_c(b, s):
        return nc - jnp.maximum(s, 1)

    def lat_specs(chunk, chunk_c):
        return [pl.BlockSpec((lc, A_G), lambda b, s: (chunk(b, s), 0)),
                pl.BlockSpec((MLSTM_W, lc), lambda b, s: (0, chunk(b, s))),
                pl.BlockSpec((lc, ng), lambda b, s: (chunk(b, s), 0)),
                pl.BlockSpec((ng, lc), lambda b, s: (0, chunk(b, s))),
                pl.BlockSpec((lc, MLSTM_W), lambda b, s: (chunk_c(b, s), 0)),
                pl.BlockSpec((lc, MLSTM_W), lambda b, s: (chunk_c(b, s), 0)),
                pl.BlockSpec((MLSTM_W, lc), lambda b, s: (0, chunk_c(b, s))),
                pl.BlockSpec((MLSTM_W, lc), lambda b, s: (0, chunk_c(b, s)))]

    in_specs = ([pl.BlockSpec((lc, A_G), lambda b, s: (lat_blocks + b, 0)),
                 pl.BlockSpec((MLSTM_W, lc), lambda b, s: (0, lat_blocks + b)),
                 pl.BlockSpec((lc, ng), lambda b, s: (lat_blocks + b, 0)),
                 pl.BlockSpec((ng, lc), lambda b, s: (0, lat_blocks + b))]
                + lat_specs(fwd, fwd_c) + lat_specs(bwd, bwd_c)
                + [pl.BlockSpec((seq, MLSTM_W), lambda b, s: (b, 3)),
                   pl.BlockSpec((lc, MLSTM_W), lambda b, s: (lat_blocks + b, 3)),
                   pl.BlockSpec((1, ng), lambda b, s: (0, 0)),
                   pl.BlockSpec((ng, 1), lambda b, s: (0, 0)),
                   pl.BlockSpec((1, MLSTM_W), lambda b, s: (0, 0)),
                   pl.BlockSpec((MLSTM_W, MLSTM_W), lambda b, s: (0, 0))])
    out_specs = [pl.BlockSpec((seq, MLSTM_W), lambda b, s: (b, 0))]
    out_shape = [jax.ShapeDtypeStruct((n_lat, MLSTM_W), BF16)]
    if need_ctx:
        out_specs.append(pl.BlockSpec((lc, MLSTM_W), lambda b, s: (b, 0)))
        out_shape.append(jax.ShapeDtypeStruct((batch * ctx_len, MLSTM_W), BF16))
    t_all = ctx_len + seq
    outs = pl.pallas_call(
        functools.partial(_mlstm_kernel, lc, nc, need_ctx),
        grid=(batch, nc + 1),
        in_specs=in_specs,
        out_specs=out_specs,
        out_shape=out_shape,
        scratch_shapes=[pltpu.VMEM((t_all, MLSTM_HEADS * LANES), F32),
                        pltpu.VMEM((t_all, MLSTM_HEADS * LANES), F32),
                        pltpu.VMEM((2, MLSTM_W, LANES), F32),
                        pltpu.VMEM((2 * MLSTM_HEADS, LANES), F32)],
        compiler_params=_cparams(("parallel", "arbitrary")),
        name="mlstm",
    )(pa, kt, g, gt, pa, kt, g, gt, cos, sin, cos_t, sin_t, pa, kt, g, gt, cos, sin, cos_t, sin_t, pa, pa,
      bias.reshape(1, ng), bias.reshape(ng, 1), norm_w.reshape(1, MLSTM_W).astype(F32), perm)
    return (outs[0], outs[1]) if need_ctx else (outs[0], None)


def _na_patterns(n_rows):
    kr = min(NA_WIN_R, n_rows)
    n_dr = 2 * NA_WIN_R - 1
    pats, pat_ids, bases = [], [], []
    for gi in range(n_rows // NA_QROWS):
        base = int(np.clip(NA_QROWS * gi - NA_WIN_R // 2, 0, n_rows - NA_KROWS))
        dr = np.full((NA_QROWS, NA_KROWS), n_dr, np.int32)
        for qr in range(NA_QROWS):
            r = NA_QROWS * gi + qr
            r0 = int(np.clip(r - kr // 2, 0, n_rows - kr))
            for kj in range(NA_KROWS):
                if r0 <= base + kj < r0 + kr:
                    dr[qr, kj] = base + kj - r + NA_WIN_R - 1
        for pi, p in enumerate(pats):
            if np.array_equal(p, dr):
                pat_ids.append(pi)
                break
        else:
            pat_ids.append(len(pats))
            pats.append(dr)
        bases.append(base)
    return tuple(pat_ids), tuple(bases), np.stack(pats)


def _na_bias_table(rpb, row_idx):
    heads = rpb.shape[0]
    col = np.arange(GRID_W)
    col_start = np.clip(col - NA_WIN_C // 2, 0, GRID_W - NA_WIN_C)
    in_win = (col[None, :] >= col_start[:, None]) & (col[None, :] < col_start[:, None] + NA_WIN_C)
    dc = np.clip(col[None, :] - col[:, None] + NA_WIN_C - 1, 0, 2 * NA_WIN_C - 2)
    onehot = (dc[None] == np.arange(2 * NA_WIN_C - 1)[:, None, None]).astype(np.float32)
    planes = jnp.einsum('hdc,cqk->hdqk', rpb, onehot, precision=lax.Precision.HIGHEST)
    planes = jnp.where(in_win[None, None], planes, NEG_BIG)
    planes = jnp.concatenate([planes, jnp.full((heads, 1, GRID_W, GRID_W), NEG_BIG, F32)], axis=1)
    npat = row_idx.shape[0]
    tab = planes[:, row_idx.reshape(-1)].reshape(heads, npat, NA_QROWS, NA_KROWS, GRID_W, GRID_W)
    return tab.transpose(0, 1, 2, 4, 3, 5).reshape(heads, npat, NA_QROWS * GRID_W, NA_KROWS * GRID_W)


def _na_kernel(pat_ids, bases, need_ctx, q_ref, k_ref, v_ref, kc_ref, vc_ref, *rest):
    if need_ctx:
        qc_ref, bias_ref, qw_ref, kw_ref, out_ref, outc_ref, kn_s, kcn_s = rest
    else:
        bias_ref, qw_ref, kw_ref, out_ref, kn_s, kcn_s = rest
    lane = lax.broadcasted_iota(jnp.int32, (1, LANES), 1)
    low = lane < HEAD_DIM
    inv_d = 1.0 / HEAD_DIM

    def rmsn(x, w):
        x2 = x * x
        s0 = jnp.sum(jnp.where(low, x2, 0.0), axis=-1, keepdims=True)
        s1 = jnp.sum(jnp.where(low, 0.0, x2), axis=-1, keepdims=True)
        r = jnp.where(low, lax.rsqrt(s0 * inv_d + EPS), lax.rsqrt(s1 * inv_d + EPS))
        return x * r * w

    qw = qw_ref[...]
    kn_s[...] = rmsn(k_ref[...].astype(F32), kw_ref[...]).astype(kn_s.dtype)
    kcn_s[...] = rmsn(kc_ref[...].astype(F32), kw_ref[...]).astype(kcn_s.dtype)
    kcn = kcn_s[...]
    vc = vc_ref[...]
    scale = HEAD_DIM ** -0.5 * LOG2_E
    nq = NA_QROWS * GRID_W
    nk = NA_KROWS * GRID_W

    def attend(qn, parts):
        outs = []
        for hh in range(2):
            qh = jnp.where(low if hh == 0 else jnp.logical_not(low), qn, 0.0).astype(MXU_DTYPE)
            scores = []
            for keys, _, bias in parts:
                sc = _mm_nt(qh, keys)
                if bias is not None:
                    sc = sc + bias[hh]
                scores.append(sc)
            m = scores[0].max(axis=-1, keepdims=True)
            for sc in scores[1:]:
                m = jnp.maximum(m, sc.max(axis=-1, keepdims=True))
            acc = None
            den = None
            for sc, (_, vals, _) in zip(scores, parts):
                p = jnp.exp2(sc - m)
                d = jnp.sum(p, axis=-1, keepdims=True)
                o = _mm(p, vals)
                acc = o if acc is None else acc + o
                den = d if den is None else den + d
            outs.append(acc / den)
        return jnp.where(low, outs[0], outs[1])

    for gi, (pid, base) in enumerate(zip(pat_ids, bases)):
        qn = rmsn(q_ref[gi * nq:(gi + 1) * nq, :].astype(F32), qw) * scale
        kwin = kn_s[base * GRID_W:base * GRID_W + nk, :]
        vwin = v_ref[base * GRID_W:base * GRID_W + nk, :]
        bias = (bias_ref[0, pid], bias_ref[1, pid])
        res = attend(qn, [(kwin, vwin, bias), (kcn, vc, None)])
        out_ref[gi * nq:(gi + 1) * nq, :] = res.astype(out_ref.dtype)

    if need_ctx:
        qn = rmsn(qc_ref[...].astype(F32), qw) * scale
        outc_ref[...] = attend(qn, [(kcn, vc, None)]).astype(outc_ref.dtype)


def _na(pb, qn_w, kn_w, rpb, batch, seq, ctx_len, need_ctx):
    n_rows = seq // GRID_W
    pat_ids, bases, row_idx = _na_patterns(n_rows)
    npat = row_idx.shape[0]
    nq, nk = NA_QROWS * GRID_W, NA_KROWS * GRID_W
    bias = _na_bias_table(rpb.astype(F32) * LOG2_E, row_idx)
    n_lat = batch * seq
    pairs = NA_HEADS // 2
    qoff, koff, voff = 0, pairs, 2 * pairs

    in_specs = [pl.BlockSpec((seq, LANES), lambda p, b: (b, qoff + p)),
                pl.BlockSpec((seq, LANES), lambda p, b: (b, koff + p)),
                pl.BlockSpec((seq, LANES), lambda p, b: (b, voff + p)),
                pl.BlockSpec((ctx_len, LANES), lambda p, b: (n_lat // ctx_len + b, koff + p)),
                pl.BlockSpec((ctx_len, LANES), lambda p, b: (n_lat // ctx_len + b, voff + p))]
    args = [pb, pb, pb, pb, pb]
    if need_ctx:
        in_specs.append(pl.BlockSpec((ctx_len, LANES), lambda p, b: (n_lat // ctx_len + b, qoff + p)))
        args.append(pb)
    in_specs += [pl.BlockSpec((2, npat, nq, nk), lambda p, b: (p, 0, 0, 0)),
                 pl.BlockSpec((1, LANES), lambda p, b: (0, 0)),
                 pl.BlockSpec((1, LANES), lambda p, b: (0, 0))]
    args += [bias, jnp.tile(qn_w.astype(F32), 2).reshape(1, LANES), jnp.tile(kn_w.astype(F32), 2).reshape(1, LANES)]
    out_specs = [pl.BlockSpec((seq, LANES), lambda p, b: (b, p))]
    out_shape = [jax.ShapeDtypeStruct((n_lat, NA_W), BF16)]
    if need_ctx:
        out_specs.append(pl.BlockSpec((ctx_len, LANES), lambda p, b: (b, p)))
        out_shape.append(jax.ShapeDtypeStruct((batch * ctx_len, NA_W), BF16))
    outs = pl.pallas_call(
        functools.partial(_na_kernel, pat_ids, bases, need_ctx),
        grid=(pairs, batch),
        in_specs=in_specs,
        out_specs=out_specs,
        out_shape=out_shape,
        scratch_shapes=[pltpu.VMEM((seq, LANES), MXU_DTYPE), pltpu.VMEM((ctx_len, LANES), MXU_DTYPE)],
        compiler_params=_cparams(("parallel", "parallel")),
        name="na_attn",
    )(*args)
    return (outs[0], outs[1]) if need_ctx else (outs[0], None)


def _conv_kernel(t_len, u_ref, w_ref, cb_ref, lw_ref, lb_ref, o_ref, pad_s):
    zeros = jnp.zeros((CONV_PAD, CONV_CH), F32)
    pad_s[0:CONV_PAD, :] = zeros
    pad_s[CONV_PAD + t_len:2 * CONV_PAD + t_len, :] = zeros
    pad_s[CONV_PAD:CONV_PAD + t_len, :] = u_ref[...].astype(F32)
    shift = CONV_PAD - CONV_WIDTH // 2

    def body(c, carry):
        r0 = pl.multiple_of(c * CONV_ROWS, CONV_ROWS)
        n_win = CONV_ROWS + 2 * CONV_PAD
        win = pad_s[pl.ds(r0, n_win), :]
        rot = [win] + [pltpu.roll(win, n_win - ph, 0) for ph in range(1, SUBLANES)]
        acc = jnp.zeros((CONV_ROWS, CONV_CH), F32) + cb_ref[...]
        for j in range(CONV_WIDTH):
            ph, al = (j + shift) % SUBLANES, (j + shift) // SUBLANES * SUBLANES
            acc = acc + rot[ph][al:al + CONV_ROWS, :] * w_ref[j:j + 1, :]
        mean = jnp.mean(acc, axis=-1, keepdims=True)
        xc = acc - mean
        var = jnp.mean(xc * xc, axis=-1, keepdims=True)
        y = xc * lax.rsqrt(var + EPS) * lw_ref[...] + lb_ref[...]
        o_ref[pl.ds(r0, CONV_ROWS), :] = (y * jax.nn.sigmoid(y)).astype(o_ref.dtype)
        return carry

    lax.fori_loop(0, t_len // CONV_ROWS, body, 0)


def _conv(u, conv_w, conv_b, ln_w, ln_b, first_block, n_seq, t_len):
    row = lambda a: a.reshape(1, CONV_CH).astype(F32)
    return pl.pallas_call(
        functools.partial(_conv_kernel, t_len),
        grid=(n_seq,),
        in_specs=[pl.BlockSpec((t_len, CONV_CH), lambda b: (first_block + b, 0)),
                  pl.BlockSpec((CONV_WIDTH, CONV_CH), lambda b: (0, 0)),
                  pl.BlockSpec((1, CONV_CH), lambda b: (0, 0)),
                  pl.BlockSpec((1, CONV_CH), lambda b: (0, 0)),
                  pl.BlockSpec((1, CONV_CH), lambda b: (0, 0))],
        out_specs=pl.BlockSpec((t_len, CONV_CH), lambda b: (b, 0)),
        out_shape=jax.ShapeDtypeStruct((n_seq * t_len, CONV_CH), BF16),
        scratch_shapes=[pltpu.VMEM((t_len + 2 * CONV_PAD, CONV_CH), F32)],
        compiler_params=_cparams(("parallel",)),
        name="conv_module",
    )(u, conv_w.astype(F32), row(conv_b), row(ln_w), row(ln_b))


def _outproj_kernel(rt, n_src, n_lat_tiles, *refs):
    a_refs, b_refs, c_refs, x_refs = (refs[j * n_src:(j + 1) * n_src] for j in range(4))
    (g1_ref, sh_ref, sc_ref, nw_ref, wa_ref, wb_ref, wc_ref, rwh_ref, rwl_ref, rb_ref,
     xo_ref, hx_ref, idx_ref, gate_ref, cnt_ref) = refs[4 * n_src:]
    tile = lambda parts: _token_tile(parts, n_lat_tiles)
    mix = _mm(tile(a_refs), wa_ref[...]) + _mm(tile(b_refs), wb_ref[...]) + _mm(tile(c_refs), wc_ref[...])
    xn = tile(x_refs) + g1_ref[0] * mix
    xo_ref[...] = xn
    ms = jnp.mean(xn * xn, axis=-1, keepdims=True)
    hx = xn * lax.rsqrt(ms + EPS) * nw_ref[...] * (1.0 + sc_ref[0]) + sh_ref[0]
    hx_ref[...] = hx
    h_hi = hx.astype(BF16)
    h_lo = (hx - h_hi.astype(F32)).astype(BF16)
    logits = (jnp.dot(h_hi, rwh_ref[...], preferred_element_type=F32)
              + jnp.dot(h_lo, rwh_ref[...], preferred_element_type=F32)
              + jnp.dot(h_hi, rwl_ref[...], preferred_element_type=F32)) + rb_ref[...]
    lane = lax.broadcasted_iota(jnp.int32, logits.shape, 1)
    idx_out = jnp.zeros(logits.shape, jnp.int32)
    val_out = jnp.zeros(logits.shape, F32)
    top = None
    den = None
    sels = []
    for kk in range(TOP_K):
        m = jnp.max(logits, axis=-1, keepdims=True)
        sel = jnp.min(jnp.where(logits == m, lane, LANES), axis=-1, keepdims=True)
        if kk == 0:
            top = m
        e = jnp.exp(m - top)
        den = e if den is None else den + e
        idx_out = jnp.where(lane == kk, sel, idx_out)
        val_out = jnp.where(lane == kk, e, val_out)
        logits = jnp.where(lane == sel, -jnp.inf, logits)
        sels.append(sel)
    gate_ref[...] = val_out / den
    tm = logits.shape[0]
    chosen = jnp.where(logits == -jnp.inf, 1.0, 0.0)
    ti = lax.broadcasted_iota(jnp.int32, (tm, tm), 0)
    si = lax.broadcasted_iota(jnp.int32, (tm, tm), 1)
    earlier = jnp.where(jnp.logical_and(si < ti, si // rt == ti // rt), 1.0, 0.0).astype(BF16)
    before = jnp.dot(earlier, chosen.astype(BF16), preferred_element_type=F32)
    for kk in range(TOP_K):
        rank = jnp.sum(jnp.where(lane == sels[kk], before, 0.0), axis=-1, keepdims=True)
        idx_out = jnp.where(lane == TOP_K + kk, rank.astype(jnp.int32), idx_out)
    idx_ref[...] = idx_out
    row = lax.broadcasted_iota(jnp.int32, cnt_ref.shape[1:], 0)
    cnt = jnp.zeros(cnt_ref.shape[1:], F32)
    for sub in range(tm // rt):
        cnt = jnp.where(row == sub, jnp.sum(chosen[sub * rt:(sub + 1) * rt], axis=0, keepdims=True), cnt)
    cnt_ref[0] = cnt.astype(jnp.int32)


def _outproj(a, b, c, x_parts, g1, sh2, sc2, norm_w, w_out, router_w, router_b, n_rows, n_lat, seq):
    d = x_parts[0].shape[1]
    n_src = len(x_parts)
    assert len(a) == len(b) == len(c) == n_src
    tm = TOK_TILE
    n_lat_tiles = n_lat // tm
    per_batch = seq // tm
    n_mod = g1.shape[0]

    def mod_map(i):
        return (jnp.where(i < n_lat_tiles, i // per_batch, n_mod - 1), 0, 0)

    wa = w_out[0:MLSTM_W].astype(MXU_DTYPE)
    wb = w_out[MLSTM_W:MLSTM_W + NA_W].astype(MXU_DTYPE)
    wc = w_out[MLSTM_W + NA_W:].astype(MXU_DTYPE)
    rw = jnp.zeros((d, LANES), F32).at[:, :N_EXPERTS].set(router_w.astype(F32))
    rw_hi = rw.astype(BF16)
    rw_lo = (rw - rw_hi.astype(F32)).astype(BF16)
    rb = jnp.full((1, LANES), NEG_BIG, F32).at[0, :N_EXPERTS].set(router_b.astype(F32))
    full = lambda r, cc: pl.BlockSpec((r, cc), lambda i: (0, 0))
    tile = lambda cc: pl.BlockSpec((tm, cc), lambda i: (i, 0))
    return pl.pallas_call(
        functools.partial(_outproj_kernel, ROUTE_TILE, n_src, n_lat_tiles),
        grid=(n_rows // tm,),
        in_specs=[spec for parts in (a, b, c, x_parts) for spec in _token_specs(parts, tm, n_lat_tiles)] + [
                  pl.BlockSpec((1, 1, d), mod_map), pl.BlockSpec((1, 1, d), mod_map),
                  pl.BlockSpec((1, 1, d), mod_map), full(1, d),
                  full(MLSTM_W, d), full(NA_W, d), full(CONV_CH, d), full(d, LANES), full(d, LANES),
                  full(1, LANES)],
        out_specs=[tile(d), tile(d), tile(LANES), tile(LANES),
                   pl.BlockSpec((1, SUBLANES, LANES), lambda i: (i, 0, 0))],
        out_shape=[jax.ShapeDtypeStruct((n_rows, d), F32), jax.ShapeDtypeStruct((n_rows, d), F32),
                   jax.ShapeDtypeStruct((n_rows, LANES), jnp.int32), jax.ShapeDtypeStruct((n_rows, LANES), F32),
                   jax.ShapeDtypeStruct((n_rows // tm, SUBLANES, LANES), jnp.int32)],
        compiler_params=_cparams(("parallel",)),
        name="out_proj",
    )(*a, *b, *c, *x_parts, g1, sh2, sc2, norm_w.reshape(1, d).astype(F32), wa, wb, wc, rw_hi, rw_lo, rb)


def _dispatch_kernel(tm, bm, lo_ref, hi_ref, nu_ref, dest_ref, hx_ref, xs_out, zeros, sem, zsem):
    n_blocks = xs_out.shape[0] // bm

    def pad_copies():
        for e in range(N_EXPERTS):
            lo, hi = lo_ref[e], hi_ref[e]
            head = jnp.minimum((SUBLANES - lo % SUBLANES) % SUBLANES, hi - lo)
            for j in range(SUBLANES - 1):
                yield j < head, pltpu.make_async_copy(zeros.at[pl.ds(0, 1), :], xs_out.at[pl.ds(lo + j, 1), :], zsem)
            rest = hi - lo - head
            for chunk in ROW_CHUNKS:
                at = pl.multiple_of(lo + head + (rest & ~(2 * chunk - 1)), SUBLANES)
                yield (rest & chunk) != 0, pltpu.make_async_copy(
                    zeros.at[pl.ds(0, chunk), :], xs_out.at[pl.ds(at, chunk), :], zsem)

    def block_copy(b):
        return pltpu.make_async_copy(zeros, xs_out.at[pl.ds(pl.multiple_of(b * bm, bm), bm), :], zsem)

    def start_block(b, carry):
        block_copy(b).start()
        return carry

    def wait_block(b, carry):
        block_copy(b).wait()
        return carry

    @pl.when(pl.program_id(0) == 0)
    def _():
        zeros[...] = jnp.zeros(zeros.shape, zeros.dtype)
        for needed, copy in pad_copies():
            pl.when(needed)(copy.start)
        lax.fori_loop(nu_ref[0], n_blocks, start_block, 0)
        for needed, copy in pad_copies():
            pl.when(needed)(copy.wait)
        lax.fori_loop(nu_ref[0], n_blocks, wait_block, 0)

    def body(r, carry):
        for kk in range(TOP_K):
            dst = dest_ref[0, 0, kk * tm + r]
            pltpu.make_async_copy(hx_ref.at[pl.ds(r, 1), :], xs_out.at[pl.ds(dst, 1), :], sem).start(
                priority=kk % 2)
        return carry

    lax.fori_loop(0, tm, body, 0)
    for kk in range(TOP_K):
        pltpu.make_async_copy(hx_ref, xs_out.at[pl.ds(0, tm), :], sem).wait()


def _dispatch(hx, dest_tiles, pad_lo, pad_hi, n_used, n_blocks, tm):
    n, d = hx.shape
    bm = MOE_BLOCK
    grid_spec = pltpu.PrefetchScalarGridSpec(
        num_scalar_prefetch=3,
        grid=(n // tm,),
        in_specs=[pl.BlockSpec((1, 1, TOP_K * tm), lambda i, *_: (i, 0, 0), memory_space=pltpu.SMEM),
                  pl.BlockSpec((tm, d), lambda i, *_: (i, 0))],
        out_specs=pl.BlockSpec(memory_space=pl.ANY),
        scratch_shapes=[pltpu.VMEM((bm, d), hx.dtype), pltpu.SemaphoreType.DMA, pltpu.SemaphoreType.DMA],
    )
    return pl.pallas_call(
        functools.partial(_dispatch_kernel, tm, bm),
        grid_spec=grid_spec,
        out_shape=jax.ShapeDtypeStruct((n_blocks * bm, d), hx.dtype),
        compiler_params=_cparams(("arbitrary",)),
        name="moe_dispatch",
    )(pad_lo, pad_hi, n_used, dest_tiles, hx)


def _expert_kernel(be_ref, nu_ref, x_ref, w1_ref, b1_ref, w2_ref, b2_ref, o_ref, w1_s, w2_s):
    i = pl.program_id(0)
    de = w2_ref.shape[1]

    @pl.when(jnp.logical_or(i == 0, be_ref[i] != be_ref[jnp.maximum(i - 1, 0)]))
    def _():
        w1_s[...] = w1_ref[0].astype(w1_s.dtype)
        w2_s[...] = w2_ref[0].astype(w2_s.dtype)

    @pl.when(i < nu_ref[0])
    def _():
        h = _mm(x_ref[...], w1_s[...]) + b1_ref[0]
        glu = jnp.minimum(h[:, :de], SWIGLU_LIMIT)
        lin = jnp.clip(h[:, de:], -SWIGLU_LIMIT, SWIGLU_LIMIT)
        act = (lin + 1.0) * glu * jax.nn.sigmoid(SWIGLU_ALPHA * glu)
        o_ref[...] = _mm(act, w2_s[...]) + b2_ref[0]

    @pl.when(i >= nu_ref[0])
    def _():
        o_ref[...] = jnp.zeros(o_ref.shape, o_ref.dtype)


def _experts(xs, n_blocks, block_e, n_used, w1, b1, w2, b2):
    d = xs.shape[1]
    ne, _, two_de = w1.shape
    de = w2.shape[1]
    bm = MOE_BLOCK
    grid_spec = pltpu.PrefetchScalarGridSpec(
        num_scalar_prefetch=2,
        grid=(n_blocks,),
        in_specs=[pl.BlockSpec((bm, d), lambda i, *_: (i, 0)),
                  pl.BlockSpec((1, d, two_de), lambda i, be, *_: (be[i], 0, 0)),
                  pl.BlockSpec((1, 1, two_de), lambda i, be, *_: (be[i], 0, 0)),
                  pl.BlockSpec((1, de, d), lambda i, be, *_: (be[i], 0, 0)),
                  pl.BlockSpec((1, 1, d), lambda i, be, *_: (be[i], 0, 0))],
        out_specs=pl.BlockSpec((bm, d), lambda i, *_: (i, 0)),
        scratch_shapes=[pltpu.VMEM((d, two_de), MXU_DTYPE), pltpu.VMEM((de, d), MXU_DTYPE)],
    )
    return pl.pallas_call(
        _expert_kernel,
        grid_spec=grid_spec,
        out_shape=jax.ShapeDtypeStruct((n_blocks * bm, d), F32),
        compiler_params=_cparams(("arbitrary",)),
        name="moe_experts",
    )(block_e, n_used, xs, w1.astype(F32), b1.reshape(ne, 1, two_de).astype(F32), w2.astype(F32),
      b2.reshape(ne, 1, d).astype(F32))


def _combine_kernel(rt, src_ref, size_ref, soff_ref, eo_hbm, x_ref, pos_ref, post_ref, gatet_ref, g2_ref, xo_ref,
                    stage, sem):
    i = pl.program_id(0)
    slot = i % 2
    sr = stage.shape[1]

    def run_copies(tile, to_slot, op):
        for e in range(N_EXPERTS):
            base = tile * N_EXPERTS + e
            src, size, dst = src_ref[base], size_ref[base], soff_ref[base]

            def chunk_ops(chunks):
                for chunk in chunks:
                    done = size & ~(2 * chunk - 1)
                    copy = pltpu.make_async_copy(
                        eo_hbm.at[pl.ds(pl.multiple_of(src + done, SUBLANES), chunk), :],
                        stage.at[to_slot, pl.ds(pl.multiple_of(dst + done, SUBLANES), chunk), :],
                        sem.at[to_slot])
                    pl.when((size & chunk) != 0)(getattr(copy, op))

            pl.when(size >= LONG_RUN)(functools.partial(chunk_ops, [c for c in ROW_CHUNKS if c >= LONG_RUN]))
            chunk_ops([c for c in ROW_CHUNKS if c < LONG_RUN])

    @pl.when(i == 0)
    def _():
        stage[...] = jnp.zeros(stage.shape, stage.dtype)
        run_copies(0, 0, "start")

    @pl.when(i + 1 < pl.num_programs(0))
    def _():
        run_copies(i + 1, 1 - slot, "start")

    run_copies(i, slot, "wait")

    last = i * N_EXPERTS + N_EXPERTS - 1
    n_staged = soff_ref[last] + size_ref[last]
    pos_t = post_ref[0]
    gate_t = gatet_ref[0]
    row_t = lax.broadcasted_iota(jnp.int32, (sr, rt), 0)
    gate_sel = jnp.zeros((sr, rt), F32)
    for kk in range(TOP_K):
        gate_sel = jnp.where(row_t == pos_t[kk:kk + 1, :], gate_t[kk:kk + 1, :], gate_sel)
    row_gate = jnp.sum(gate_sel, axis=1, keepdims=True)
    row = lax.broadcasted_iota(jnp.int32, (sr, 1), 0)
    staged = jnp.where(row < n_staged, stage[slot], 0.0)
    staged = (staged * row_gate).astype(MXU_DTYPE)
    col = lax.broadcasted_iota(jnp.int32, (rt, sr), 1)
    pos = pos_ref[...]
    picked = col == pos[:, 0:1]
    for kk in range(1, TOP_K):
        picked = jnp.logical_or(picked, col == pos[:, kk:kk + 1])
    sel = jnp.where(picked, 1.0, 0.0).astype(MXU_DTYPE)
    y = jnp.dot(sel, staged, preferred_element_type=F32)
    xo_ref[...] = x_ref[...] + g2_ref[0] * y


def _combine(eo, tables, pos, xres, gates, g2, n_rows, n_lat, seq):
    d = xres.shape[1]
    rt = ROUTE_TILE
    n_lat_tiles = n_lat // rt
    per_batch = seq // rt
    n_mod = g2.shape[0]
    sr = TOP_K * rt + N_EXPERTS * 2 * (SUBLANES - 1)
    sr = -(-sr // MXU_TILE) * MXU_TILE

    def mod_map(i, *_):
        return (jnp.where(i < n_lat_tiles, i // per_batch, n_mod - 1), 0, 0)

    def by_tile(a):
        return a[:n_rows].reshape(n_rows // rt, rt, TOP_K).transpose(0, 2, 1)

    grid_spec = pltpu.PrefetchScalarGridSpec(
        num_scalar_prefetch=3,
        grid=(n_rows // rt,),
        in_specs=[pl.BlockSpec(memory_space=pl.ANY),
                  pl.BlockSpec((rt, d), lambda i, *_: (i, 0)),
                  pl.BlockSpec((rt, TOP_K), lambda i, *_: (i, 0)),
                  pl.BlockSpec((1, TOP_K, rt), lambda i, *_: (i, 0, 0)),
                  pl.BlockSpec((1, TOP_K, rt), lambda i, *_: (i, 0, 0)),
                  pl.BlockSpec((1, 1, d), mod_map)],
        out_specs=pl.BlockSpec((rt, d), lambda i, *_: (i, 0)),
        scratch_shapes=[pltpu.VMEM((2, sr, d), F32), pltpu.SemaphoreType.DMA((2,))],
    )
    return pl.pallas_call(
        functools.partial(_combine_kernel, rt),
        grid_spec=grid_spec,
        out_shape=jax.ShapeDtypeStruct((n_rows, d), F32),
        compiler_params=_cparams(("arbitrary",)),
        name="moe_combine",
    )(*tables, eo, xres, pos, by_tile(pos), by_tile(gates[:, :TOP_K]), g2)


def _dest_tiles(dest, tm):
    n = dest.shape[0]
    return dest.reshape(n // tm, tm, TOP_K).transpose(0, 2, 1).reshape(n // tm, 1, TOP_K * tm)


def _route(idx, counts, n_blocks):
    bm = MOE_BLOCK
    rt = ROUTE_TILE
    n = idx.shape[0]
    tile_before = jnp.cumsum(counts, axis=0) - counts
    total = jnp.sum(counts, axis=0)
    padded = (total + bm - 1) // bm * bm
    pad_end = jnp.cumsum(padded)
    pad_start = pad_end - padded
    first_row = tile_before + pad_start[None, :]
    early = first_row % SUBLANES
    size = jnp.where(counts > 0, (counts + early + SUBLANES - 1) // SUBLANES * SUBLANES, 0)
    stage_off = jnp.cumsum(size, axis=1) - size
    experts = idx[:, :TOP_K].reshape(n // rt, rt, TOP_K)
    rank = idx[:, TOP_K:2 * TOP_K]
    onehot = experts[..., None] == lax.broadcasted_iota(jnp.int32, (1, 1, 1, N_EXPERTS), 3)
    lookup = lambda tab: jnp.sum(jnp.where(onehot, tab[:, None, None, :], 0), axis=-1).reshape(n, TOP_K)
    dest = lookup(first_row) + rank
    pos = lookup(stage_off + early) + rank
    tables = tuple(t.reshape(-1).astype(jnp.int32) for t in (first_row - early, size, stage_off))
    block_start = jnp.arange(n_blocks, dtype=jnp.int32) * bm
    block_e = jnp.minimum(jnp.sum(pad_end[None, :] <= block_start[:, None], axis=1), N_EXPERTS - 1)
    n_used = (pad_end[-1] // bm).astype(jnp.int32).reshape(1)
    padding = ((pad_start + total).astype(jnp.int32), pad_end.astype(jnp.int32))
    return dest.astype(jnp.int32), pos.astype(jnp.int32), tables, block_e.astype(jnp.int32), n_used, padding


def _moe(hx, idx, counts, gates, xres, g2, layer, w1, b1, w2, b2, n_out_rows, n_lat, seq):
    n = hx.shape[0]
    n_blocks = -(-(n * TOP_K + N_EXPERTS * (MOE_BLOCK - 1)) // MOE_BLOCK) + 1
    per_tile = TOK_TILE // ROUTE_TILE
    counts = counts[:, :per_tile, :N_EXPERTS].reshape(n // ROUTE_TILE, N_EXPERTS)
    dest, pos, tables, block_e, n_used, (pad_lo, pad_hi) = _route(idx, counts, n_blocks)
    xs = _dispatch(hx, _dest_tiles(dest, TOK_TILE), pad_lo, pad_hi, n_used, n_blocks, TOK_TILE)
    eo = _experts(xs, n_blocks, block_e + layer * N_EXPERTS, n_used, w1, b1, w2, b2)
    n_rt = n_out_rows // ROUTE_TILE
    tables = tuple(t[:n_rt * N_EXPERTS] for t in tables)
    return _combine(eo, tables, pos[:n_out_rows], xres, gates, g2, n_out_rows, n_lat, seq)


def kernel(x, c, ctx, c_ctx, norm_mix_w, norm_ffn_w, w_ada, b_ada, w_in, mlstm_ig_b, mlstm_fg_b, mlstm_norm_w,
           na_qnorm_w, na_knorm_w, na_rpb, conv_w, conv_b, conv_ln_w, conv_ln_b, w_out, router_w, router_b,
           exp_w1, exp_b1, exp_w2, exp_b2):
    batch, seq, d = x.shape
    ctx_len = ctx.shape[1]
    depth = w_ada.shape[0]
    n_lat = batch * seq
    n_ctx = batch * ctx_len
    n_all = n_lat + n_ctx
    assert depth == 2
    assert seq % TOK_TILE == 0 and n_ctx % TOK_TILE == 0 and seq % ctx_len == 0
    assert (seq // GRID_W) % NA_QROWS == 0 and seq // GRID_W >= NA_KROWS

    mod_rows = -(-(batch + 1) // SUBLANES) * SUBLANES
    cc = jnp.zeros((mod_rows, d), F32).at[:batch].set(c).at[batch].set(c_ctx)
    mods = _ada(cc, w_ada, b_ada)[:, :batch + 1].reshape(depth, batch + 1, 1, 6, d)
    x_parts = (x.reshape(n_lat, d), ctx.reshape(n_ctx, d))
    rope = _rope_tables(seq)
    n_exp = exp_w1.shape[1]
    ew1 = exp_w1.reshape((depth * n_exp,) + exp_w1.shape[2:])
    eb1 = exp_b1.reshape((depth * n_exp,) + exp_b1.shape[2:])
    ew2 = exp_w2.reshape((depth * n_exp,) + exp_w2.shape[2:])
    eb2 = exp_b2.reshape((depth * n_exp,) + exp_b2.shape[2:])

    for l in range(depth):
        need_ctx = l < depth - 1
        sh1, sc1, g1, sh2, sc2, g2 = [mods[l, :, :, i, :] for i in range(6)]
        pa, kt, g, gt, pb, u = _inproj(x_parts, sh1, sc1, norm_mix_w[l], w_in[l], n_lat, seq)
        a_lat, a_ctx = _mlstm(pa, kt, g, gt, mlstm_ig_b[l], mlstm_fg_b[l], mlstm_norm_w[l], rope,
                              batch, seq, ctx_len, need_ctx)
        b_lat, b_ctx = _na(pb, na_qnorm_w[l], na_knorm_w[l], na_rpb[l], batch, seq, ctx_len, need_ctx)
        c_lat = _conv(u, conv_w[l], conv_b[l], conv_ln_w[l], conv_ln_b[l], 0, batch, seq)
        if need_ctx:
            c_ctx_out = _conv(u, conv_w[l], conv_b[l], conv_ln_w[l], conv_ln_b[l], n_lat // ctx_len, batch, ctx_len)
            a_all, b_all, c_all = (a_lat, a_ctx), (b_lat, b_ctx), (c_lat, c_ctx_out)
            n_rows = n_all
        else:
            a_all, b_all, c_all = (a_lat,), (b_lat,), (c_lat,)
            n_rows = n_lat
        xmid, hx, idx, gates, counts = _outproj(a_all, b_all, c_all, x_parts, g1, sh2, sc2, norm_ffn_w[l], w_out[l],
                                                router_w[l], router_b[l], n_rows, n_lat, seq)
        xall = _moe(hx, idx, counts, gates, xmid, g2, l, ew1, eb1, ew2, eb2, n_rows, n_lat, seq)
        x_parts = (xall,)
    return x_parts[0][:n_lat].reshape(batch, seq, d)
```

```python
import functools

import numpy as np
import jax
import jax.numpy as jnp
from jax import lax
from jax.experimental import pallas as pl
from jax.experimental.pallas import tpu as pltpu

F32 = jnp.float32
BF16 = jnp.bfloat16
MXU_DTYPE = BF16

GRID_W = 64
HEAD_DIM = 64
MLSTM_HEADS = 4
NA_HEADS = 8
CONV_CH = 256
MLSTM_W = MLSTM_HEADS * HEAD_DIM
NA_W = NA_HEADS * HEAD_DIM
NA_WIN_R = 8
NA_WIN_C = 16
CONV_WIDTH = 31
ROPE_BASE = 10000.0
N_EXPERTS = 32
TOP_K = 4
SWIGLU_LIMIT = 7.0
SWIGLU_ALPHA = 1.702
EPS = 1e-6

A_Q = 0
A_G = 4 * MLSTM_W
B_Q = A_G + 4 * MLSTM_HEADS
C_A = B_Q + 3 * NA_W
IN_COLS = C_A + 2 * CONV_CH

LANES = 128
SUBLANES = 8
MXU_TILE = 256
NEG_BIG = -1e30
LOG2_E = 1.4426950408889634
VMEM_LIMIT = 56 * 1024 * 1024

NA_QROWS = 4
NA_KROWS = NA_QROWS + NA_WIN_R - 1
TOK_TILE = 512
ADA_TILE = 512
MOE_BLOCK = 512
DISPATCH_TILE = 1024
ROUTE_TILE = 256
ROW_CHUNKS = (256, 128, 64, 32, 16, 8)
CONV_ROWS = 64
CONV_PAD = 16


def _mm(a, b):
    return jnp.dot(a.astype(MXU_DTYPE), b.astype(MXU_DTYPE), preferred_element_type=F32)


def _mm_nt(a, b):
    return lax.dot_general(a.astype(MXU_DTYPE), b.astype(MXU_DTYPE), (((1,), (1,)), ((), ())),
                           preferred_element_type=F32)


def _cparams(sem):
    return pltpu.CompilerParams(dimension_semantics=sem, vmem_limit_bytes=VMEM_LIMIT)


def _ada_kernel(c_ref, w_ref, b_ref, o_ref):
    cc = c_ref[...]
    s = cc * jax.nn.sigmoid(cc)
    o_ref[0] = _mm(s, w_ref[0]) + b_ref[0]


def _ada(cc, w_ada, b_ada):
    depth, d, n = w_ada.shape
    rows = cc.shape[0]
    tn = ADA_TILE
    return pl.pallas_call(
        _ada_kernel,
        grid=(depth, n // tn),
        in_specs=[pl.BlockSpec((rows, d), lambda l, j: (0, 0)),
                  pl.BlockSpec((1, d, tn), lambda l, j: (l, 0, j)),
                  pl.BlockSpec((1, 1, tn), lambda l, j: (l, 0, j))],
        out_specs=pl.BlockSpec((1, rows, tn), lambda l, j: (l, 0, j)),
        out_shape=jax.ShapeDtypeStruct((depth, rows, n), F32),
        compiler_params=_cparams(("parallel", "parallel")),
        name="ada_mod",
    )(cc, w_ada, b_ada.reshape(depth, 1, n))


def _token_specs(parts, tm, n_lat_tiles):
    cols = parts[0].shape[1]
    if len(parts) == 1:
        return [pl.BlockSpec((tm, cols), lambda i: (i, 0))]
    return [pl.BlockSpec((tm, cols), lambda i: (jnp.minimum(i, n_lat_tiles - 1), 0)),
            pl.BlockSpec((tm, cols), lambda i: (jnp.maximum(i - n_lat_tiles, 0), 0))]


def _token_tile(refs, n_lat_tiles):
    if len(refs) == 1:
        return refs[0][...]
    return jnp.where(pl.program_id(0) < n_lat_tiles, refs[0][...], refs[1][...])


def _inproj_kernel(n_x, n_lat_tiles, *refs):
    x_refs = refs[:n_x]
    (sh_ref, sc_ref, nw_ref, wa_ref, wkt_ref, wg_ref, wgt_ref, wb_ref, wc_ref,
     pa_ref, kt_ref, g_ref, gt_ref, pb_ref, u_ref) = refs[n_x:]
    x = _token_tile(x_refs, n_lat_tiles)
    ms = jnp.mean(x * x, axis=-1, keepdims=True)
    y = x * lax.rsqrt(ms + EPS) * nw_ref[...]
    h = (y * (1.0 + sc_ref[0]) + sh_ref[0]).astype(MXU_DTYPE)
    pa_ref[...] = _mm(h, wa_ref[...]).astype(pa_ref.dtype)
    kt_ref[...] = _mm_nt(wkt_ref[...], h).astype(kt_ref.dtype)
    g_ref[...] = _mm(h, wg_ref[...])
    gt_ref[...] = _mm_nt(wgt_ref[...], h)
    pb_ref[...] = _mm(h, wb_ref[...]).astype(pb_ref.dtype)
    pc = _mm(h, wc_ref[...])
    u_ref[...] = (pc[:, :CONV_CH] * jax.nn.sigmoid(pc[:, CONV_CH:])).astype(u_ref.dtype)


def _inproj(x_parts, shift, scale, norm_w, w_in, n_lat, seq):
    n = sum(p.shape[0] for p in x_parts)
    d = x_parts[0].shape[1]
    tm = TOK_TILE
    n_lat_tiles = n_lat // tm
    per_batch = seq // tm
    n_mod = shift.shape[0]

    def mod_map(i):
        return (jnp.where(i < n_lat_tiles, i // per_batch, n_mod - 1), 0, 0)

    wq, wk, wv, wo = (w_in[:, A_Q + j * MLSTM_W:A_Q + (j + 1) * MLSTM_W] for j in range(4))
    wv = jnp.pad(wv.reshape(d, MLSTM_HEADS, HEAD_DIM), ((0, 0), (0, 0), (0, LANES - HEAD_DIM)))
    wa = jnp.concatenate([wq, wv.reshape(d, MLSTM_HEADS * LANES), wo], axis=1).astype(MXU_DTYPE)
    wkt = wk.T.astype(MXU_DTYPE)
    wg = w_in[:, A_G:B_Q].astype(MXU_DTYPE)
    wb = w_in[:, B_Q:C_A].astype(MXU_DTYPE)
    wc = w_in[:, C_A:IN_COLS].astype(MXU_DTYPE)
    ng = B_Q - A_G
    full = lambda r, c: pl.BlockSpec((r, c), lambda i: (0, 0))
    return pl.pallas_call(
        functools.partial(_inproj_kernel, len(x_parts), n_lat_tiles),
        grid=(n // tm,),
        in_specs=_token_specs(x_parts, tm, n_lat_tiles) + [
                  pl.BlockSpec((1, 1, d), mod_map),
                  pl.BlockSpec((1, 1, d), mod_map),
                  full(1, d),
                  full(d, A_G), full(MLSTM_W, d), full(d, ng), full(ng, d), full(d, 3 * NA_W),
                  full(d, 2 * CONV_CH)],
        out_specs=[pl.BlockSpec((tm, A_G), lambda i: (i, 0)),
                   pl.BlockSpec((MLSTM_W, tm), lambda i: (0, i)),
                   pl.BlockSpec((tm, ng), lambda i: (i, 0)),
                   pl.BlockSpec((ng, tm), lambda i: (0, i)),
                   pl.BlockSpec((tm, 3 * NA_W), lambda i: (i, 0)),
                   pl.BlockSpec((tm, CONV_CH), lambda i: (i, 0))],
        out_shape=[jax.ShapeDtypeStruct((n, A_G), BF16),
                   jax.ShapeDtypeStruct((MLSTM_W, n), BF16),
                   jax.ShapeDtypeStruct((n, ng), F32),
                   jax.ShapeDtypeStruct((ng, n), F32),
                   jax.ShapeDtypeStruct((n, 3 * NA_W), BF16),
                   jax.ShapeDtypeStruct((n, CONV_CH), BF16)],
        compiler_params=_cparams(("parallel",)),
        name="in_proj",
    )(*x_parts, shift, scale, norm_w.reshape(1, d), wa, wkt, wg, wg.T, wb, wc)


def _split3(x):
    hi = x.astype(BF16)
    r1 = x - hi.astype(F32)
    mid = r1.astype(BF16)
    lo = (r1 - mid.astype(F32)).astype(BF16)
    return hi, mid, lo


def _tri_left(tri, x):
    return sum(jnp.dot(tri, p, preferred_element_type=F32) for p in _split3(x))


def _tri_right(x, tri):
    return sum(jnp.dot(p, tri, preferred_element_type=F32) for p in _split3(x))


def _log_sigmoid(x):
    return jnp.minimum(x, 0.0) - jnp.log(1.0 + jnp.exp(-jnp.abs(x)))


def _mlstm_direction(z, q, kt, vx, gcol, grow, c_st, m_st, h_ref, row0, lc, with_output):
    nh = MLSTM_HEADS
    ti = lax.broadcasted_iota(jnp.int32, (lc, lc), 0)
    si = lax.broadcasted_iota(jnp.int32, (lc, lc), 1)
    lower = si <= ti
    upper = si >= ti
    tl = jnp.where(lower, 1.0, 0.0).astype(BF16)
    tu = jnp.where(upper, 1.0, 0.0).astype(BF16)
    f_col = _log_sigmoid(gcol[:, 2 * nh + z * nh:2 * nh + (z + 1) * nh])
    i_row = grow[z * nh:(z + 1) * nh, :]
    f_row = _log_sigmoid(grow[2 * nh + z * nh:2 * nh + (z + 1) * nh, :])
    if z == 0:
        b_col = _tri_left(tl, f_col)
        b_row = _tri_right(f_row, tu)
        b_tot = b_col[lc - 1:lc, :]
        mask = lower
    else:
        b_col = _tri_left(tu, f_col)
        b_row = _tri_right(f_row, tl)
        b_tot = b_col[0:1, :]
        mask = upper
    a_row = i_row - b_row
    a_max = jnp.max(a_row, axis=1, keepdims=True)
    lane256 = lax.broadcasted_iota(jnp.int32, (1, MLSTM_W), 1)
    lane128 = lax.broadcasted_iota(jnp.int32, (1, LANES), 1)
    kt_mx = kt.astype(MXU_DTYPE)
    c_all = c_st[z].astype(MXU_DTYPE)
    for h in range(nh):
        r = z * nh + h
        ar = a_row[h:h + 1, :]
        bl = b_tot[:, h:h + 1]
        m_old = m_st[r:r + 1, 0:1]
        vh = vx[:, h * LANES:(h + 1) * LANES]
        vext = jnp.where(lane128 == HEAD_DIM, jnp.ones((), vh.dtype), vh).astype(MXU_DTYPE)
        if with_output:
            qh = jnp.where(lane256 // HEAD_DIM == h, q, 0.0).astype(MXU_DTYPE)
            am = jnp.where(mask, ar, NEG_BIG)
            g = jnp.maximum(m_old, jnp.max(am, axis=1, keepdims=True))
            s = _mm(qh, kt_mx) * jnp.exp(am - g)
            w_inter = jnp.exp(m_old - g)
            nd = _mm(s, vext) + w_inter * _mm(qh, c_all)
            den = nd[:, HEAD_DIM:HEAD_DIM + 1]
            hval = nd / jnp.maximum(jnp.abs(den), jnp.exp(-(b_col[:, h:h + 1] + g)))
            h_ref[pl.ds(row0, lc), h * LANES:(h + 1) * LANES] = hval
        m_new = bl + jnp.maximum(m_old, a_max[h:h + 1, :])
        w_row = jnp.exp(bl + ar - m_new)
        decay = jnp.exp(bl + m_old - m_new)
        rows = slice(h * HEAD_DIM, (h + 1) * HEAD_DIM)
        c_st[z, rows, :] = decay * c_st[z, rows, :] + _mm(kt[rows, :] * w_row, vext)
        m_st[r:r + 1, :] = jnp.broadcast_to(m_new, (1, LANES))


def _mlstm_kernel(lc, nc, need_ctx,
                  pa_c, kt_c, g_c, gt_c,
                  pa_f, kt_f, g_f, gt_f, cos_f, sin_f, cost_f, sint_f,
                  pa_b, kt_b, g_b, gt_b, cos_b, sin_b, cost_b, sint_b,
                  o_lat, o_ctx, brow_ref, bcol_ref, nw_ref, perm_ref,
                  *rest):
    if need_ctx:
        out_lat, out_ctx, hf, hb, c_st, m_st = rest
    else:
        out_lat, hf, hb, c_st, m_st = rest
        out_ctx = None
    s = pl.program_id(1)
    w = MLSTM_W
    k_scale = HEAD_DIM ** -0.5

    def load(pa, kt_ref, rope_refs):
        q = pa[:, 0:w].astype(F32)
        kt = kt_ref[...].astype(F32)
        vx = pa[:, w:w + MLSTM_HEADS * LANES]
        if rope_refs is not None:
            cos_ref, sin_ref, cost_ref, sint_ref = rope_refs
            q = q * cos_ref[...] + _mm(q, perm_ref[...]) * sin_ref[...]
            blk = HEAD_DIM // 4
            swapped = jnp.concatenate([kt[(i ^ 1) * blk:((i ^ 1) + 1) * blk, :] for i in range(w // blk)], axis=0)
            kt = kt * cost_ref[...] + swapped * sint_ref[...]
        return q, kt * k_scale, vx

    @pl.when(s == 0)
    def _():
        c_st[...] = jnp.zeros(c_st.shape, F32)
        m_st[...] = jnp.zeros(m_st.shape, F32)
        q, kt, vx = load(pa_c, kt_c, None)
        gcol = g_c[...] + brow_ref[...]
        grow = gt_c[...] + bcol_ref[...]
        for z, h_ref in ((0, hf), (1, hb)):
            _mlstm_direction(z, q, kt, vx, gcol, grow, c_st, m_st, h_ref, 0, lc, need_ctx)

    @pl.when(s > 0)
    def _():
        for z, h_ref, refs in ((0, hf, (pa_f, kt_f, g_f, gt_f, (cos_f, sin_f, cost_f, sint_f))),
                               (1, hb, (pa_b, kt_b, g_b, gt_b, (cos_b, sin_b, cost_b, sint_b)))):
            pa, kt_ref, g, gt, rope_refs = refs
            j = s - 1 if z == 0 else nc - s
            row0 = pl.multiple_of(lc + j * lc, lc)
            q, kt, vx = load(pa, kt_ref, rope_refs)
            gcol = g[...] + brow_ref[...]
            grow = gt[...] + bcol_ref[...]
            _mlstm_direction(z, q, kt, vx, gcol, grow, c_st, m_st, h_ref, row0, lc, True)

    @pl.when(s == nc)
    def _():
        lane128 = lax.broadcasted_iota(jnp.int32, (1, LANES), 1)
        mean_w = jnp.where(lax.broadcasted_iota(jnp.int32, (LANES, LANES), 0) < HEAD_DIM,
                           1.0 / HEAD_DIM, 0.0).astype(BF16)
        first = 0 if need_ctx else 1
        for ch in range(first, nc + 1):
            rows = slice(ch * lc, (ch + 1) * lc)
            if ch == 0:
                o_val, dst, dst_rows = o_ctx[...], out_ctx, slice(0, lc)
            else:
                dst_rows = slice((ch - 1) * lc, ch * lc)
                o_val, dst = o_lat[dst_rows, :], out_lat
            o_val = o_val.astype(F32)
            for p in range(MLSTM_HEADS // 2):
                pair = []
                for h in (2 * p, 2 * p + 1):
                    hv = hf[rows, h * LANES:(h + 1) * LANES] + hb[rows, h * LANES:(h + 1) * LANES]
                    hv = jnp.where(lane128 < HEAD_DIM, hv, 0.0)
                    sq = hv * hv
                    sq_hi = sq.astype(BF16)
                    sq_lo = (sq - sq_hi.astype(F32)).astype(BF16)
                    ms = (jnp.dot(sq_hi, mean_w, preferred_element_type=F32)
                          + jnp.dot(sq_lo, mean_w, preferred_element_type=F32))
                    pair.append(hv * lax.rsqrt(ms + EPS))
                packed = jnp.where(lane128 < HEAD_DIM, pair[0], pltpu.roll(pair[1], HEAD_DIM, 1))
                cols = slice(p * LANES, (p + 1) * LANES)
                res = packed * nw_ref[:, cols] * jax.nn.sigmoid(o_val[:, cols])
                dst[dst_rows, cols] = res.astype(dst.dtype)


def _rope_tables(seq):
    half = HEAD_DIM // 2
    quarter = half // 2
    t = jnp.arange(seq, dtype=jnp.int32)
    inv_freq = ROPE_BASE ** (-jnp.arange(quarter, dtype=F32) / quarter)
    parts_c, parts_s = [], []
    for pos in (t // GRID_W, t % GRID_W):
        ang = pos.astype(F32)[:, None] * inv_freq[None, :]
        parts_c += [jnp.cos(ang), jnp.cos(ang)]
        parts_s += [-jnp.sin(ang), jnp.sin(ang)]
    cos = jnp.tile(jnp.concatenate(parts_c, axis=-1), (1, MLSTM_HEADS))
    sin = jnp.tile(jnp.concatenate(parts_s, axis=-1), (1, MLSTM_HEADS))
    j = np.arange(MLSTM_W)
    partner = np.where(j % half < quarter, j + quarter, j - quarter)
    perm = np.zeros((MLSTM_W, MLSTM_W), np.float32)
    perm[partner, j] = 1.0
    return cos, sin, cos.T, sin.T, jnp.asarray(perm, dtype=MXU_DTYPE)


def _mlstm(pa, kt, g, gt, ig_b, fg_b, norm_w, rope, batch, seq, ctx_len, need_ctx):
    lc = ctx_len
    nc = seq // lc
    n_lat = batch * seq
    cos, sin, cos_t, sin_t, perm = rope
    bias = jnp.concatenate([ig_b.reshape(-1), fg_b.reshape(-1)]).astype(F32)
    ng = bias.shape[0]
    lat_blocks = n_lat // lc

    def fwd(b, s):
        return b * nc + jnp.maximum(s - 1, 0)

    def bwd(b, s):
        return b * nc + nc - jnp.maximum(s, 1)

    def fwd_c(b, s):
        return jnp.maximum(s - 1, 0)

    def bwd_c(b, s):
        return nc - jnp.maximum(s, 1)

    def lat_specs(chunk, chunk_c):
        return [pl.BlockSpec((lc, A_G), lambda b, s: (chunk(b, s), 0)),
                pl.BlockSpec((MLSTM_W, lc), lambda b, s: (0, chunk(b, s))),
                pl.BlockSpec((lc, ng), lambda b, s: (chunk(b, s), 0)),
                pl.BlockSpec((ng, lc), lambda b, s: (0, chunk(b, s))),
                pl.BlockSpec((lc, MLSTM_W), lambda b, s: (chunk_c(b, s), 0)),
                pl.BlockSpec((lc, MLSTM_W), lambda b, s: (chunk_c(b, s), 0)),
                pl.BlockSpec((MLSTM_W, lc), lambda b, s: (0, chunk_c(b, s))),
                pl.BlockSpec((MLSTM_W, lc), lambda b, s: (0, chunk_c(b, s)))]

    in_specs = ([pl.BlockSpec((lc, A_G), lambda b, s: (lat_blocks + b, 0)),
                 pl.BlockSpec((MLSTM_W, lc), lambda b, s: (0, lat_blocks + b)),
                 pl.BlockSpec((lc, ng), lambda b, s: (lat_blocks + b, 0)),
                 pl.BlockSpec((ng, lc), lambda b, s: (0, lat_blocks + b))]
                + lat_specs(fwd, fwd_c) + lat_specs(bwd, bwd_c)
                + [pl.BlockSpec((seq, MLSTM_W), lambda b, s: (b, 3)),
                   pl.BlockSpec((lc, MLSTM_W), lambda b, s: (lat_blocks + b, 3)),
                   pl.BlockSpec((1, ng), lambda b, s: (0, 0)),
                   pl.BlockSpec((ng, 1), lambda b, s: (0, 0)),
                   pl.BlockSpec((1, MLSTM_W), lambda b, s: (0, 0)),
                   pl.BlockSpec((MLSTM_W, MLSTM_W), lambda b, s: (0, 0))])
    out_specs = [pl.BlockSpec((seq, MLSTM_W), lambda b, s: (b, 0))]
    out_shape = [jax.ShapeDtypeStruct((n_lat, MLSTM_W), BF16)]
    if need_ctx:
        out_specs.append(pl.BlockSpec((lc, MLSTM_W), lambda b, s: (b, 0)))
        out_shape.append(jax.ShapeDtypeStruct((batch * ctx_len, MLSTM_W), BF16))
    t_all = ctx_len + seq
    outs = pl.pallas_call(
        functools.partial(_mlstm_kernel, lc, nc, need_ctx),
        grid=(batch, nc + 1),
        in_specs=in_specs,
        out_specs=out_specs,
        out_shape=out_shape,
        scratch_shapes=[pltpu.VMEM((t_all, MLSTM_HEADS * LANES), F32),
                        pltpu.VMEM((t_all, MLSTM_HEADS * LANES), F32),
                        pltpu.VMEM((2, MLSTM_W, LANES), F32),
                        pltpu.VMEM((2 * MLSTM_HEADS, LANES), F32)],
        compiler_params=_cparams(("parallel", "arbitrary")),
        name="mlstm",
    )(pa, kt, g, gt, pa, kt, g, gt, cos, sin, cos_t, sin_t, pa, kt, g, gt, cos, sin, cos_t, sin_t, pa, pa,
      bias.reshape(1, ng), bias.reshape(ng, 1), norm_w.reshape(1, MLSTM_W).astype(F32), perm)
    return (outs[0], outs[1]) if need_ctx else (outs[0], None)


def _na_patterns(n_rows):
    kr = min(NA_WIN_R, n_rows)
    n_dr = 2 * NA_WIN_R - 1
    pats, pat_ids, bases = [], [], []
    for gi in range(n_rows // NA_QROWS):
        base = int(np.clip(NA_QROWS * gi - NA_WIN_R // 2, 0, n_rows - NA_KROWS))
        dr = np.full((NA_QROWS, NA_KROWS), n_dr, np.int32)
        for qr in range(NA_QROWS):
            r = NA_QROWS * gi + qr
            r0 = int(np.clip(r - kr // 2, 0, n_rows - kr))
            for kj in range(NA_KROWS):
                if r0 <= base + kj < r0 + kr:
                    dr[qr, kj] = base + kj - r + NA_WIN_R - 1
        for pi, p in enumerate(pats):
            if np.array_equal(p, dr):
                pat_ids.append(pi)
                break
        else:
            pat_ids.append(len(pats))
            pats.append(dr)
        bases.append(base)
    return tuple(pat_ids), tuple(bases), np.stack(pats)


def _na_bias_table(rpb, row_idx):
    heads = rpb.shape[0]
    col = np.arange(GRID_W)
    col_start = np.clip(col - NA_WIN_C // 2, 0, GRID_W - NA_WIN_C)
    in_win = (col[None, :] >= col_start[:, None]) & (col[None, :] < col_start[:, None] + NA_WIN_C)
    dc = np.clip(col[None, :] - col[:, None] + NA_WIN_C - 1, 0, 2 * NA_WIN_C - 2)
    onehot = (dc[None] == np.arange(2 * NA_WIN_C - 1)[:, None, None]).astype(np.float32)
    planes = jnp.einsum('hdc,cqk->hdqk', rpb, onehot, precision=lax.Precision.HIGHEST)
    planes = jnp.where(in_win[None, None], planes, NEG_BIG)
    planes = jnp.concatenate([planes, jnp.full((heads, 1, GRID_W, GRID_W), NEG_BIG, F32)], axis=1)
    npat = row_idx.shape[0]
    tab = planes[:, row_idx.reshape(-1)].reshape(heads, npat, NA_QROWS, NA_KROWS, GRID_W, GRID_W)
    return tab.transpose(0, 1, 2, 4, 3, 5).reshape(heads, npat, NA_QROWS * GRID_W, NA_KROWS * GRID_W)


def _na_kernel(pat_ids, bases, need_ctx, q_ref, k_ref, v_ref, kc_ref, vc_ref, *rest):
    if need_ctx:
        qc_ref, bias_ref, qw_ref, kw_ref, out_ref, outc_ref, kn_s, kcn_s = rest
    else:
        bias_ref, qw_ref, kw_ref, out_ref, kn_s, kcn_s = rest
    lane = lax.broadcasted_iota(jnp.int32, (1, LANES), 1)
    low = lane < HEAD_DIM
    inv_d = 1.0 / HEAD_DIM

    def rmsn(x, w):
        x2 = x * x
        s0 = jnp.sum(jnp.where(low, x2, 0.0), axis=-1, keepdims=True)
        s1 = jnp.sum(jnp.where(low, 0.0, x2), axis=-1, keepdims=True)
        r = jnp.where(low, lax.rsqrt(s0 * inv_d + EPS), lax.rsqrt(s1 * inv_d + EPS))
        return x * r * w

    qw = qw_ref[...]
    kn_s[...] = rmsn(k_ref[...].astype(F32), kw_ref[...]).astype(kn_s.dtype)
    kcn_s[...] = rmsn(kc_ref[...].astype(F32), kw_ref[...]).astype(kcn_s.dtype)
    kcn = kcn_s[...]
    vc = vc_ref[...]
    scale = HEAD_DIM ** -0.5 * LOG2_E
    nq = NA_QROWS * GRID_W
    nk = NA_KROWS * GRID_W

    def attend(qn, parts):
        outs = []
        for hh in range(2):
            qh = jnp.where(low if hh == 0 else jnp.logical_not(low), qn, 0.0).astype(MXU_DTYPE)
            scores = []
            for keys, _, bias in parts:
                sc = _mm_nt(qh, keys)
                if bias is not None:
                    sc = sc + bias[hh]
                scores.append(sc)
            m = scores[0].max(axis=-1, keepdims=True)
            for sc in scores[1:]:
                m = jnp.maximum(m, sc.max(axis=-1, keepdims=True))
            acc = None
            den = None
            for sc, (_, vals, _) in zip(scores, parts):
                p = jnp.exp2(sc - m)
                d = jnp.sum(p, axis=-1, keepdims=True)
                o = _mm(p, vals)
                acc = o if acc is None else acc + o
                den = d if den is None else den + d
            outs.append(acc / den)
        return jnp.where(low, outs[0], outs[1])

    for gi, (pid, base) in enumerate(zip(pat_ids, bases)):
        qn = rmsn(q_ref[gi * nq:(gi + 1) * nq, :].astype(F32), qw) * scale
        kwin = kn_s[base * GRID_W:base * GRID_W + nk, :]
        vwin = v_ref[base * GRID_W:base * GRID_W + nk, :]
        bias = (bias_ref[0, pid], bias_ref[1, pid])
        res = attend(qn, [(kwin, vwin, bias), (kcn, vc, None)])
        out_ref[gi * nq:(gi + 1) * nq, :] = res.astype(out_ref.dtype)

    if need_ctx:
        qn = rmsn(qc_ref[...].astype(F32), qw) * scale
        outc_ref[...] = attend(qn, [(kcn, vc, None)]).astype(outc_ref.dtype)


def _na(pb, qn_w, kn_w, rpb, batch, seq, ctx_len, need_ctx):
    n_rows = seq // GRID_W
    pat_ids, bases, row_idx = _na_patterns(n_rows)
    npat = row_idx.shape[0]
    nq, nk = NA_QROWS * GRID_W, NA_KROWS * GRID_W
    bias = _na_bias_table(rpb.astype(F32) * LOG2_E, row_idx)
    n_lat = batch * seq
    pairs = NA_HEADS // 2
    qoff, koff, voff = 0, pairs, 2 * pairs

    in_specs = [pl.BlockSpec((seq, LANES), lambda p, b: (b, qoff + p)),
                pl.BlockSpec((seq, LANES), lambda p, b: (b, koff + p)),
                pl.BlockSpec((seq, LANES), lambda p, b: (b, voff + p)),
                pl.BlockSpec((ctx_len, LANES), lambda p, b: (n_lat // ctx_len + b, koff + p)),
                pl.BlockSpec((ctx_len, LANES), lambda p, b: (n_lat // ctx_len + b, voff + p))]
    args = [pb, pb, pb, pb, pb]
    if need_ctx:
        in_specs.append(pl.BlockSpec((ctx_len, LANES), lambda p, b: (n_lat // ctx_len + b, qoff + p)))
        args.append(pb)
    in_specs += [pl.BlockSpec((2, npat, nq, nk), lambda p, b: (p, 0, 0, 0)),
                 pl.BlockSpec((1, LANES), lambda p, b: (0, 0)),
                 pl.BlockSpec((1, LANES), lambda p, b: (0, 0))]
    args += [bias, jnp.tile(qn_w.astype(F32), 2).reshape(1, LANES), jnp.tile(kn_w.astype(F32), 2).reshape(1, LANES)]
    out_specs = [pl.BlockSpec((seq, LANES), lambda p, b: (b, p))]
    out_shape = [jax.ShapeDtypeStruct((n_lat, NA_W), BF16)]
    if need_ctx:
        out_specs.append(pl.BlockSpec((ctx_len, LANES), lambda p, b: (b, p)))
        out_shape.append(jax.ShapeDtypeStruct((batch * ctx_len, NA_W), BF16))
    outs = pl.pallas_call(
        functools.partial(_na_kernel, pat_ids, bases, need_ctx),
        grid=(pairs, batch),
        in_specs=in_specs,
        out_specs=out_specs,
        out_shape=out_shape,
        scratch_shapes=[pltpu.VMEM((seq, LANES), MXU_DTYPE), pltpu.VMEM((ctx_len, LANES), MXU_DTYPE)],
        compiler_params=_cparams(("parallel", "parallel")),
        name="na_attn",
    )(*args)
    return (outs[0], outs[1]) if need_ctx else (outs[0], None)


def _conv_kernel(t_len, u_ref, w_ref, cb_ref, lw_ref, lb_ref, o_ref, pad_s):
    zeros = jnp.zeros((CONV_PAD, CONV_CH), F32)
    pad_s[0:CONV_PAD, :] = zeros
    pad_s[CONV_PAD + t_len:2 * CONV_PAD + t_len, :] = zeros
    pad_s[CONV_PAD:CONV_PAD + t_len, :] = u_ref[...].astype(F32)
    shift = CONV_PAD - CONV_WIDTH // 2

    def body(c, carry):
        r0 = pl.multiple_of(c * CONV_ROWS, CONV_ROWS)
        n_win = CONV_ROWS + 2 * CONV_PAD
        win = pad_s[pl.ds(r0, n_win), :]
        rot = [win] + [pltpu.roll(win, n_win - ph, 0) for ph in range(1, SUBLANES)]
        acc = jnp.zeros((CONV_ROWS, CONV_CH), F32) + cb_ref[...]
        for j in range(CONV_WIDTH):
            ph, al = (j + shift) % SUBLANES, (j + shift) // SUBLANES * SUBLANES
            acc = acc + rot[ph][al:al + CONV_ROWS, :] * w_ref[j:j + 1, :]
        mean = jnp.mean(acc, axis=-1, keepdims=True)
        xc = acc - mean
        var = jnp.mean(xc * xc, axis=-1, keepdims=True)
        y = xc * lax.rsqrt(var + EPS) * lw_ref[...] + lb_ref[...]
        o_ref[pl.ds(r0, CONV_ROWS), :] = (y * jax.nn.sigmoid(y)).astype(o_ref.dtype)
        return carry

    lax.fori_loop(0, t_len // CONV_ROWS, body, 0)


def _conv(u, conv_w, conv_b, ln_w, ln_b, first_block, n_seq, t_len):
    row = lambda a: a.reshape(1, CONV_CH).astype(F32)
    return pl.pallas_call(
        functools.partial(_conv_kernel, t_len),
        grid=(n_seq,),
        in_specs=[pl.BlockSpec((t_len, CONV_CH), lambda b: (first_block + b, 0)),
                  pl.BlockSpec((CONV_WIDTH, CONV_CH), lambda b: (0, 0)),
                  pl.BlockSpec((1, CONV_CH), lambda b: (0, 0)),
                  pl.BlockSpec((1, CONV_CH), lambda b: (0, 0)),
                  pl.BlockSpec((1, CONV_CH), lambda b: (0, 0))],
        out_specs=pl.BlockSpec((t_len, CONV_CH), lambda b: (b, 0)),
        out_shape=jax.ShapeDtypeStruct((n_seq * t_len, CONV_CH), BF16),
        scratch_shapes=[pltpu.VMEM((t_len + 2 * CONV_PAD, CONV_CH), F32)],
        compiler_params=_cparams(("parallel",)),
        name="conv_module",
    )(u, conv_w.astype(F32), row(conv_b), row(ln_w), row(ln_b))


def _outproj_kernel(rt, n_src, n_lat_tiles, *refs):
    a_refs, b_refs, c_refs, x_refs = (refs[j * n_src:(j + 1) * n_src] for j in range(4))
    (g1_ref, sh_ref, sc_ref, nw_ref, wa_ref, wb_ref, wc_ref, rwh_ref, rwl_ref, rb_ref,
     xo_ref, hx_ref, idx_ref, gate_ref, cnt_ref) = refs[4 * n_src:]
    tile = lambda parts: _token_tile(parts, n_lat_tiles)
    mix = _mm(tile(a_refs), wa_ref[...]) + _mm(tile(b_refs), wb_ref[...]) + _mm(tile(c_refs), wc_ref[...])
    xn = tile(x_refs) + g1_ref[0] * mix
    xo_ref[...] = xn
    ms = jnp.mean(xn * xn, axis=-1, keepdims=True)
    hx = xn * lax.rsqrt(ms + EPS) * nw_ref[...] * (1.0 + sc_ref[0]) + sh_ref[0]
    hx_ref[...] = hx
    h_hi = hx.astype(BF16)
    h_lo = (hx - h_hi.astype(F32)).astype(BF16)
    logits = (jnp.dot(h_hi, rwh_ref[...], preferred_element_type=F32)
              + jnp.dot(h_lo, rwh_ref[...], preferred_element_type=F32)
              + jnp.dot(h_hi, rwl_ref[...], preferred_element_type=F32)) + rb_ref[...]
    lane = lax.broadcasted_iota(jnp.int32, logits.shape, 1)
    idx_out = jnp.zeros(logits.shape, jnp.int32)
    val_out = jnp.zeros(logits.shape, F32)
    top = None
    den = None
    sels = []
    for kk in range(TOP_K):
        m = jnp.max(logits, axis=-1, keepdims=True)
        sel = jnp.min(jnp.where(logits == m, lane, LANES), axis=-1, keepdims=True)
        if kk == 0:
            top = m
        e = jnp.exp(m - top)
        den = e if den is None else den + e
        idx_out = jnp.where(lane == kk, sel, idx_out)
        val_out = jnp.where(lane == kk, e, val_out)
        logits = jnp.where(lane == sel, -jnp.inf, logits)
        sels.append(sel)
    gate_ref[...] = val_out / den
    tm = logits.shape[0]
    chosen = jnp.where(logits == -jnp.inf, 1.0, 0.0)
    ti = lax.broadcasted_iota(jnp.int32, (tm, tm), 0)
    si = lax.broadcasted_iota(jnp.int32, (tm, tm), 1)
    earlier = jnp.where(jnp.logical_and(si < ti, si // rt == ti // rt), 1.0, 0.0).astype(BF16)
    before = jnp.dot(earlier, chosen.astype(BF16), preferred_element_type=F32)
    for kk in range(TOP_K):
        rank = jnp.sum(jnp.where(lane == sels[kk], before, 0.0), axis=-1, keepdims=True)
        idx_out = jnp.where(lane == TOP_K + kk, rank.astype(jnp.int32), idx_out)
    idx_ref[...] = idx_out
    row = lax.broadcasted_iota(jnp.int32, cnt_ref.shape[1:], 0)
    cnt = jnp.zeros(cnt_ref.shape[1:], F32)
    for sub in range(tm // rt):
        cnt = jnp.where(row == sub, jnp.sum(chosen[sub * rt:(sub + 1) * rt], axis=0, keepdims=True), cnt)
    cnt_ref[0] = cnt.astype(jnp.int32)


def _outproj(a, b, c, x_parts, g1, sh2, sc2, norm_w, w_out, router_w, router_b, n_rows, n_lat, seq):
    d = x_parts[0].shape[1]
    n_src = len(x_parts)
    assert len(a) == len(b) == len(c) == n_src
    tm = TOK_TILE
    n_lat_tiles = n_lat // tm
    per_batch = seq // tm
    n_mod = g1.shape[0]

    def mod_map(i):
        return (jnp.where(i < n_lat_tiles, i // per_batch, n_mod - 1), 0, 0)

    wa = w_out[0:MLSTM_W].astype(MXU_DTYPE)
    wb = w_out[MLSTM_W:MLSTM_W + NA_W].astype(MXU_DTYPE)
    wc = w_out[MLSTM_W + NA_W:].astype(MXU_DTYPE)
    rw = jnp.zeros((d, LANES), F32).at[:, :N_EXPERTS].set(router_w.astype(F32))
    rw_hi = rw.astype(BF16)
    rw_lo = (rw - rw_hi.astype(F32)).astype(BF16)
    rb = jnp.full((1, LANES), NEG_BIG, F32).at[0, :N_EXPERTS].set(router_b.astype(F32))
    full = lambda r, cc: pl.BlockSpec((r, cc), lambda i: (0, 0))
    tile = lambda cc: pl.BlockSpec((tm, cc), lambda i: (i, 0))
    return pl.pallas_call(
        functools.partial(_outproj_kernel, ROUTE_TILE, n_src, n_lat_tiles),
        grid=(n_rows // tm,),
        in_specs=[spec for parts in (a, b, c, x_parts) for spec in _token_specs(parts, tm, n_lat_tiles)] + [
                  pl.BlockSpec((1, 1, d), mod_map), pl.BlockSpec((1, 1, d), mod_map),
                  pl.BlockSpec((1, 1, d), mod_map), full(1, d),
                  full(MLSTM_W, d), full(NA_W, d), full(CONV_CH, d), full(d, LANES), full(d, LANES),
                  full(1, LANES)],
        out_specs=[tile(d), tile(d), tile(LANES), tile(LANES),
                   pl.BlockSpec((1, SUBLANES, LANES), lambda i: (i, 0, 0))],
        out_shape=[jax.ShapeDtypeStruct((n_rows, d), F32), jax.ShapeDtypeStruct((n_rows, d), F32),
                   jax.ShapeDtypeStruct((n_rows, LANES), jnp.int32), jax.ShapeDtypeStruct((n_rows, LANES), F32),
                   jax.ShapeDtypeStruct((n_rows // tm, SUBLANES, LANES), jnp.int32)],
        compiler_params=_cparams(("parallel",)),
        name="out_proj",
    )(*a, *b, *c, *x_parts, g1, sh2, sc2, norm_w.reshape(1, d).astype(F32), wa, wb, wc, rw_hi, rw_lo, rb)


def _dispatch_kernel(tm, bm, lo_ref, hi_ref, nu_ref, dest_ref, hx_ref, xs_out, zeros, sem, zsem):
    n_blocks = xs_out.shape[0] // bm

    def pad_copies():
        for e in range(N_EXPERTS):
            lo, hi = lo_ref[e], hi_ref[e]
            head = jnp.minimum((SUBLANES - lo % SUBLANES) % SUBLANES, hi - lo)
            for j in range(SUBLANES - 1):
                yield j < head, pltpu.make_async_copy(zeros.at[pl.ds(0, 1), :], xs_out.at[pl.ds(lo + j, 1), :], zsem)
            rest = hi - lo - head
            for chunk in ROW_CHUNKS:
                at = pl.multiple_of(lo + head + (rest & ~(2 * chunk - 1)), SUBLANES)
                yield (rest & chunk) != 0, pltpu.make_async_copy(
                    zeros.at[pl.ds(0, chunk), :], xs_out.at[pl.ds(at, chunk), :], zsem)

    def block_copy(b):
        return pltpu.make_async_copy(zeros, xs_out.at[pl.ds(pl.multiple_of(b * bm, bm), bm), :], zsem)

    def start_block(b, carry):
        block_copy(b).start()
        return carry

    def wait_block(b, carry):
        block_copy(b).wait()
        return carry

    @pl.when(pl.program_id(0) == 0)
    def _():
        zeros[...] = jnp.zeros(zeros.shape, zeros.dtype)
        for needed, copy in pad_copies():
            pl.when(needed)(copy.start)
        lax.fori_loop(nu_ref[0], n_blocks, start_block, 0)
        for needed, copy in pad_copies():
            pl.when(needed)(copy.wait)
        lax.fori_loop(nu_ref[0], n_blocks, wait_block, 0)

    def body(r, carry):
        for kk in range(TOP_K):
            dst = dest_ref[0, 0, kk * tm + r]
            pltpu.make_async_copy(hx_ref.at[pl.ds(r, 1), :], xs_out.at[pl.ds(dst, 1), :], sem).start()
        return carry

    lax.fori_loop(0, tm, body, 0)
    for kk in range(TOP_K):
        pltpu.make_async_copy(hx_ref, xs_out.at[pl.ds(0, tm), :], sem).wait()


def _dispatch(hx, dest_tiles, pad_lo, pad_hi, n_used, n_blocks, tm):
    n, d = hx.shape
    bm = MOE_BLOCK
    grid_spec = pltpu.PrefetchScalarGridSpec(
        num_scalar_prefetch=3,
        grid=(n // tm,),
        in_specs=[pl.BlockSpec((1, 1, TOP_K * tm), lambda i, *_: (i, 0, 0), memory_space=pltpu.SMEM),
                  pl.BlockSpec((tm, d), lambda i, *_: (i, 0))],
        out_specs=pl.BlockSpec(memory_space=pl.ANY),
        scratch_shapes=[pltpu.VMEM((bm, d), hx.dtype), pltpu.SemaphoreType.DMA, pltpu.SemaphoreType.DMA],
    )
    return pl.pallas_call(
        functools.partial(_dispatch_kernel, tm, bm),
        grid_spec=grid_spec,
        out_shape=jax.ShapeDtypeStruct((n_blocks * bm, d), hx.dtype),
        compiler_params=_cparams(("arbitrary",)),
        name="moe_dispatch",
    )(pad_lo, pad_hi, n_used, dest_tiles, hx)


def _expert_kernel(be_ref, nu_ref, x_ref, w1_ref, b1_ref, w2_ref, b2_ref, o_ref, w1_s, w2_s):
    i = pl.program_id(0)
    de = w2_ref.shape[1]

    @pl.when(jnp.logical_or(i == 0, be_ref[i] != be_ref[jnp.maximum(i - 1, 0)]))
    def _():
        w1_s[...] = w1_ref[0].astype(w1_s.dtype)
        w2_s[...] = w2_ref[0].astype(w2_s.dtype)

    @pl.when(i < nu_ref[0])
    def _():
        h = _mm(x_ref[...], w1_s[...]) + b1_ref[0]
        glu = jnp.minimum(h[:, :de], SWIGLU_LIMIT)
        lin = jnp.clip(h[:, de:], -SWIGLU_LIMIT, SWIGLU_LIMIT)
        act = (lin + 1.0) * glu * jax.nn.sigmoid(SWIGLU_ALPHA * glu)
        o_ref[...] = _mm(act, w2_s[...]) + b2_ref[0]

    @pl.when(i >= nu_ref[0])
    def _():
        o_ref[...] = jnp.zeros(o_ref.shape, o_ref.dtype)


def _experts(xs, n_blocks, block_e, n_used, w1, b1, w2, b2):
    d = xs.shape[1]
    ne, _, two_de = w1.shape
    de = w2.shape[1]
    bm = MOE_BLOCK
    grid_spec = pltpu.PrefetchScalarGridSpec(
        num_scalar_prefetch=2,
        grid=(n_blocks,),
        in_specs=[pl.BlockSpec((bm, d), lambda i, *_: (i, 0)),
                  pl.BlockSpec((1, d, two_de), lambda i, be, *_: (be[i], 0, 0)),
                  pl.BlockSpec((1, 1, two_de), lambda i, be, *_: (be[i], 0, 0)),
                  pl.BlockSpec((1, de, d), lambda i, be, *_: (be[i], 0, 0)),
                  pl.BlockSpec((1, 1, d), lambda i, be, *_: (be[i], 0, 0))],
        out_specs=pl.BlockSpec((bm, d), lambda i, *_: (i, 0)),
        scratch_shapes=[pltpu.VMEM((d, two_de), MXU_DTYPE), pltpu.VMEM((de, d), MXU_DTYPE)],
    )
    return pl.pallas_call(
        _expert_kernel,
        grid_spec=grid_spec,
        out_shape=jax.ShapeDtypeStruct((n_blocks * bm, d), F32),
        compiler_params=_cparams(("arbitrary",)),
        name="moe_experts",
    )(block_e, n_used, xs, w1.astype(F32), b1.reshape(ne, 1, two_de).astype(F32), w2.astype(F32),
      b2.reshape(ne, 1, d).astype(F32))


def _combine_kernel(rt, src_ref, size_ref, soff_ref, eo_hbm, x_ref, pos_ref, post_ref, gatet_ref, g2_ref, xo_ref,
                    stage, sem):
    i = pl.program_id(0)
    slot = i % 2
    sr = stage.shape[1]

    def run_copies(tile, to_slot, op):
        for e in range(N_EXPERTS):
            base = tile * N_EXPERTS + e
            src, size, dst = src_ref[base], size_ref[base], soff_ref[base]
            for chunk in ROW_CHUNKS:
                done = size & ~(2 * chunk - 1)
                copy = pltpu.make_async_copy(
                    eo_hbm.at[pl.ds(pl.multiple_of(src + done, SUBLANES), chunk), :],
                    stage.at[to_slot, pl.ds(pl.multiple_of(dst + done, SUBLANES), chunk), :], sem.at[to_slot])
                pl.when((size & chunk) != 0)(getattr(copy, op))

    @pl.when(i == 0)
    def _():
        stage[...] = jnp.zeros(stage.shape, stage.dtype)
        run_copies(0, 0, "start")

    @pl.when(i + 1 < pl.num_programs(0))
    def _():
        run_copies(i + 1, 1 - slot, "start")

    run_copies(i, slot, "wait")

    last = i * N_EXPERTS + N_EXPERTS - 1
    n_staged = soff_ref[last] + size_ref[last]
    pos_t = post_ref[0]
    gate_t = gatet_ref[0]
    row_t = lax.broadcasted_iota(jnp.int32, (sr, rt), 0)
    gate_sel = jnp.zeros((sr, rt), F32)
    for kk in range(TOP_K):
        gate_sel = jnp.where(row_t == pos_t[kk:kk + 1, :], gate_t[kk:kk + 1, :], gate_sel)
    row_gate = jnp.sum(gate_sel, axis=1, keepdims=True)
    row = lax.broadcasted_iota(jnp.int32, (sr, 1), 0)
    staged = jnp.where(row < n_staged, stage[slot], 0.0)
    staged = (staged * row_gate).astype(MXU_DTYPE)
    col = lax.broadcasted_iota(jnp.int32, (rt, sr), 1)
    pos = pos_ref[...]
    picked = col == pos[:, 0:1]
    for kk in range(1, TOP_K):
        picked = jnp.logical_or(picked, col == pos[:, kk:kk + 1])
    sel = jnp.where(picked, 1.0, 0.0).astype(MXU_DTYPE)
    y = jnp.dot(sel, staged, preferred_element_type=F32)
    xo_ref[...] = x_ref[...] + g2_ref[0] * y


def _combine(eo, tables, pos, xres, gates, g2, n_rows, n_lat, seq):
    d = xres.shape[1]
    rt = ROUTE_TILE
    n_lat_tiles = n_lat // rt
    per_batch = seq // rt
    n_mod = g2.shape[0]
    sr = TOP_K * rt + N_EXPERTS * 2 * (SUBLANES - 1)
    sr = -(-sr // MXU_TILE) * MXU_TILE

    def mod_map(i, *_):
        return (jnp.where(i < n_lat_tiles, i // per_batch, n_mod - 1), 0, 0)

    def by_tile(a):
        return a[:n_rows].reshape(n_rows // rt, rt, TOP_K).transpose(0, 2, 1)

    grid_spec = pltpu.PrefetchScalarGridSpec(
        num_scalar_prefetch=3,
        grid=(n_rows // rt,),
        in_specs=[pl.BlockSpec(memory_space=pl.ANY),
                  pl.BlockSpec((rt, d), lambda i, *_: (i, 0)),
                  pl.BlockSpec((rt, TOP_K), lambda i, *_: (i, 0)),
                  pl.BlockSpec((1, TOP_K, rt), lambda i, *_: (i, 0, 0)),
                  pl.BlockSpec((1, TOP_K, rt), lambda i, *_: (i, 0, 0)),
                  pl.BlockSpec((1, 1, d), mod_map)],
        out_specs=pl.BlockSpec((rt, d), lambda i, *_: (i, 0)),
        scratch_shapes=[pltpu.VMEM((2, sr, d), F32), pltpu.SemaphoreType.DMA((2,))],
    )
    return pl.pallas_call(
        functools.partial(_combine_kernel, rt),
        grid_spec=grid_spec,
        out_shape=jax.ShapeDtypeStruct((n_rows, d), F32),
        compiler_params=_cparams(("arbitrary",)),
        name="moe_combine",
    )(*tables, eo, xres, pos, by_tile(pos), by_tile(gates[:, :TOP_K]), g2)


def _dest_tiles(dest, tm):
    n = dest.shape[0]
    return dest.reshape(n // tm, tm, TOP_K).transpose(0, 2, 1).reshape(n // tm, 1, TOP_K * tm)


def _route(idx, counts, n_blocks):
    bm = MOE_BLOCK
    rt = ROUTE_TILE
    n = idx.shape[0]
    tile_before = jnp.cumsum(counts, axis=0) - counts
    total = jnp.sum(counts, axis=0)
    padded = (total + bm - 1) // bm * bm
    pad_end = jnp.cumsum(padded)
    pad_start = pad_end - padded
    first_row = tile_before + pad_start[None, :]
    early = first_row % SUBLANES
    size = jnp.where(counts > 0, (counts + early + SUBLANES - 1) // SUBLANES * SUBLANES, 0)
    stage_off = jnp.cumsum(size, axis=1) - size
    experts = idx[:, :TOP_K].reshape(n // rt, rt, TOP_K)
    rank = idx[:, TOP_K:2 * TOP_K]
    onehot = experts[..., None] == lax.broadcasted_iota(jnp.int32, (1, 1, 1, N_EXPERTS), 3)
    lookup = lambda tab: jnp.sum(jnp.where(onehot, tab[:, None, None, :], 0), axis=-1).reshape(n, TOP_K)
    dest = lookup(first_row) + rank
    pos = lookup(stage_off + early) + rank
    tables = tuple(t.reshape(-1).astype(jnp.int32) for t in (first_row - early, size, stage_off))
    block_start = jnp.arange(n_blocks, dtype=jnp.int32) * bm
    block_e = jnp.minimum(jnp.sum(pad_end[None, :] <= block_start[:, None], axis=1), N_EXPERTS - 1)
    n_used = (pad_end[-1] // bm).astype(jnp.int32).reshape(1)
    padding = ((pad_start + total).astype(jnp.int32), pad_end.astype(jnp.int32))
    return dest.astype(jnp.int32), pos.astype(jnp.int32), tables, block_e.astype(jnp.int32), n_used, padding


def _moe(hx, idx, counts, gates, xres, g2, layer, w1, b1, w2, b2, n_out_rows, n_lat, seq):
    n = hx.shape[0]
    n_blocks = -(-(n * TOP_K + N_EXPERTS * (MOE_BLOCK - 1)) // MOE_BLOCK) + 1
    per_tile = TOK_TILE // ROUTE_TILE
    counts = counts[:, :per_tile, :N_EXPERTS].reshape(n // ROUTE_TILE, N_EXPERTS)
    dest, pos, tables, block_e, n_used, (pad_lo, pad_hi) = _route(idx, counts, n_blocks)
    xs = _dispatch(hx, _dest_tiles(dest, DISPATCH_TILE), pad_lo, pad_hi, n_used, n_blocks, DISPATCH_TILE)
    eo = _experts(xs, n_blocks, block_e + layer * N_EXPERTS, n_used, w1, b1, w2, b2)
    n_rt = n_out_rows // ROUTE_TILE
    tables = tuple(t[:n_rt * N_EXPERTS] for t in tables)
    return _combine(eo, tables, pos[:n_out_rows], xres, gates, g2, n_out_rows, n_lat, seq)


def kernel(x, c, ctx, c_ctx, norm_mix_w, norm_ffn_w, w_ada, b_ada, w_in, mlstm_ig_b, mlstm_fg_b, mlstm_norm_w,
           na_qnorm_w, na_knorm_w, na_rpb, conv_w, conv_b, conv_ln_w, conv_ln_b, w_out, router_w, router_b,
           exp_w1, exp_b1, exp_w2, exp_b2):
    batch, seq, d = x.shape
    ctx_len = ctx.shape[1]
    depth = w_ada.shape[0]
    n_lat = batch * seq
    n_ctx = batch * ctx_len
    n_all = n_lat + n_ctx
    assert depth == 2
    assert seq % TOK_TILE == 0 and n_ctx % TOK_TILE == 0 and seq % ctx_len == 0
    assert n_lat % DISPATCH_TILE == 0 and n_all % DISPATCH_TILE == 0
    assert (seq // GRID_W) % NA_QROWS == 0 and seq // GRID_W >= NA_KROWS

    mod_rows = -(-(batch + 1) // SUBLANES) * SUBLANES
    cc = jnp.zeros((mod_rows, d), F32).at[:batch].set(c).at[batch].set(c_ctx)
    mods = _ada(cc, w_ada, b_ada)[:, :batch + 1].reshape(depth, batch + 1, 1, 6, d)
    x_parts = (x.reshape(n_lat, d), ctx.reshape(n_ctx, d))
    rope = _rope_tables(seq)
    n_exp = exp_w1.shape[1]
    ew1 = exp_w1.reshape((depth * n_exp,) + exp_w1.shape[2:])
    eb1 = exp_b1.reshape((depth * n_exp,) + exp_b1.shape[2:])
    ew2 = exp_w2.reshape((depth * n_exp,) + exp_w2.shape[2:])
    eb2 = exp_b2.reshape((depth * n_exp,) + exp_b2.shape[2:])

    for l in range(depth):
        need_ctx = l < depth - 1
        sh1, sc1, g1, sh2, sc2, g2 = [mods[l, :, :, i, :] for i in range(6)]
        pa, kt, g, gt, pb, u = _inproj(x_parts, sh1, sc1, norm_mix_w[l], w_in[l], n_lat, seq)
        a_lat, a_ctx = _mlstm(pa, kt, g, gt, mlstm_ig_b[l], mlstm_fg_b[l], mlstm_norm_w[l], rope,
                              batch, seq, ctx_len, need_ctx)
        b_lat, b_ctx = _na(pb, na_qnorm_w[l], na_knorm_w[l], na_rpb[l], batch, seq, ctx_len, need_ctx)
        c_lat = _conv(u, conv_w[l], conv_b[l], conv_ln_w[l], conv_ln_b[l], 0, batch, seq)
        if need_ctx:
            c_ctx_out = _conv(u, conv_w[l], conv_b[l], conv_ln_w[l], conv_ln_b[l], n_lat // ctx_len, batch, ctx_len)
            a_all, b_all, c_all = (a_lat, a_ctx), (b_lat, b_ctx), (c_lat, c_ctx_out)
            n_rows = n_all
        else:
            a_all, b_all, c_all = (a_lat,), (b_lat,), (c_lat,)
            n_rows = n_lat
        xmid, hx, idx, gates, counts = _outproj(a_all, b_all, c_all, x_parts, g1, sh2, sc2, norm_ffn_w[l], w_out[l],
                                                router_w[l], router_b[l], n_rows, n_lat, seq)
        xall = _moe(hx, idx, counts, gates, xmid, g2, l, ew1, eb1, ew2, eb2, n_rows, n_lat, seq)
        x_parts = (xall,)
    return x_parts[0][:n_lat].reshape(batch, seq, d)
```

```python
import functools

import numpy as np
import jax
import jax.numpy as jnp
from jax import lax
from jax.experimental import pallas as pl
from jax.experimental.pallas import tpu as pltpu

F32 = jnp.float32
BF16 = jnp.bfloat16
MXU_DTYPE = BF16

GRID_W = 64
HEAD_DIM = 64
MLSTM_HEADS = 4
NA_HEADS = 8
CONV_CH = 256
MLSTM_W = MLSTM_HEADS * HEAD_DIM
NA_W = NA_HEADS * HEAD_DIM
NA_WIN_R = 8
NA_WIN_C = 16
CONV_WIDTH = 31
ROPE_BASE = 10000.0
N_EXPERTS = 32
TOP_K = 4
SWIGLU_LIMIT = 7.0
SWIGLU_ALPHA = 1.702
EPS = 1e-6

A_Q = 0
A_G = 4 * MLSTM_W
B_Q = A_G + 4 * MLSTM_HEADS
C_A = B_Q + 3 * NA_W
IN_COLS = C_A + 2 * CONV_CH

LANES = 128
SUBLANES = 8
MXU_TILE = 256
NEG_BIG = -1e30
LOG2_E = 1.4426950408889634
VMEM_LIMIT = 56 * 1024 * 1024

NA_QROWS = 4
NA_KROWS = NA_QROWS + NA_WIN_R - 1
TOK_TILE = 512
ADA_TILE = 512
MOE_BLOCK = 512
DISPATCH_TILE = 2048
ROUTE_TILE = 256
ROW_CHUNKS = (256, 128, 64, 32, 16, 8)
CONV_ROWS = 64
CONV_PAD = 16


def _mm(a, b):
    return jnp.dot(a.astype(MXU_DTYPE), b.astype(MXU_DTYPE), preferred_element_type=F32)


def _mm_nt(a, b):
    return lax.dot_general(a.astype(MXU_DTYPE), b.astype(MXU_DTYPE), (((1,), (1,)), ((), ())),
                           preferred_element_type=F32)


def _cparams(sem):
    return pltpu.CompilerParams(dimension_semantics=sem, vmem_limit_bytes=VMEM_LIMIT)


def _ada_kernel(c_ref, w_ref, b_ref, o_ref):
    cc = c_ref[...]
    s = cc * jax.nn.sigmoid(cc)
    o_ref[0] = _mm(s, w_ref[0]) + b_ref[0]


def _ada(cc, w_ada, b_ada):
    depth, d, n = w_ada.shape
    rows = cc.shape[0]
    tn = ADA_TILE
    return pl.pallas_call(
        _ada_kernel,
        grid=(depth, n // tn),
        in_specs=[pl.BlockSpec((rows, d), lambda l, j: (0, 0)),
                  pl.BlockSpec((1, d, tn), lambda l, j: (l, 0, j)),
                  pl.BlockSpec((1, 1, tn), lambda l, j: (l, 0, j))],
        out_specs=pl.BlockSpec((1, rows, tn), lambda l, j: (l, 0, j)),
        out_shape=jax.ShapeDtypeStruct((depth, rows, n), F32),
        compiler_params=_cparams(("parallel", "parallel")),
        name="ada_mod",
    )(cc, w_ada, b_ada.reshape(depth, 1, n))


def _token_specs(parts, tm, n_lat_tiles):
    cols = parts[0].shape[1]
    if len(parts) == 1:
        return [pl.BlockSpec((tm, cols), lambda i: (i, 0))]
    return [pl.BlockSpec((tm, cols), lambda i: (jnp.minimum(i, n_lat_tiles - 1), 0)),
            pl.BlockSpec((tm, cols), lambda i: (jnp.maximum(i - n_lat_tiles, 0), 0))]


def _token_tile(refs, n_lat_tiles):
    if len(refs) == 1:
        return refs[0][...]
    return jnp.where(pl.program_id(0) < n_lat_tiles, refs[0][...], refs[1][...])


def _inproj_kernel(n_x, n_lat_tiles, *refs):
    x_refs = refs[:n_x]
    (sh_ref, sc_ref, nw_ref, wa_ref, wkt_ref, wg_ref, wgt_ref, wb_ref, wc_ref,
     pa_ref, kt_ref, g_ref, gt_ref, pb_ref, u_ref) = refs[n_x:]
    x = _token_tile(x_refs, n_lat_tiles)
    ms = jnp.mean(x * x, axis=-1, keepdims=True)
    y = x * lax.rsqrt(ms + EPS) * nw_ref[...]
    h = (y * (1.0 + sc_ref[0]) + sh_ref[0]).astype(MXU_DTYPE)
    pa_ref[...] = _mm(h, wa_ref[...]).astype(pa_ref.dtype)
    kt_ref[...] = _mm_nt(wkt_ref[...], h).astype(kt_ref.dtype)
    g_ref[...] = _mm(h, wg_ref[...])
    gt_ref[...] = _mm_nt(wgt_ref[...], h)
    pb_ref[...] = _mm(h, wb_ref[...]).astype(pb_ref.dtype)
    pc = _mm(h, wc_ref[...])
    u_ref[...] = (pc[:, :CONV_CH] * jax.nn.sigmoid(pc[:, CONV_CH:])).astype(u_ref.dtype)


def _inproj(x_parts, shift, scale, norm_w, w_in, n_lat, seq):
    n = sum(p.shape[0] for p in x_parts)
    d = x_parts[0].shape[1]
    tm = TOK_TILE
    n_lat_tiles = n_lat // tm
    per_batch = seq // tm
    n_mod = shift.shape[0]

    def mod_map(i):
        return (jnp.where(i < n_lat_tiles, i // per_batch, n_mod - 1), 0, 0)

    wq, wk, wv, wo = (w_in[:, A_Q + j * MLSTM_W:A_Q + (j + 1) * MLSTM_W] for j in range(4))
    wv = jnp.pad(wv.reshape(d, MLSTM_HEADS, HEAD_DIM), ((0, 0), (0, 0), (0, LANES - HEAD_DIM)))
    wa = jnp.concatenate([wq, wv.reshape(d, MLSTM_HEADS * LANES), wo], axis=1).astype(MXU_DTYPE)
    wkt = wk.T.astype(MXU_DTYPE)
    wg = w_in[:, A_G:B_Q].astype(MXU_DTYPE)
    wb = w_in[:, B_Q:C_A].astype(MXU_DTYPE)
    wc = w_in[:, C_A:IN_COLS].astype(MXU_DTYPE)
    ng = B_Q - A_G
    full = lambda r, c: pl.BlockSpec((r, c), lambda i: (0, 0))
    return pl.pallas_call(
        functools.partial(_inproj_kernel, len(x_parts), n_lat_tiles),
        grid=(n // tm,),
        in_specs=_token_specs(x_parts, tm, n_lat_tiles) + [
                  pl.BlockSpec((1, 1, d), mod_map),
                  pl.BlockSpec((1, 1, d), mod_map),
                  full(1, d),
                  full(d, A_G), full(MLSTM_W, d), full(d, ng), full(ng, d), full(d, 3 * NA_W),
                  full(d, 2 * CONV_CH)],
        out_specs=[pl.BlockSpec((tm, A_G), lambda i: (i, 0)),
                   pl.BlockSpec((MLSTM_W, tm), lambda i: (0, i)),
                   pl.BlockSpec((tm, ng), lambda i: (i, 0)),
                   pl.BlockSpec((ng, tm), lambda i: (0, i)),
                   pl.BlockSpec((tm, 3 * NA_W), lambda i: (i, 0)),
                   pl.BlockSpec((tm, CONV_CH), lambda i: (i, 0))],
        out_shape=[jax.ShapeDtypeStruct((n, A_G), BF16),
                   jax.ShapeDtypeStruct((MLSTM_W, n), BF16),
                   jax.ShapeDtypeStruct((n, ng), F32),
                   jax.ShapeDtypeStruct((ng, n), F32),
                   jax.ShapeDtypeStruct((n, 3 * NA_W), BF16),
                   jax.ShapeDtypeStruct((n, CONV_CH), BF16)],
        compiler_params=_cparams(("parallel",)),
        name="in_proj",
    )(*x_parts, shift, scale, norm_w.reshape(1, d), wa, wkt, wg, wg.T, wb, wc)


def _split3(x):
    hi = x.astype(BF16)
    r1 = x - hi.astype(F32)
    mid = r1.astype(BF16)
    lo = (r1 - mid.astype(F32)).astype(BF16)
    return hi, mid, lo


def _tri_left(tri, x):
    return sum(jnp.dot(tri, p, preferred_element_type=F32) for p in _split3(x))


def _tri_right(x, tri):
    return sum(jnp.dot(p, tri, preferred_element_type=F32) for p in _split3(x))


def _log_sigmoid(x):
    return jnp.minimum(x, 0.0) - jnp.log(1.0 + jnp.exp(-jnp.abs(x)))


def _mlstm_direction(z, q, kt, vx, gcol, grow, c_st, m_st, h_ref, row0, lc, with_output):
    nh = MLSTM_HEADS
    ti = lax.broadcasted_iota(jnp.int32, (lc, lc), 0)
    si = lax.broadcasted_iota(jnp.int32, (lc, lc), 1)
    lower = si <= ti
    upper = si >= ti
    tl = jnp.where(lower, 1.0, 0.0).astype(BF16)
    tu = jnp.where(upper, 1.0, 0.0).astype(BF16)
    f_col = _log_sigmoid(gcol[:, 2 * nh + z * nh:2 * nh + (z + 1) * nh])
    i_row = grow[z * nh:(z + 1) * nh, :]
    f_row = _log_sigmoid(grow[2 * nh + z * nh:2 * nh + (z + 1) * nh, :])
    if z == 0:
        b_col = _tri_left(tl, f_col)
        b_row = _tri_right(f_row, tu)
        b_tot = b_col[lc - 1:lc, :]
        mask = lower
    else:
        b_col = _tri_left(tu, f_col)
        b_row = _tri_right(f_row, tl)
        b_tot = b_col[0:1, :]
        mask = upper
    a_row = i_row - b_row
    a_max = jnp.max(a_row, axis=1, keepdims=True)
    lane256 = lax.broadcasted_iota(jnp.int32, (1, MLSTM_W), 1)
    lane128 = lax.broadcasted_iota(jnp.int32, (1, LANES), 1)
    kt_mx = kt.astype(MXU_DTYPE)
    c_all = c_st[z].astype(MXU_DTYPE)
    for h in range(nh):
        r = z * nh + h
        ar = a_row[h:h + 1, :]
        bl = b_tot[:, h:h + 1]
        m_old = m_st[r:r + 1, 0:1]
        vh = vx[:, h * LANES:(h + 1) * LANES]
        vext = jnp.where(lane128 == HEAD_DIM, jnp.ones((), vh.dtype), vh).astype(MXU_DTYPE)
        if with_output:
            qh = jnp.where(lane256 // HEAD_DIM == h, q, 0.0).astype(MXU_DTYPE)
            am = jnp.where(mask, ar, NEG_BIG)
            g = jnp.maximum(m_old, jnp.max(am, axis=1, keepdims=True))
            s = _mm(qh, kt_mx) * jnp.exp(am - g)
            w_inter = jnp.exp(m_old - g)
            nd = _mm(s, vext) + w_inter * _mm(qh, c_all)
            den = nd[:, HEAD_DIM:HEAD_DIM + 1]
            hval = nd / jnp.maximum(jnp.abs(den), jnp.exp(-(b_col[:, h:h + 1] + g)))
            h_ref[pl.ds(row0, lc), h * LANES:(h + 1) * LANES] = hval
        m_new = bl + jnp.maximum(m_old, a_max[h:h + 1, :])
        w_row = jnp.exp(bl + ar - m_new)
        decay = jnp.exp(bl + m_old - m_new)
        rows = slice(h * HEAD_DIM, (h + 1) * HEAD_DIM)
        c_st[z, rows, :] = decay * c_st[z, rows, :] + _mm(kt[rows, :] * w_row, vext)
        m_st[r:r + 1, :] = jnp.broadcast_to(m_new, (1, LANES))


def _mlstm_kernel(lc, nc, need_ctx,
                  pa_c, kt_c, g_c, gt_c,
                  pa_f, kt_f, g_f, gt_f, cos_f, sin_f, cost_f, sint_f,
                  pa_b, kt_b, g_b, gt_b, cos_b, sin_b, cost_b, sint_b,
                  o_lat, o_ctx, brow_ref, bcol_ref, nw_ref, perm_ref,
                  *rest):
    if need_ctx:
        out_lat, out_ctx, hf, hb, c_st, m_st = rest
    else:
        out_lat, hf, hb, c_st, m_st = rest
        out_ctx = None
    s = pl.program_id(1)
    w = MLSTM_W
    k_scale = HEAD_DIM ** -0.5

    def load(pa, kt_ref, rope_refs):
        q = pa[:, 0:w].astype(F32)
        kt = kt_ref[...].astype(F32)
        vx = pa[:, w:w + MLSTM_HEADS * LANES]
        if rope_refs is not None:
            cos_ref, sin_ref, cost_ref, sint_ref = rope_refs
            q = q * cos_ref[...] + _mm(q, perm_ref[...]) * sin_ref[...]
            blk = HEAD_DIM // 4
            swapped = jnp.concatenate([kt[(i ^ 1) * blk:((i ^ 1) + 1) * blk, :] for i in range(w // blk)], axis=0)
            kt = kt * cost_ref[...] + swapped * sint_ref[...]
        return q, kt * k_scale, vx

    @pl.when(s == 0)
    def _():
        c_st[...] = jnp.zeros(c_st.shape, F32)
        m_st[...] = jnp.zeros(m_st.shape, F32)
        q, kt, vx = load(pa_c, kt_c, None)
        gcol = g_c[...] + brow_ref[...]
        grow = gt_c[...] + bcol_ref[...]
        for z, h_ref in ((0, hf), (1, hb)):
            _mlstm_direction(z, q, kt, vx, gcol, grow, c_st, m_st, h_ref, 0, lc, need_ctx)

    @pl.when(s > 0)
    def _():
        for z, h_ref, refs in ((0, hf, (pa_f, kt_f, g_f, gt_f, (cos_f, sin_f, cost_f, sint_f))),
                               (1, hb, (pa_b, kt_b, g_b, gt_b, (cos_b, sin_b, cost_b, sint_b)))):
            pa, kt_ref, g, gt, rope_refs = refs
            j = s - 1 if z == 0 else nc - s
            row0 = pl.multiple_of(lc + j * lc, lc)
            q, kt, vx = load(pa, kt_ref, rope_refs)
            gcol = g[...] + brow_ref[...]
            grow = gt[...] + bcol_ref[...]
            _mlstm_direction(z, q, kt, vx, gcol, grow, c_st, m_st, h_ref, row0, lc, True)

    @pl.when(s == nc)
    def _():
        lane128 = lax.broadcasted_iota(jnp.int32, (1, LANES), 1)
        mean_w = jnp.where(lax.broadcasted_iota(jnp.int32, (LANES, LANES), 0) < HEAD_DIM,
                           1.0 / HEAD_DIM, 0.0).astype(BF16)
        first = 0 if need_ctx else 1
        for ch in range(first, nc + 1):
            rows = slice(ch * lc, (ch + 1) * lc)
            if ch == 0:
                o_val, dst, dst_rows = o_ctx[...], out_ctx, slice(0, lc)
            else:
                dst_rows = slice((ch - 1) * lc, ch * lc)
                o_val, dst = o_lat[dst_rows, :], out_lat
            o_val = o_val.astype(F32)
            for p in range(MLSTM_HEADS // 2):
                pair = []
                for h in (2 * p, 2 * p + 1):
                    hv = hf[rows, h * LANES:(h + 1) * LANES] + hb[rows, h * LANES:(h + 1) * LANES]
                    hv = jnp.where(lane128 < HEAD_DIM, hv, 0.0)
                    sq = hv * hv
                    sq_hi = sq.astype(BF16)
                    sq_lo = (sq - sq_hi.astype(F32)).astype(BF16)
                    ms = (jnp.dot(sq_hi, mean_w, preferred_element_type=F32)
                          + jnp.dot(sq_lo, mean_w, preferred_element_type=F32))
                    pair.append(hv * lax.rsqrt(ms + EPS))
                packed = jnp.where(lane128 < HEAD_DIM, pair[0], pltpu.roll(pair[1], HEAD_DIM, 1))
                cols = slice(p * LANES, (p + 1) * LANES)
                res = packed * nw_ref[:, cols] * jax.nn.sigmoid(o_val[:, cols])
                dst[dst_rows, cols] = res.astype(dst.dtype)


def _rope_tables(seq):
    half = HEAD_DIM // 2
    quarter = half // 2
    t = jnp.arange(seq, dtype=jnp.int32)
    inv_freq = ROPE_BASE ** (-jnp.arange(quarter, dtype=F32) / quarter)
    parts_c, parts_s = [], []
    for pos in (t // GRID_W, t % GRID_W):
        ang = pos.astype(F32)[:, None] * inv_freq[None, :]
        parts_c += [jnp.cos(ang), jnp.cos(ang)]
        parts_s += [-jnp.sin(ang), jnp.sin(ang)]
    cos = jnp.tile(jnp.concatenate(parts_c, axis=-1), (1, MLSTM_HEADS))
    sin = jnp.tile(jnp.concatenate(parts_s, axis=-1), (1, MLSTM_HEADS))
    j = np.arange(MLSTM_W)
    partner = np.where(j % half < quarter, j + quarter, j - quarter)
    perm = np.zeros((MLSTM_W, MLSTM_W), np.float32)
    perm[partner, j] = 1.0
    return cos, sin, cos.T, sin.T, jnp.asarray(perm, dtype=MXU_DTYPE)


def _mlstm(pa, kt, g, gt, ig_b, fg_b, norm_w, rope, batch, seq, ctx_len, need_ctx):
    lc = ctx_len
    nc = seq // lc
    n_lat = batch * seq
    cos, sin, cos_t, sin_t, perm = rope
    bias = jnp.concatenate([ig_b.reshape(-1), fg_b.reshape(-1)]).astype(F32)
    ng = bias.shape[0]
    lat_blocks = n_lat // lc

    def fwd(b, s):
        return b * nc + jnp.maximum(s - 1, 0)

    def bwd(b, s):
        return b * nc + nc - jnp.maximum(s, 1)

    def fwd_c(b, s):
        return jnp.maximum(s - 1, 0)

    def bwd_c(b, s):
        return nc - jnp.maximum(s, 1)

    def lat_specs(chunk, chunk_c):
        return [pl.BlockSpec((lc, A_G), lambda b, s: (chunk(b, s), 0)),
                pl.BlockSpec((MLSTM_W, lc), lambda b, s: (0, chunk(b, s))),
                pl.BlockSpec((lc, ng), lambda b, s: (chunk(b, s), 0)),
                pl.BlockSpec((ng, lc), lambda b, s: (0, chunk(b, s))),
                pl.BlockSpec((lc, MLSTM_W), lambda b, s: (chunk_c(b, s), 0)),
                pl.BlockSpec((lc, MLSTM_W), lambda b, s: (chunk_c(b, s), 0)),
                pl.BlockSpec((MLSTM_W, lc), lambda b, s: (0, chunk_c(b, s))),
                pl.BlockSpec((MLSTM_W, lc), lambda b, s: (0, chunk_c(b, s)))]

    in_specs = ([pl.BlockSpec((lc, A_G), lambda b, s: (lat_blocks + b, 0)),
                 pl.BlockSpec((MLSTM_W, lc), lambda b, s: (0, lat_blocks + b)),
                 pl.BlockSpec((lc, ng), lambda b, s: (lat_blocks + b, 0)),
                 pl.BlockSpec((ng, lc), lambda b, s: (0, lat_blocks + b))]
                + lat_specs(fwd, fwd_c) + lat_specs(bwd, bwd_c)
                + [pl.BlockSpec((seq, MLSTM_W), lambda b, s: (b, 3)),
                   pl.BlockSpec((lc, MLSTM_W), lambda b, s: (lat_blocks + b, 3)),
                   pl.BlockSpec((1, ng), lambda b, s: (0, 0)),
                   pl.BlockSpec((ng, 1), lambda b, s: (0, 0)),
                   pl.BlockSpec((1, MLSTM_W), lambda b, s: (0, 0)),
                   pl.BlockSpec((MLSTM_W, MLSTM_W), lambda b, s: (0, 0))])
    out_specs = [pl.BlockSpec((seq, MLSTM_W), lambda b, s: (b, 0))]
    out_shape = [jax.ShapeDtypeStruct((n_lat, MLSTM_W), BF16)]
    if need_ctx:
        out_specs.append(pl.BlockSpec((lc, MLSTM_W), lambda b, s: (b, 0)))
        out_shape.append(jax.ShapeDtypeStruct((batch * ctx_len, MLSTM_W), BF16))
    t_all = ctx_len + seq
    outs = pl.pallas_call(
        functools.partial(_mlstm_kernel, lc, nc, need_ctx),
        grid=(batch, nc + 1),
        in_specs=in_specs,
        out_specs=out_specs,
        out_shape=out_shape,
        scratch_shapes=[pltpu.VMEM((t_all, MLSTM_HEADS * LANES), F32),
                        pltpu.VMEM((t_all, MLSTM_HEADS * LANES), F32),
                        pltpu.VMEM((2, MLSTM_W, LANES), F32),
                        pltpu.VMEM((2 * MLSTM_HEADS, LANES), F32)],
        compiler_params=_cparams(("parallel", "arbitrary")),
        name="mlstm",
    )(pa, kt, g, gt, pa, kt, g, gt, cos, sin, cos_t, sin_t, pa, kt, g, gt, cos, sin, cos_t, sin_t, pa, pa,
      bias.reshape(1, ng), bias.reshape(ng, 1), norm_w.reshape(1, MLSTM_W).astype(F32), perm)
    return (outs[0], outs[1]) if need_ctx else (outs[0], None)


def _na_patterns(n_rows):
    kr = min(NA_WIN_R, n_rows)
    n_dr = 2 * NA_WIN_R - 1
    pats, pat_ids, bases = [], [], []
    for gi in range(n_rows // NA_QROWS):
        base = int(np.clip(NA_QROWS * gi - NA_WIN_R // 2, 0, n_rows - NA_KROWS))
        dr = np.full((NA_QROWS, NA_KROWS), n_dr, np.int32)
        for qr in range(NA_QROWS):
            r = NA_QROWS * gi + qr
            r0 = int(np.clip(r - kr // 2, 0, n_rows - kr))
            for kj in range(NA_KROWS):
                if r0 <= base + kj < r0 + kr:
                    dr[qr, kj] = base + kj - r + NA_WIN_R - 1
        for pi, p in enumerate(pats):
            if np.array_equal(p, dr):
                pat_ids.append(pi)
                break
        else:
            pat_ids.append(len(pats))
            pats.append(dr)
        bases.append(base)
    return tuple(pat_ids), tuple(bases), np.stack(pats)


def _na_bias_table(rpb, row_idx):
    heads = rpb.shape[0]
    col = np.arange(GRID_W)
    col_start = np.clip(col - NA_WIN_C // 2, 0, GRID_W - NA_WIN_C)
    in_win = (col[None, :] >= col_start[:, None]) & (col[None, :] < col_start[:, None] + NA_WIN_C)
    dc = np.clip(col[None, :] - col[:, None] + NA_WIN_C - 1, 0, 2 * NA_WIN_C - 2)
    onehot = (dc[None] == np.arange(2 * NA_WIN_C - 1)[:, None, None]).astype(np.float32)
    planes = jnp.einsum('hdc,cqk->hdqk', rpb, onehot, precision=lax.Precision.HIGHEST)
    planes = jnp.where(in_win[None, None], planes, NEG_BIG)
    planes = jnp.concatenate([planes, jnp.full((heads, 1, GRID_W, GRID_W), NEG_BIG, F32)], axis=1)
    npat = row_idx.shape[0]
    tab = planes[:, row_idx.reshape(-1)].reshape(heads, npat, NA_QROWS, NA_KROWS, GRID_W, GRID_W)
    return tab.transpose(0, 1, 2, 4, 3, 5).reshape(heads, npat, NA_QROWS * GRID_W, NA_KROWS * GRID_W)


def _na_kernel(pat_ids, bases, need_ctx, q_ref, k_ref, v_ref, kc_ref, vc_ref, *rest):
    if need_ctx:
        qc_ref, bias_ref, qw_ref, kw_ref, out_ref, outc_ref, kn_s, kcn_s = rest
    else:
        bias_ref, qw_ref, kw_ref, out_ref, kn_s, kcn_s = rest
    lane = lax.broadcasted_iota(jnp.int32, (1, LANES), 1)
    low = lane < HEAD_DIM
    inv_d = 1.0 / HEAD_DIM

    def rmsn(x, w):
        x2 = x * x
        s0 = jnp.sum(jnp.where(low, x2, 0.0), axis=-1, keepdims=True)
        s1 = jnp.sum(jnp.where(low, 0.0, x2), axis=-1, keepdims=True)
        r = jnp.where(low, lax.rsqrt(s0 * inv_d + EPS), lax.rsqrt(s1 * inv_d + EPS))
        return x * r * w

    qw = qw_ref[...]
    kn_s[...] = rmsn(k_ref[...].astype(F32), kw_ref[...]).astype(kn_s.dtype)
    kcn_s[...] = rmsn(kc_ref[...].astype(F32), kw_ref[...]).astype(kcn_s.dtype)
    kcn = kcn_s[...]
    vc = vc_ref[...]
    scale = HEAD_DIM ** -0.5 * LOG2_E
    nq = NA_QROWS * GRID_W
    nk = NA_KROWS * GRID_W

    def attend(qn, parts):
        outs = []
        for hh in range(2):
            qh = jnp.where(low if hh == 0 else jnp.logical_not(low), qn, 0.0).astype(MXU_DTYPE)
            scores = []
            for keys, _, bias in parts:
                sc = _mm_nt(qh, keys)
                if bias is not None:
                    sc = sc + bias[hh]
                scores.append(sc)
            m = scores[0].max(axis=-1, keepdims=True)
            for sc in scores[1:]:
                m = jnp.maximum(m, sc.max(axis=-1, keepdims=True))
            acc = None
            den = None
            for sc, (_, vals, _) in zip(scores, parts):
                p = jnp.exp2(sc - m)
                d = jnp.sum(p, axis=-1, keepdims=True)
                o = _mm(p, vals)
                acc = o if acc is None else acc + o
                den = d if den is None else den + d
            outs.append(acc / den)
        return jnp.where(low, outs[0], outs[1])

    for gi, (pid, base) in enumerate(zip(pat_ids, bases)):
        qn = rmsn(q_ref[gi * nq:(gi + 1) * nq, :].astype(F32), qw) * scale
        kwin = kn_s[base * GRID_W:base * GRID_W + nk, :]
        vwin = v_ref[base * GRID_W:base * GRID_W + nk, :]
        bias = (bias_ref[0, pid], bias_ref[1, pid])
        res = attend(qn, [(kwin, vwin, bias), (kcn, vc, None)])
        out_ref[gi * nq:(gi + 1) * nq, :] = res.astype(out_ref.dtype)

    if need_ctx:
        qn = rmsn(qc_ref[...].astype(F32), qw) * scale
        outc_ref[...] = attend(qn, [(kcn, vc, None)]).astype(outc_ref.dtype)


def _na(pb, qn_w, kn_w, rpb, batch, seq, ctx_len, need_ctx):
    n_rows = seq // GRID_W
    pat_ids, bases, row_idx = _na_patterns(n_rows)
    npat = row_idx.shape[0]
    nq, nk = NA_QROWS * GRID_W, NA_KROWS * GRID_W
    bias = _na_bias_table(rpb.astype(F32) * LOG2_E, row_idx)
    n_lat = batch * seq
    pairs = NA_HEADS // 2
    qoff, koff, voff = 0, pairs, 2 * pairs

    in_specs = [pl.BlockSpec((seq, LANES), lambda p, b: (b, qoff + p)),
                pl.BlockSpec((seq, LANES), lambda p, b: (b, koff + p)),
                pl.BlockSpec((seq, LANES), lambda p, b: (b, voff + p)),
                pl.BlockSpec((ctx_len, LANES), lambda p, b: (n_lat // ctx_len + b, koff + p)),
                pl.BlockSpec((ctx_len, LANES), lambda p, b: (n_lat // ctx_len + b, voff + p))]
    args = [pb, pb, pb, pb, pb]
    if need_ctx:
        in_specs.append(pl.BlockSpec((ctx_len, LANES), lambda p, b: (n_lat // ctx_len + b, qoff + p)))
        args.append(pb)
    in_specs += [pl.BlockSpec((2, npat, nq, nk), lambda p, b: (p, 0, 0, 0)),
                 pl.BlockSpec((1, LANES), lambda p, b: (0, 0)),
                 pl.BlockSpec((1, LANES), lambda p, b: (0, 0))]
    args += [bias, jnp.tile(qn_w.astype(F32), 2).reshape(1, LANES), jnp.tile(kn_w.astype(F32), 2).reshape(1, LANES)]
    out_specs = [pl.BlockSpec((seq, LANES), lambda p, b: (b, p))]
    out_shape = [jax.ShapeDtypeStruct((n_lat, NA_W), BF16)]
    if need_ctx:
        out_specs.append(pl.BlockSpec((ctx_len, LANES), lambda p, b: (b, p)))
        out_shape.append(jax.ShapeDtypeStruct((batch * ctx_len, NA_W), BF16))
    outs = pl.pallas_call(
        functools.partial(_na_kernel, pat_ids, bases, need_ctx),
        grid=(pairs, batch),
        in_specs=in_specs,
        out_specs=out_specs,
        out_shape=out_shape,
        scratch_shapes=[pltpu.VMEM((seq, LANES), MXU_DTYPE), pltpu.VMEM((ctx_len, LANES), MXU_DTYPE)],
        compiler_params=_cparams(("parallel", "parallel")),
        name="na_attn",
    )(*args)
    return (outs[0], outs[1]) if need_ctx else (outs[0], None)


def _conv_kernel(t_len, u_ref, w_ref, cb_ref, lw_ref, lb_ref, o_ref, pad_s):
    zeros = jnp.zeros((CONV_PAD, CONV_CH), F32)
    pad_s[0:CONV_PAD, :] = zeros
    pad_s[CONV_PAD + t_len:2 * CONV_PAD + t_len, :] = zeros
    pad_s[CONV_PAD:CONV_PAD + t_len, :] = u_ref[...].astype(F32)
    shift = CONV_PAD - CONV_WIDTH // 2

    def body(c, carry):
        r0 = pl.multiple_of(c * CONV_ROWS, CONV_ROWS)
        n_win = CONV_ROWS + 2 * CONV_PAD
        win = pad_s[pl.ds(r0, n_win), :]
        rot = [win] + [pltpu.roll(win, n_win - ph, 0) for ph in range(1, SUBLANES)]
        acc = jnp.zeros((CONV_ROWS, CONV_CH), F32) + cb_ref[...]
        for j in range(CONV_WIDTH):
            ph, al = (j + shift) % SUBLANES, (j + shift) // SUBLANES * SUBLANES
            acc = acc + rot[ph][al:al + CONV_ROWS, :] * w_ref[j:j + 1, :]
        mean = jnp.mean(acc, axis=-1, keepdims=True)
        xc = acc - mean
        var = jnp.mean(xc * xc, axis=-1, keepdims=True)
        y = xc * lax.rsqrt(var + EPS) * lw_ref[...] + lb_ref[...]
        o_ref[pl.ds(r0, CONV_ROWS), :] = (y * jax.nn.sigmoid(y)).astype(o_ref.dtype)
        return carry

    lax.fori_loop(0, t_len // CONV_ROWS, body, 0)


def _conv(u, conv_w, conv_b, ln_w, ln_b, first_block, n_seq, t_len):
    row = lambda a: a.reshape(1, CONV_CH).astype(F32)
    return pl.pallas_call(
        functools.partial(_conv_kernel, t_len),
        grid=(n_seq,),
        in_specs=[pl.BlockSpec((t_len, CONV_CH), lambda b: (first_block + b, 0)),
                  pl.BlockSpec((CONV_WIDTH, CONV_CH), lambda b: (0, 0)),
                  pl.BlockSpec((1, CONV_CH), lambda b: (0, 0)),
                  pl.BlockSpec((1, CONV_CH), lambda b: (0, 0)),
                  pl.BlockSpec((1, CONV_CH), lambda b: (0, 0))],
        out_specs=pl.BlockSpec((t_len, CONV_CH), lambda b: (b, 0)),
        out_shape=jax.ShapeDtypeStruct((n_seq * t_len, CONV_CH), BF16),
        scratch_shapes=[pltpu.VMEM((t_len + 2 * CONV_PAD, CONV_CH), F32)],
        compiler_params=_cparams(("parallel",)),
        name="conv_module",
    )(u, conv_w.astype(F32), row(conv_b), row(ln_w), row(ln_b))


def _outproj_kernel(rt, n_src, n_lat_tiles, *refs):
    a_refs, b_refs, c_refs, x_refs = (refs[j * n_src:(j + 1) * n_src] for j in range(4))
    (g1_ref, sh_ref, sc_ref, nw_ref, wa_ref, wb_ref, wc_ref, rwh_ref, rwl_ref, rb_ref,
     xo_ref, hx_ref, idx_ref, gate_ref, cnt_ref) = refs[4 * n_src:]
    tile = lambda parts: _token_tile(parts, n_lat_tiles)
    mix = _mm(tile(a_refs), wa_ref[...]) + _mm(tile(b_refs), wb_ref[...]) + _mm(tile(c_refs), wc_ref[...])
    xn = tile(x_refs) + g1_ref[0] * mix
    xo_ref[...] = xn
    ms = jnp.mean(xn * xn, axis=-1, keepdims=True)
    hx = xn * lax.rsqrt(ms + EPS) * nw_ref[...] * (1.0 + sc_ref[0]) + sh_ref[0]
    hx_ref[...] = hx
    h_hi = hx.astype(BF16)
    h_lo = (hx - h_hi.astype(F32)).astype(BF16)
    logits = (jnp.dot(h_hi, rwh_ref[...], preferred_element_type=F32)
              + jnp.dot(h_lo, rwh_ref[...], preferred_element_type=F32)
              + jnp.dot(h_hi, rwl_ref[...], preferred_element_type=F32)) + rb_ref[...]
    lane = lax.broadcasted_iota(jnp.int32, logits.shape, 1)
    idx_out = jnp.zeros(logits.shape, jnp.int32)
    val_out = jnp.zeros(logits.shape, F32)
    top = None
    den = None
    sels = []
    for kk in range(TOP_K):
        m = jnp.max(logits, axis=-1, keepdims=True)
        sel = jnp.min(jnp.where(logits == m, lane, LANES), axis=-1, keepdims=True)
        if kk == 0:
            top = m
        e = jnp.exp(m - top)
        den = e if den is None else den + e
        idx_out = jnp.where(lane == kk, sel, idx_out)
        val_out = jnp.where(lane == kk, e, val_out)
        logits = jnp.where(lane == sel, -jnp.inf, logits)
        sels.append(sel)
    gate_ref[...] = val_out / den
    tm = logits.shape[0]
    chosen = jnp.where(logits == -jnp.inf, 1.0, 0.0)
    ti = lax.broadcasted_iota(jnp.int32, (tm, tm), 0)
    si = lax.broadcasted_iota(jnp.int32, (tm, tm), 1)
    earlier = jnp.where(jnp.logical_and(si < ti, si // rt == ti // rt), 1.0, 0.0).astype(BF16)
    before = jnp.dot(earlier, chosen.astype(BF16), preferred_element_type=F32)
    for kk in range(TOP_K):
        rank = jnp.sum(jnp.where(lane == sels[kk], before, 0.0), axis=-1, keepdims=True)
        idx_out = jnp.where(lane == TOP_K + kk, rank.astype(jnp.int32), idx_out)
    idx_ref[...] = idx_out
    row = lax.broadcasted_iota(jnp.int32, cnt_ref.shape[1:], 0)
    cnt = jnp.zeros(cnt_ref.shape[1:], F32)
    for sub in range(tm // rt):
        cnt = jnp.where(row == sub, jnp.sum(chosen[sub * rt:(sub + 1) * rt], axis=0, keepdims=True), cnt)
    cnt_ref[0] = cnt.astype(jnp.int32)


def _outproj(a, b, c, x_parts, g1, sh2, sc2, norm_w, w_out, router_w, router_b, n_rows, n_lat, seq):
    d = x_parts[0].shape[1]
    n_src = len(x_parts)
    assert len(a) == len(b) == len(c) == n_src
    tm = TOK_TILE
    n_lat_tiles = n_lat // tm
    per_batch = seq // tm
    n_mod = g1.shape[0]

    def mod_map(i):
        return (jnp.where(i < n_lat_tiles, i // per_batch, n_mod - 1), 0, 0)

    wa = w_out[0:MLSTM_W].astype(MXU_DTYPE)
    wb = w_out[MLSTM_W:MLSTM_W + NA_W].astype(MXU_DTYPE)
    wc = w_out[MLSTM_W + NA_W:].astype(MXU_DTYPE)
    rw = jnp.zeros((d, LANES), F32).at[:, :N_EXPERTS].set(router_w.astype(F32))
    rw_hi = rw.astype(BF16)
    rw_lo = (rw - rw_hi.astype(F32)).astype(BF16)
    rb = jnp.full((1, LANES), NEG_BIG, F32).at[0, :N_EXPERTS].set(router_b.astype(F32))
    full = lambda r, cc: pl.BlockSpec((r, cc), lambda i: (0, 0))
    tile = lambda cc: pl.BlockSpec((tm, cc), lambda i: (i, 0))
    return pl.pallas_call(
        functools.partial(_outproj_kernel, ROUTE_TILE, n_src, n_lat_tiles),
        grid=(n_rows // tm,),
        in_specs=[spec for parts in (a, b, c, x_parts) for spec in _token_specs(parts, tm, n_lat_tiles)] + [
                  pl.BlockSpec((1, 1, d), mod_map), pl.BlockSpec((1, 1, d), mod_map),
                  pl.BlockSpec((1, 1, d), mod_map), full(1, d),
                  full(MLSTM_W, d), full(NA_W, d), full(CONV_CH, d), full(d, LANES), full(d, LANES),
                  full(1, LANES)],
        out_specs=[tile(d), tile(d), tile(LANES), tile(LANES),
                   pl.BlockSpec((1, SUBLANES, LANES), lambda i: (i, 0, 0))],
        out_shape=[jax.ShapeDtypeStruct((n_rows, d), F32), jax.ShapeDtypeStruct((n_rows, d), F32),
                   jax.ShapeDtypeStruct((n_rows, LANES), jnp.int32), jax.ShapeDtypeStruct((n_rows, LANES), F32),
                   jax.ShapeDtypeStruct((n_rows // tm, SUBLANES, LANES), jnp.int32)],
        compiler_params=_cparams(("parallel",)),
        name="out_proj",
    )(*a, *b, *c, *x_parts, g1, sh2, sc2, norm_w.reshape(1, d).astype(F32), wa, wb, wc, rw_hi, rw_lo, rb)


def _dispatch_kernel(tm, bm, lo_ref, hi_ref, nu_ref, dest_ref, hx_ref, xs_out, zeros, sem, zsem):
    n_blocks = xs_out.shape[0] // bm

    def pad_copies():
        for e in range(N_EXPERTS):
            lo, hi = lo_ref[e], hi_ref[e]
            head = jnp.minimum((SUBLANES - lo % SUBLANES) % SUBLANES, hi - lo)
            for j in range(SUBLANES - 1):
                yield j < head, pltpu.make_async_copy(zeros.at[pl.ds(0, 1), :], xs_out.at[pl.ds(lo + j, 1), :], zsem)
            rest = hi - lo - head
            for chunk in ROW_CHUNKS:
                at = pl.multiple_of(lo + head + (rest & ~(2 * chunk - 1)), SUBLANES)
                yield (rest & chunk) != 0, pltpu.make_async_copy(
                    zeros.at[pl.ds(0, chunk), :], xs_out.at[pl.ds(at, chunk), :], zsem)

    def block_copy(b):
        return pltpu.make_async_copy(zeros, xs_out.at[pl.ds(pl.multiple_of(b * bm, bm), bm), :], zsem)

    def start_block(b, carry):
        block_copy(b).start()
        return carry

    def wait_block(b, carry):
        block_copy(b).wait()
        return carry

    @pl.when(pl.program_id(0) == 0)
    def _():
        zeros[...] = jnp.zeros(zeros.shape, zeros.dtype)
        for needed, copy in pad_copies():
            pl.when(needed)(copy.start)
        lax.fori_loop(nu_ref[0], n_blocks, start_block, 0)
        for needed, copy in pad_copies():
            pl.when(needed)(copy.wait)
        lax.fori_loop(nu_ref[0], n_blocks, wait_block, 0)

    def body(r, carry):
        for kk in range(TOP_K):
            dst = dest_ref[0, 0, kk * tm + r]
            pltpu.make_async_copy(hx_ref.at[pl.ds(r, 1), :], xs_out.at[pl.ds(dst, 1), :], sem).start()
        return carry

    lax.fori_loop(0, tm, body, 0)
    for kk in range(TOP_K):
        pltpu.make_async_copy(hx_ref, xs_out.at[pl.ds(0, tm), :], sem).wait()


def _dispatch(hx, dest_tiles, pad_lo, pad_hi, n_used, n_blocks, tm):
    n, d = hx.shape
    bm = MOE_BLOCK
    grid_spec = pltpu.PrefetchScalarGridSpec(
        num_scalar_prefetch=3,
        grid=(n // tm,),
        in_specs=[pl.BlockSpec((1, 1, TOP_K * tm), lambda i, *_: (i, 0, 0), memory_space=pltpu.SMEM),
                  pl.BlockSpec((tm, d), lambda i, *_: (i, 0))],
        out_specs=pl.BlockSpec(memory_space=pl.ANY),
        scratch_shapes=[pltpu.VMEM((bm, d), hx.dtype), pltpu.SemaphoreType.DMA, pltpu.SemaphoreType.DMA],
    )
    return pl.pallas_call(
        functools.partial(_dispatch_kernel, tm, bm),
        grid_spec=grid_spec,
        out_shape=jax.ShapeDtypeStruct((n_blocks * bm, d), hx.dtype),
        compiler_params=_cparams(("arbitrary",)),
        name="moe_dispatch",
    )(pad_lo, pad_hi, n_used, dest_tiles, hx)


def _expert_kernel(be_ref, nu_ref, x_ref, w1_ref, b1_ref, w2_ref, b2_ref, o_ref, w1_s, w2_s):
    i = pl.program_id(0)
    de = w2_ref.shape[1]

    @pl.when(jnp.logical_or(i == 0, be_ref[i] != be_ref[jnp.maximum(i - 1, 0)]))
    def _():
        w1_s[...] = w1_ref[0].astype(w1_s.dtype)
        w2_s[...] = w2_ref[0].astype(w2_s.dtype)

    @pl.when(i < nu_ref[0])
    def _():
        h = _mm(x_ref[...], w1_s[...]) + b1_ref[0]
        glu = jnp.minimum(h[:, :de], SWIGLU_LIMIT)
        lin = jnp.clip(h[:, de:], -SWIGLU_LIMIT, SWIGLU_LIMIT)
        act = (lin + 1.0) * glu * jax.nn.sigmoid(SWIGLU_ALPHA * glu)
        o_ref[...] = _mm(act, w2_s[...]) + b2_ref[0]

    @pl.when(i >= nu_ref[0])
    def _():
        o_ref[...] = jnp.zeros(o_ref.shape, o_ref.dtype)


def _experts(xs, n_blocks, block_e, n_used, w1, b1, w2, b2):
    d = xs.shape[1]
    ne, _, two_de = w1.shape
    de = w2.shape[1]
    bm = MOE_BLOCK
    grid_spec = pltpu.PrefetchScalarGridSpec(
        num_scalar_prefetch=2,
        grid=(n_blocks,),
        in_specs=[pl.BlockSpec((bm, d), lambda i, *_: (i, 0)),
                  pl.BlockSpec((1, d, two_de), lambda i, be, *_: (be[i], 0, 0)),
                  pl.BlockSpec((1, 1, two_de), lambda i, be, *_: (be[i], 0, 0)),
                  pl.BlockSpec((1, de, d), lambda i, be, *_: (be[i], 0, 0)),
                  pl.BlockSpec((1, 1, d), lambda i, be, *_: (be[i], 0, 0))],
        out_specs=pl.BlockSpec((bm, d), lambda i, *_: (i, 0)),
        scratch_shapes=[pltpu.VMEM((d, two_de), MXU_DTYPE), pltpu.VMEM((de, d), MXU_DTYPE)],
    )
    return pl.pallas_call(
        _expert_kernel,
        grid_spec=grid_spec,
        out_shape=jax.ShapeDtypeStruct((n_blocks * bm, d), F32),
        compiler_params=_cparams(("arbitrary",)),
        name="moe_experts",
    )(block_e, n_used, xs, w1.astype(F32), b1.reshape(ne, 1, two_de).astype(F32), w2.astype(F32),
      b2.reshape(ne, 1, d).astype(F32))


def _combine_kernel(rt, src_ref, size_ref, soff_ref, eo_hbm, x_ref, pos_ref, post_ref, gatet_ref, g2_ref, xo_ref,
                    stage, sem):
    i = pl.program_id(0)
    slot = i % 2
    sr = stage.shape[1]

    def run_copies(tile, to_slot, op):
        for e in range(N_EXPERTS):
            base = tile * N_EXPERTS + e
            src, size, dst = src_ref[base], size_ref[base], soff_ref[base]
            for chunk in ROW_CHUNKS:
                done = size & ~(2 * chunk - 1)
                copy = pltpu.make_async_copy(
                    eo_hbm.at[pl.ds(pl.multiple_of(src + done, SUBLANES), chunk), :],
                    stage.at[to_slot, pl.ds(pl.multiple_of(dst + done, SUBLANES), chunk), :], sem.at[to_slot])
                pl.when((size & chunk) != 0)(getattr(copy, op))

    @pl.when(i == 0)
    def _():
        stage[...] = jnp.zeros(stage.shape, stage.dtype)
        run_copies(0, 0, "start")

    @pl.when(i + 1 < pl.num_programs(0))
    def _():
        run_copies(i + 1, 1 - slot, "start")

    run_copies(i, slot, "wait")

    last = i * N_EXPERTS + N_EXPERTS - 1
    n_staged = soff_ref[last] + size_ref[last]
    pos_t = post_ref[0]
    gate_t = gatet_ref[0]
    row_t = lax.broadcasted_iota(jnp.int32, (sr, rt), 0)
    gate_sel = jnp.zeros((sr, rt), F32)
    for kk in range(TOP_K):
        gate_sel = jnp.where(row_t == pos_t[kk:kk + 1, :], gate_t[kk:kk + 1, :], gate_sel)
    row_gate = jnp.sum(gate_sel, axis=1, keepdims=True)
    row = lax.broadcasted_iota(jnp.int32, (sr, 1), 0)
    staged = jnp.where(row < n_staged, stage[slot], 0.0)
    staged = (staged * row_gate).astype(MXU_DTYPE)
    col = lax.broadcasted_iota(jnp.int32, (rt, sr), 1)
    pos = pos_ref[...]
    picked = col == pos[:, 0:1]
    for kk in range(1, TOP_K):
        picked = jnp.logical_or(picked, col == pos[:, kk:kk + 1])
    sel = jnp.where(picked, 1.0, 0.0).astype(MXU_DTYPE)
    y = jnp.dot(sel, staged, preferred_element_type=F32)
    xo_ref[...] = x_ref[...] + g2_ref[0] * y


def _combine(eo, tables, pos, xres, gates, g2, n_rows, n_lat, seq):
    d = xres.shape[1]
    rt = ROUTE_TILE
    n_lat_tiles = n_lat // rt
    per_batch = seq // rt
    n_mod = g2.shape[0]
    sr = TOP_K * rt + N_EXPERTS * 2 * (SUBLANES - 1)
    sr = -(-sr // MXU_TILE) * MXU_TILE

    def mod_map(i, *_):
        return (jnp.where(i < n_lat_tiles, i // per_batch, n_mod - 1), 0, 0)

    def by_tile(a):
        return a[:n_rows].reshape(n_rows // rt, rt, TOP_K).transpose(0, 2, 1)

    grid_spec = pltpu.PrefetchScalarGridSpec(
        num_scalar_prefetch=3,
        grid=(n_rows // rt,),
        in_specs=[pl.BlockSpec(memory_space=pl.ANY),
                  pl.BlockSpec((rt, d), lambda i, *_: (i, 0)),
                  pl.BlockSpec((rt, TOP_K), lambda i, *_: (i, 0)),
                  pl.BlockSpec((1, TOP_K, rt), lambda i, *_: (i, 0, 0)),
                  pl.BlockSpec((1, TOP_K, rt), lambda i, *_: (i, 0, 0)),
                  pl.BlockSpec((1, 1, d), mod_map)],
        out_specs=pl.BlockSpec((rt, d), lambda i, *_: (i, 0)),
        scratch_shapes=[pltpu.VMEM((2, sr, d), F32), pltpu.SemaphoreType.DMA((2,))],
    )
    return pl.pallas_call(
        functools.partial(_combine_kernel, rt),
        grid_spec=grid_spec,
        out_shape=jax.ShapeDtypeStruct((n_rows, d), F32),
        compiler_params=_cparams(("arbitrary",)),
        name="moe_combine",
    )(*tables, eo, xres, pos, by_tile(pos), by_tile(gates[:, :TOP_K]), g2)


def _dest_tiles(dest, tm):
    n = dest.shape[0]
    return dest.reshape(n // tm, tm, TOP_K).transpose(0, 2, 1).reshape(n // tm, 1, TOP_K * tm)


def _route(idx, counts, n_blocks):
    bm = MOE_BLOCK
    rt = ROUTE_TILE
    n = idx.shape[0]
    tile_before = jnp.cumsum(counts, axis=0) - counts
    total = jnp.sum(counts, axis=0)
    padded = (total + bm - 1) // bm * bm
    pad_end = jnp.cumsum(padded)
    pad_start = pad_end - padded
    first_row = tile_before + pad_start[None, :]
    early = first_row % SUBLANES
    size = jnp.where(counts > 0, (counts + early + SUBLANES - 1) // SUBLANES * SUBLANES, 0)
    stage_off = jnp.cumsum(size, axis=1) - size
    experts = idx[:, :TOP_K].reshape(n // rt, rt, TOP_K)
    rank = idx[:, TOP_K:2 * TOP_K]
    onehot = experts[..., None] == lax.broadcasted_iota(jnp.int32, (1, 1, 1, N_EXPERTS), 3)
    lookup = lambda tab: jnp.sum(jnp.where(onehot, tab[:, None, None, :], 0), axis=-1).reshape(n, TOP_K)
    dest = lookup(first_row) + rank
    pos = lookup(stage_off + early) + rank
    tables = tuple(t.reshape(-1).astype(jnp.int32) for t in (first_row - early, size, stage_off))
    block_start = jnp.arange(n_blocks, dtype=jnp.int32) * bm
    block_e = jnp.minimum(jnp.sum(pad_end[None, :] <= block_start[:, None], axis=1), N_EXPERTS - 1)
    n_used = (pad_end[-1] // bm).astype(jnp.int32).reshape(1)
    padding = ((pad_start + total).astype(jnp.int32), pad_end.astype(jnp.int32))
    return dest.astype(jnp.int32), pos.astype(jnp.int32), tables, block_e.astype(jnp.int32), n_used, padding


def _moe(hx, idx, counts, gates, xres, g2, layer, w1, b1, w2, b2, n_out_rows, n_lat, seq):
    n = hx.shape[0]
    n_blocks = -(-(n * TOP_K + N_EXPERTS * (MOE_BLOCK - 1)) // MOE_BLOCK) + 1
    per_tile = TOK_TILE // ROUTE_TILE
    counts = counts[:, :per_tile, :N_EXPERTS].reshape(n // ROUTE_TILE, N_EXPERTS)
    dest, pos, tables, block_e, n_used, (pad_lo, pad_hi) = _route(idx, counts, n_blocks)
    xs = _dispatch(hx, _dest_tiles(dest, DISPATCH_TILE), pad_lo, pad_hi, n_used, n_blocks, DISPATCH_TILE)
    eo = _experts(xs, n_blocks, block_e + layer * N_EXPERTS, n_used, w1, b1, w2, b2)
    n_rt = n_out_rows // ROUTE_TILE
    tables = tuple(t[:n_rt * N_EXPERTS] for t in tables)
    return _combine(eo, tables, pos[:n_out_rows], xres, gates, g2, n_out_rows, n_lat, seq)


def kernel(x, c, ctx, c_ctx, norm_mix_w, norm_ffn_w, w_ada, b_ada, w_in, mlstm_ig_b, mlstm_fg_b, mlstm_norm_w,
           na_qnorm_w, na_knorm_w, na_rpb, conv_w, conv_b, conv_ln_w, conv_ln_b, w_out, router_w, router_b,
           exp_w1, exp_b1, exp_w2, exp_b2):
    batch, seq, d = x.shape
    ctx_len = ctx.shape[1]
    depth = w_ada.shape[0]
    n_lat = batch * seq
    n_ctx = batch * ctx_len
    n_all = n_lat + n_ctx
    assert depth == 2
    assert seq % TOK_TILE == 0 and n_ctx % TOK_TILE == 0 and seq % ctx_len == 0
    assert n_lat % DISPATCH_TILE == 0 and n_all % DISPATCH_TILE == 0
    assert (seq // GRID_W) % NA_QROWS == 0 and seq // GRID_W >= NA_KROWS

    mod_rows = -(-(batch + 1) // SUBLANES) * SUBLANES
    cc = jnp.zeros((mod_rows, d), F32).at[:batch].set(c).at[batch].set(c_ctx)
    mods = _ada(cc, w_ada, b_ada)[:, :batch + 1].reshape(depth, batch + 1, 1, 6, d)
    x_parts = (x.reshape(n_lat, d), ctx.reshape(n_ctx, d))
    rope = _rope_tables(seq)
    n_exp = exp_w1.shape[1]
    ew1 = exp_w1.reshape((depth * n_exp,) + exp_w1.shape[2:])
    eb1 = exp_b1.reshape((depth * n_exp,) + exp_b1.shape[2:])
    ew2 = exp_w2.reshape((depth * n_exp,) + exp_w2.shape[2:])
    eb2 = exp_b2.reshape((depth * n_exp,) + exp_b2.shape[2:])

    for l in range(depth):
        need_ctx = l < depth - 1
        sh1, sc1, g1, sh2, sc2, g2 = [mods[l, :, :, i, :] for i in range(6)]
        pa, kt, g, gt, pb, u = _inproj(x_parts, sh1, sc1, norm_mix_w[l], w_in[l], n_lat, seq)
        a_lat, a_ctx = _mlstm(pa, kt, g, gt, mlstm_ig_b[l], mlstm_fg_b[l], mlstm_norm_w[l], rope,
                              batch, seq, ctx_len, need_ctx)
        b_lat, b_ctx = _na(pb, na_qnorm_w[l], na_knorm_w[l], na_rpb[l], batch, seq, ctx_len, need_ctx)
        c_lat = _conv(u, conv_w[l], conv_b[l], conv_ln_w[l], conv_ln_b[l], 0, batch, seq)
        if need_ctx:
            c_ctx_out = _conv(u, conv_w[l], conv_b[l], conv_ln_w[l], conv_ln_b[l], n_lat // ctx_len, batch, ctx_len)
            a_all, b_all, c_all = (a_lat, a_ctx), (b_lat, b_ctx), (c_lat, c_ctx_out)
            n_rows = n_all
        else:
            a_all, b_all, c_all = (a_lat,), (b_lat,), (c_lat,)
            n_rows = n_lat
        xmid, hx, idx, gates, counts = _outproj(a_all, b_all, c_all, x_parts, g1, sh2, sc2, norm_ffn_w[l], w_out[l],
                                                router_w[l], router_b[l], n_rows, n_lat, seq)
        xall = _moe(hx, idx, counts, gates, xmid, g2, l, ew1, eb1, ew2, eb2, n_rows, n_lat, seq)
        x_parts = (xall,)
    return x_parts[0][:n_lat].reshape(batch, seq, d)
```
